```python
import jax, jax.numpy as jnp
from jax import lax
import numpy as np

D_MODEL = 1024
BATCH = 2
SEQ = 8192
DEPTH = 1

D_MIX = D_MODEL
NSA_HEADS = 8
NSA_KV_HEADS = 2
NSA_HEAD_DIM = 64
NSA_WIDTH = NSA_HEADS * NSA_HEAD_DIM
NSA_KV_WIDTH = NSA_KV_HEADS * NSA_HEAD_DIM
CMP_LEN = 32
CMP_STRIDE = 16
CMP_HIDDEN = 2 * NSA_HEAD_DIM
SLC_BLOCK = 64
SLC_TOPN = 16
WINDOW = 512
Q_BLOCK = 128
HG_HEADS = 4
HG_KEY_DIM = 128
HG_VAL_DIM = 128
HG_WIDTH = HG_HEADS * HG_VAL_DIM
HG_CHUNK = 64
ROPE_THETA = 500000.0
ROT_DIM = NSA_HEAD_DIM // 4
D_FF = 4 * D_MODEL
EPS = 1e-6
NEG = -1e30
IN_SPLITS = (NSA_WIDTH, NSA_KV_WIDTH, NSA_KV_WIDTH, NSA_KV_WIDTH, NSA_KV_WIDTH,
             NSA_KV_WIDTH, NSA_KV_WIDTH, NSA_HEADS * 3,
             HG_HEADS * HG_KEY_DIM, HG_HEADS * HG_KEY_DIM, HG_WIDTH, HG_WIDTH)
D_IN = sum(IN_SPLITS)

kernel_name = 'hymba_nsa_hgrn2_block'


def rms_norm(x, gain):
    xf = x.astype(jnp.float32)
    y = xf * lax.rsqrt(jnp.mean(xf * xf, axis=-1, keepdims=True) + EPS)
    return (y * gain.astype(jnp.float32)).astype(x.dtype)


def rope(x, cos, sin):
    half = ROT_DIM // 2
    c, s = cos.astype(x.dtype), sin.astype(x.dtype)
    x1, x2 = x[..., :half], x[..., half:ROT_DIM]
    return jnp.concatenate([x1 * c - x2 * s, x2 * c + x1 * s, x[..., ROT_DIM:]], axis=-1)


def masked_softmax(s, mask):
    return jax.nn.softmax(jnp.where(mask, s, NEG), axis=-1) * mask


def cmp_to_slc_matrix(n_cmp, n_slc):
    tok = np.arange(n_cmp)[:, None] * CMP_STRIDE + np.arange(CMP_LEN)[None, :]
    blk = tok // SLC_BLOCK
    return (blk[:, :, None] == np.arange(n_slc)[None, None, :]).mean(axis=1).astype(np.float32)


def nsa_mixer(q, kc, vc, ks, vs, kw, vw, gates, q_gain, k_gain,
              pe_k, w1_k, w2_k, pe_v, w1_v, w2_v, cos, sin):
    B, S, _ = q.shape
    G, HPG, DK = NSA_KV_HEADS, NSA_HEADS // NSA_KV_HEADS, NSA_HEAD_DIM
    dtype = q.dtype
    q = q.reshape(B, S, G, HPG, DK).transpose(0, 2, 3, 1, 4)
    q = rope(rms_norm(q, q_gain), cos, sin)

    def kv_heads(t):
        return t.reshape(B, S, G, DK).transpose(0, 2, 1, 3)

    def prep_k(t):
        return rope(rms_norm(kv_heads(t), k_gain), cos, sin)

    kc, ks, kw = prep_k(kc), prep_k(ks), prep_k(kw)
    vc, vs, vw = kv_heads(vc), kv_heads(vs), kv_heads(vw)
    gates = jax.nn.sigmoid(gates.astype(jnp.float32)).reshape(B, S, G, HPG, 3).transpose(0, 2, 3, 1, 4)

    n_cmp = (S - CMP_LEN) // CMP_STRIDE + 1
    tok_idx = jnp.arange(n_cmp)[:, None] * CMP_STRIDE + jnp.arange(CMP_LEN)[None, :]

    def compress(t, pe, w1, w2):
        blocks = t[:, :, tok_idx] + pe
        flat = blocks.reshape(B, G, n_cmp, CMP_LEN * DK)
        return jax.nn.silu(flat @ w1) @ w2

    kc = compress(kc, pe_k, w1_k, w2_k)
    vc = compress(vc, pe_v, w1_v, w2_v)
    cmp_end = jnp.arange(n_cmp) * CMP_STRIDE + CMP_LEN - 1

    n_slc = S // SLC_BLOCK
    n_top = min(SLC_TOPN, n_slc)
    ks_blocks = ks.reshape(B, G, n_slc, SLC_BLOCK, DK)
    vs_blocks = vs.reshape(B, G, n_slc, SLC_BLOCK, DK)
    slc_map = jnp.asarray(cmp_to_slc_matrix(n_cmp, n_slc))
    kw_pad = jnp.pad(kw, ((0, 0), (0, 0), (WINDOW, 0), (0, 0)))
    vw_pad = jnp.pad(vw, ((0, 0), (0, 0), (WINDOW, 0), (0, 0)))
    scale = DK ** -0.5
    gather = jax.vmap(jax.vmap(lambda blocks, ix: blocks[ix]))
    j_blk = jnp.arange(n_slc)

    def block_fn(blk):
        q0 = blk * Q_BLOCK
        qb = lax.dynamic_slice_in_dim(q, q0, Q_BLOCK, axis=3)
        gb = lax.dynamic_slice_in_dim(gates, q0, Q_BLOCK, axis=3)
        t = q0 + jnp.arange(Q_BLOCK)
        s = jnp.einsum('bghqd,bgnd->bghqn', qb, kc).astype(jnp.float32) * scale
        p_cmp = masked_softmax(s, cmp_end[None, :] <= t[:, None])
        o_cmp = jnp.einsum('bghqn,bgnd->bghqd', p_cmp.astype(dtype), vc)
        imp = jnp.einsum('bghqn,nj->bgqj', p_cmp, slc_map)
        cur = t // SLC_BLOCK
        forced = (j_blk[None, :] == 0) | (j_blk[None, :] == cur[:, None]) | (j_blk[None, :] == cur[:, None] - 1)
        valid = j_blk[None, :] <= cur[:, None]
        imp = jnp.where(forced, jnp.inf, jnp.where(valid, imp, -jnp.inf))
        _, idx = lax.top_k(imp, n_top)
        k_sel = gather(ks_blocks, idx).reshape(B, G, Q_BLOCK, n_top * SLC_BLOCK, DK)
        v_sel = gather(vs_blocks, idx).reshape(B, G, Q_BLOCK, n_top * SLC_BLOCK, DK)
        pos = (idx[..., None] * SLC_BLOCK + jnp.arange(SLC_BLOCK)).reshape(B, G, Q_BLOCK, n_top * SLC_BLOCK)
        s = jnp.einsum('bghqd,bgqkd->bghqk', qb, k_sel).astype(jnp.float32) * scale
        p = masked_softmax(s, (pos <= t[:, None])[:, :, None])
        o_slc = jnp.einsum('bghqk,bgqkd->bghqd', p.astype(dtype), v_sel)
        kwb = lax.dynamic_slice_in_dim(kw_pad, q0, Q_BLOCK + WINDOW, axis=2)
        vwb = lax.dynamic_slice_in_dim(vw_pad, q0, Q_BLOCK + WINDOW, axis=2)
        kpos = q0 - WINDOW + jnp.arange(Q_BLOCK + WINDOW)
        wmask = (kpos[None, :] <= t[:, None]) & (kpos[None, :] > t[:, None] - WINDOW) & (kpos[None, :] >= 0)
        s = jnp.einsum('bghqd,bgkd->bghqk', qb, kwb).astype(jnp.float32) * scale
        p = masked_softmax(s, wmask)
        o_win = jnp.einsum('bghqk,bgkd->bghqd', p.astype(dtype), vwb)
        return (gb[..., 0:1] * o_cmp + gb[..., 1:2] * o_slc + gb[..., 2:3] * o_win).astype(dtype)

    out = lax.map(block_fn, jnp.arange(S // Q_BLOCK))
    return out.transpose(1, 0, 4, 2, 3, 5).reshape(B, S, NSA_WIDTH)


def hgrn2_mixer(q, f_logit, i, g, lb, out_gain):
    B, S, _ = q.shape
    H, DK, DV, C = HG_HEADS, HG_KEY_DIM, HG_VAL_DIM, HG_CHUNK
    n_chunk = S // C
    lbf = lb.astype(jnp.float32)
    f = lbf + (1.0 - lbf) * jax.nn.sigmoid(f_logit.astype(jnp.float32))
    logf = jnp.log(f)
    k = 1.0 - f
    qf = jax.nn.silu(q.astype(jnp.float32))
    v = i.astype(jnp.float32)

    def chunks(t, d):
        return t.reshape(B, n_chunk, C, H, d).transpose(1, 0, 3, 2, 4)

    causal = jnp.tril(jnp.ones((C, C), bool))

    def step(state, xs):
        qc, kc, vc, lc = xs
        bc = jnp.cumsum(lc, axis=2)
        diff = jnp.where(causal[:, :, None], bc[:, :, :, None, :] - bc[:, :, None, :, :], -jnp.inf)
        a = jnp.einsum('bhtd,bhsd,bhtsd->bhts', qc, kc, jnp.exp(diff))
        o = jnp.einsum('bhtd,bhdv->bhtv', qc * jnp.exp(bc), state) + jnp.einsum('bhts,bhsv->bhtv', a, vc)
        b_last = bc[:, :, -1:, :]
        state = jnp.exp(b_last[:, :, 0, :])[..., None] * state + jnp.einsum('bhsd,bhsv->bhdv', kc * jnp.exp(b_last - bc), vc)
        return state, o

    s0 = jnp.zeros((B, H, DK, DV), jnp.float32)
    _, o = lax.scan(step, s0, (chunks(qf, DK), chunks(k, DK), chunks(v, DV), chunks(logf, DK)))
    o = o.transpose(1, 0, 3, 2, 4).reshape(B, S, H, DV)
    o = rms_norm(o, out_gain) * jax.nn.silu(g.astype(jnp.float32).reshape(B, S, H, DV))
    return o.reshape(B, S, HG_WIDTH).astype(q.dtype)


def setup_inputs(seed: int = 0) -> dict:
    key = jax.random.key(seed)
    ks = jax.random.split(key, 18)
    f32 = jnp.float32

    def w(k, shape, fan_in):
        return jax.random.normal(k, shape, f32) * fan_in ** -0.5

    def gain(k, shape):
        return 1.0 + 0.02 * jax.random.normal(k, shape, f32)

    L = DEPTH
    return {
        'x': jax.random.normal(ks[0], (BATCH, SEQ, D_MODEL), f32),
        'attn_norm': gain(ks[1], (L, D_MODEL)),
        'w_in': w(ks[2], (L, D_MODEL, D_IN), D_MODEL),
        'q_norm': gain(ks[3], (L, NSA_HEAD_DIM)),
        'k_norm': gain(ks[4], (L, NSA_HEAD_DIM)),
        'cmp_pe_k': 0.02 * jax.random.normal(ks[5], (L, CMP_LEN, NSA_HEAD_DIM), f32),
        'cmp_w1_k': w(ks[6], (L, CMP_LEN * NSA_HEAD_DIM, CMP_HIDDEN), CMP_LEN * NSA_HEAD_DIM),
        'cmp_w2_k': w(ks[7], (L, CMP_HIDDEN, NSA_HEAD_DIM), CMP_HIDDEN),
        'cmp_pe_v': 0.02 * jax.random.normal(ks[8], (L, CMP_LEN, NSA_HEAD_DIM), f32),
        'cmp_w1_v': w(ks[9], (L, CMP_LEN * NSA_HEAD_DIM, CMP_HIDDEN), CMP_LEN * NSA_HEAD_DIM),
        'cmp_w2_v': w(ks[10], (L, CMP_HIDDEN, NSA_HEAD_DIM), CMP_HIDDEN),
        'nsa_out_norm': gain(ks[11], (L, NSA_WIDTH)),
        'hgrn_lb': 0.5 * jax.random.normal(ks[12], (L + 1, HG_HEADS * HG_KEY_DIM), f32),
        'hgrn_out_norm': gain(ks[13], (L, HG_VAL_DIM)),
        'w_out': w(ks[14], (L, D_MIX, D_MODEL), D_MIX),
        'mlp_norm': gain(ks[15], (L, D_MODEL)),
        'w_mlp_in': w(ks[16], (L, D_MODEL, D_FF), D_MODEL),
        'w_mlp_out': w(ks[17], (L, D_FF, D_MODEL), D_FF),
    }


def reference(x, attn_norm, w_in, q_norm, k_norm, cmp_pe_k, cmp_w1_k, cmp_w2_k,
              cmp_pe_v, cmp_w1_v, cmp_w2_v, nsa_out_norm, hgrn_lb, hgrn_out_norm,
              w_out, mlp_norm, w_mlp_in, w_mlp_out):
    B, S, _ = x.shape
    pos = jnp.arange(S, dtype=jnp.float32)
    inv_freq = ROPE_THETA ** (-jnp.arange(0, ROT_DIM, 2, dtype=jnp.float32) / ROT_DIM)
    ang = pos[:, None] * inv_freq[None, :]
    cos, sin = jnp.cos(ang), jnp.sin(ang)
    lb_all = jnp.cumsum(jax.nn.softmax(hgrn_lb.astype(jnp.float32), axis=0), axis=0)
    split_at = [int(v) for v in np.cumsum(IN_SPLITS)[:-1]]
    for l in range(DEPTH):
        h = rms_norm(x, attn_norm[l])
        proj = h @ w_in[l]
        (nq, nkc, nvc, nks, nvs, nkw, nvw, ngate, hq, hf, hi, hg) = jnp.split(proj, split_at, axis=-1)
        y_nsa = nsa_mixer(nq, nkc, nvc, nks, nvs, nkw, nvw, ngate, q_norm[l], k_norm[l],
                          cmp_pe_k[l], cmp_w1_k[l], cmp_w2_k[l], cmp_pe_v[l], cmp_w1_v[l], cmp_w2_v[l],
                          cos, sin)
        y_nsa = rms_norm(y_nsa, nsa_out_norm[l])
        y_hg = hgrn2_mixer(hq, hf, hi, hg, lb_all[l], hgrn_out_norm[l])
        x = x + jnp.concatenate([y_nsa, y_hg], axis=-1) @ w_out[l]
        h = rms_norm(x, mlp_norm[l])
        x = x + jnp.square(jax.nn.relu(h @ w_mlp_in[l])) @ w_mlp_out[l]
    return x
```

```python
import functools

import jax
import jax.numpy as jnp
import numpy as np
from jax import lax
from jax.experimental import pallas as pl
from jax.experimental.pallas import tpu as pltpu

F32 = jnp.float32
BF16 = jnp.bfloat16

NSA_HEADS = 8
NSA_GROUPS = 2
HEADS_PER_GROUP = NSA_HEADS // NSA_GROUPS
HEAD_DIM = 64
NSA_WIDTH = NSA_HEADS * HEAD_DIM
KV_WIDTH = NSA_GROUPS * HEAD_DIM
CMP_LEN = 32
CMP_STRIDE = 16
CMP_HIDDEN = 2 * HEAD_DIM
SLC_BLOCK = 64
SLC_SHIFT = 6
SLC_TOPN = 16
WINDOW = 512
ROT_DIM = HEAD_DIM // 4
ROPE_THETA = 500000.0
HG_HEADS = 4
HG_DIM = 128
HG_WIDTH = HG_HEADS * HG_DIM
EPS = 1e-6

LANES = 128
Q_TILE = 128
Q_COLS = HEADS_PER_GROUP * Q_TILE
SLC_KEYS = 256
SLC_LANES = 128
V_ROWS = 80
NSA_COLS = 1408
NEG = -1e30
BIG = 2.0 ** 100
HG_CHUNK = 128
VMEM_LIMIT = 56 * 1024 * 1024


def _cparams(n_grid):
    return pltpu.CompilerParams(
        dimension_semantics=("arbitrary",) * n_grid, vmem_limit_bytes=VMEM_LIMIT)


def _inproj_kernel(x_ref, g_ref, wn_ref, wh_ref, nsa_ref, hg_ref):
    x = x_ref[...]
    ms = jnp.mean(x * x, axis=-1, keepdims=True)
    h = (x * lax.rsqrt(ms + EPS) * g_ref[...]).astype(BF16)
    nsa_ref[...] = jnp.dot(h, wn_ref[...], preferred_element_type=F32)
    hg_ref[...] = jnp.dot(h, wh_ref[...], preferred_element_type=F32)


def _inproj(x2, gain, w_nsa, w_hg, tm=256):
    n, d = x2.shape
    const = lambda i: (0, 0)
    return pl.pallas_call(
        _inproj_kernel,
        grid=(n // tm,),
        in_specs=[
            pl.BlockSpec((tm, d), lambda i: (i, 0)),
            pl.BlockSpec((1, d), const),
            pl.BlockSpec(w_nsa.shape, const),
            pl.BlockSpec(w_hg.shape, const),
        ],
        out_specs=[
            pl.BlockSpec((tm, w_nsa.shape[1]), lambda i: (i, 0)),
            pl.BlockSpec((tm, w_hg.shape[1]), lambda i: (i, 0)),
        ],
        out_shape=[
            jax.ShapeDtypeStruct((n, w_nsa.shape[1]), F32),
            jax.ShapeDtypeStruct((n, w_hg.shape[1]), F32),
        ],
        compiler_params=_cparams(1),
        name="inproj",
    )(x2, gain, w_nsa, w_hg)


def _nsa_prep_kernel(p_ref, cos_ref, sin_ref, qg_ref, kg_ref,
                     q_ref, kc_ref, ks_ref, kw_ref, v_ref, gate_ref):
    tq = p_ref.shape[0]
    cos = cos_ref[...]
    sin = sin_ref[...]
    lane = lax.broadcasted_iota(jnp.int32, (tq, LANES), 1)
    low_head = lane < HEAD_DIM
    dim = lane & (HEAD_DIM - 1)
    first_half = dim < ROT_DIM // 2

    def norm_rope(xs, gain, scale):
        x2 = xs * xs
        s_lo = jnp.sum(jnp.where(low_head, x2, 0.0), axis=-1, keepdims=True)
        s_hi = jnp.sum(jnp.where(low_head, 0.0, x2), axis=-1, keepdims=True)
        ms = jnp.where(low_head, s_lo, s_hi) * (1.0 / HEAD_DIM)
        y = xs * lax.rsqrt(ms + EPS) * gain
        partner = jnp.where(first_half,
                            pltpu.roll(y, LANES - ROT_DIM // 2, 1),
                            pltpu.roll(y, ROT_DIM // 2, 1))
        out = y * cos + partner * sin
        return (out * scale).astype(BF16)

    qg = qg_ref[...]
    kg = kg_ref[...]
    for s in range(NSA_WIDTH // LANES):
        q_ref[:, s * LANES:(s + 1) * LANES] = norm_rope(
            p_ref[:, s * LANES:(s + 1) * LANES], qg, HEAD_DIM ** -0.5)
    base = NSA_WIDTH
    kc_ref[...] = norm_rope(p_ref[:, base:base + LANES], kg, 1.0)
    ks_ref[...] = norm_rope(p_ref[:, base + LANES:base + 2 * LANES], kg, 1.0)
    kw_ref[...] = norm_rope(p_ref[:, base + 2 * LANES:base + 3 * LANES], kg, 1.0)
    base += 3 * LANES
    v_ref[...] = p_ref[:, base:base + 3 * LANES].astype(BF16)
    base += 3 * LANES
    gate_ref[...] = jax.nn.sigmoid(p_ref[:, base:base + LANES])


def _nsa_prep(nsa_proj, cos_t, sin_t, qg, kg, seq, tq=256):
    n = nsa_proj.shape[0]
    n_seq_tiles = seq // tq
    row = lambda i: (i, 0)
    tab = lambda i: (i % n_seq_tiles, 0)
    const = lambda i: (0, 0)
    widths = (NSA_WIDTH, LANES, LANES, LANES, 3 * LANES)
    return pl.pallas_call(
        _nsa_prep_kernel,
        grid=(n // tq,),
        in_specs=[
            pl.BlockSpec((tq, NSA_COLS), row),
            pl.BlockSpec((tq, LANES), tab),
            pl.BlockSpec((tq, LANES), tab),
            pl.BlockSpec((1, LANES), const),
            pl.BlockSpec((1, LANES), const),
        ],
        out_specs=[pl.BlockSpec((tq, w), row) for w in widths]
        + [pl.BlockSpec((tq, LANES), row)],
        out_shape=[jax.ShapeDtypeStruct((n, w), BF16) for w in widths]
        + [jax.ShapeDtypeStruct((n, LANES), F32)],
        compiler_params=_cparams(1),
        name="nsa_prep",
    )(nsa_proj, cos_t, sin_t, qg, kg)


def _compress_kernel(tk_ref, tv_ref, pek_ref, pev_ref, w1k_ref, w1v_ref,
                     w2k_ref, w2v_ref, ok_ref, ov_ref):
    half = CMP_STRIDE * HEAD_DIM

    def run(t_ref, pe_ref, w1_ref, w2_ref, o_ref):
        t = t_ref[...]
        n_rows = t.shape[0]
        top = jnp.dot(t, w1_ref[0:half, :], preferred_element_type=F32)
        bot = jnp.dot(t, w1_ref[half:2 * half, :], preferred_element_type=F32)
        bot_next = pltpu.roll(bot, n_rows - 1, 0)
        pe = jnp.dot(pe_ref[...], w1_ref[...], preferred_element_type=F32)[0:1, :]
        h = top + bot_next + pe
        h = h * jax.nn.sigmoid(h)
        o = jnp.dot(h.astype(BF16), w2_ref[...], preferred_element_type=F32)
        r = lax.broadcasted_iota(jnp.int32, o.shape, 0)
        o_ref[...] = jnp.where(r < n_rows - 1, o, 0.0).astype(BF16)

    run(tk_ref, pek_ref, w1k_ref, w2k_ref, ok_ref)
    run(tv_ref, pev_ref, w1v_ref, w2v_ref, ov_ref)


def _compress(tk, tv, pek, pev, w1k, w1v, w2k, w2v):
    bg, n_rows, width = tk.shape
    blk = lambda i: (i, 0, 0)
    const = lambda i: (0, 0)
    return pl.pallas_call(
        _compress_kernel,
        grid=(bg,),
        in_specs=[
            pl.BlockSpec((None, n_rows, width), blk),
            pl.BlockSpec((None, n_rows, width), blk),
            pl.BlockSpec(pek.shape, const),
            pl.BlockSpec(pev.shape, const),
            pl.BlockSpec(w1k.shape, const),
            pl.BlockSpec(w1v.shape, const),
            pl.BlockSpec(w2k.shape, const),
            pl.BlockSpec(w2v.shape, const),
        ],
        out_specs=[pl.BlockSpec((None, n_rows, HEAD_DIM), blk)] * 2,
        out_shape=[jax.ShapeDtypeStruct((bg, n_rows, HEAD_DIM), BF16)] * 2,
        compiler_params=_cparams(1),
        name="compress",
    )(tk, tv, pek, pev, w1k, w1v, w2k, w2v)


def _nsa_attn_kernel(qt_ref, gt_ref, kc_ref, vct_ref, mapt_ref, ks_ref, vst_ref,
                     kw_ref, vwt_ref, y_ref, qaug_ref):
    i = pl.program_id(2)
    q0 = i * Q_TILE
    qt = qt_ref[...]
    col = lax.broadcasted_iota(jnp.int32, (1, Q_COLS), 1)
    tpos = q0 + (col & (Q_TILE - 1))

    n_cmp_rows = kc_ref.shape[0]
    s = jnp.dot(kc_ref[...], qt, preferred_element_type=F32)
    cmp_end = lax.broadcasted_iota(jnp.int32, (n_cmp_rows, 1), 0) * CMP_STRIDE + (CMP_LEN - 1)
    vis = cmp_end <= tpos
    s = jnp.where(vis, s, NEG)
    m = jnp.max(s, axis=0, keepdims=True)
    p = jnp.where(vis, jnp.exp(s - m), 0.0)
    acc = jnp.dot(vct_ref[...], p.astype(BF16), preferred_element_type=F32)
    l = jnp.sum(p, axis=0, keepdims=True)
    inv = jnp.where(l > 0.0, 1.0 / l, 0.0)
    o_cmp = acc[0:HEAD_DIM, :] * inv

    pn = p * inv
    psum = pn[:, 0:Q_TILE]
    for h in range(1, HEADS_PER_GROUP):
        psum = psum + pn[:, h * Q_TILE:(h + 1) * Q_TILE]
    p_hi = psum.astype(BF16)
    p_lo = (psum - p_hi.astype(F32)).astype(BF16)
    mapt = mapt_ref[...]
    imp = (jnp.dot(mapt, p_hi, preferred_element_type=F32)
           + jnp.dot(mapt, p_lo, preferred_element_type=F32))

    blk_i = lax.broadcasted_iota(jnp.int32, (SLC_LANES, Q_TILE), 0)
    cur = (q0 + lax.broadcasted_iota(jnp.int32, (SLC_LANES, Q_TILE), 1)) >> SLC_SHIFT
    blk = blk_i.astype(F32)
    val = jnp.where(blk_i <= cur, imp, -1.0)
    for forced in (blk_i == 0, blk_i == cur, blk_i == cur - 1):
        val = jnp.where(forced, 1e30, val)
    sel = jnp.zeros((SLC_LANES, Q_TILE), F32)
    for _ in range(SLC_TOPN):
        mx = jnp.max(val, axis=0, keepdims=True)
        first = jnp.min(jnp.where(val == mx, blk, float(SLC_LANES)), axis=0, keepdims=True)
        first = jnp.where(mx > -0.5, first, -1.0)
        pick = blk == first
        sel = jnp.where(pick, 1.0, sel)
        val = jnp.where(pick, -1.0, val)
    sel_bias = jnp.where(sel > 0.5, 0.0, -BIG).astype(BF16)

    qaug_ref[0:HEAD_DIM, :] = qt
    for h in range(HEADS_PER_GROUP):
        qaug_ref[HEAD_DIM:HEAD_DIM + SLC_LANES, h * Q_TILE:(h + 1) * Q_TILE] = sel_bias

    def slc_update(k0, carry, masked):
        m_prev, acc_prev = carry
        sc = jnp.dot(ks_ref[pl.ds(k0, SLC_KEYS), :], qaug_ref[...],
                     preferred_element_type=F32)
        if masked:
            kpos = k0 + lax.broadcasted_iota(jnp.int32, (SLC_KEYS, 1), 0)
            sc = jnp.where(kpos <= tpos, sc, -BIG)
        m_new = jnp.maximum(m_prev, jnp.max(sc, axis=0, keepdims=True))
        alpha = jnp.exp(m_prev - m_new)
        pr = jnp.exp(sc - m_new).astype(BF16)
        pv = jnp.dot(vst_ref[:, pl.ds(k0, SLC_KEYS)], pr, preferred_element_type=F32)
        return m_new, acc_prev * alpha + pv

    n_full = q0 // SLC_KEYS
    carry = (jnp.full((1, Q_COLS), NEG, F32), jnp.zeros((V_ROWS, Q_COLS), F32))
    carry = lax.fori_loop(
        0, n_full,
        lambda kt, c: slc_update(pl.multiple_of(kt * SLC_KEYS, SLC_KEYS), c, False),
        carry)
    _, acc = slc_update(pl.multiple_of(n_full * SLC_KEYS, SLC_KEYS), carry, True)
    o_slc = acc[0:HEAD_DIM, :] / acc[HEAD_DIM:HEAD_DIM + 1, :]

    span = WINDOW + Q_TILE
    start = pl.multiple_of(jnp.maximum(q0 - WINDOW, 0), Q_TILE)
    sw = jnp.dot(kw_ref[pl.ds(start, span), :], qt, preferred_element_type=F32)
    kpos = start + lax.broadcasted_iota(jnp.int32, (span, 1), 0)
    sw = jnp.where(kpos <= tpos, jnp.where(kpos > tpos - WINDOW, sw, NEG), NEG)
    mw = jnp.max(sw, axis=0, keepdims=True)
    pw = jnp.exp(sw - mw).astype(BF16)
    accw = jnp.dot(vwt_ref[:, pl.ds(start, span)], pw, preferred_element_type=F32)
    o_win = accw[0:HEAD_DIM, :] / accw[HEAD_DIM:HEAD_DIM + 1, :]

    g = gt_ref[...]

    def gate_row(branch):
        return jnp.concatenate(
            [g[h * 3 + branch:h * 3 + branch + 1, :] for h in range(HEADS_PER_GROUP)], axis=1)

    y_ref[...] = gate_row(0) * o_cmp + gate_row(1) * o_slc + gate_row(2) * o_win


def _nsa_attn(qt, gt, kc, vct, mapt, ks, vst, kw, vwt):
    b, g, nt = qt.shape[:3]
    seq = ks.shape[2]
    n_cmp_rows = kc.shape[2]
    per_bg = lambda bi, gi, i: (bi, gi, 0, 0)
    return pl.pallas_call(
        _nsa_attn_kernel,
        grid=(b, g, nt),
        in_specs=[
            pl.BlockSpec((None, None, None, HEAD_DIM, Q_COLS), lambda bi, gi, i: (bi, gi, i, 0, 0)),
            pl.BlockSpec((None, None, 16, Q_TILE), lambda bi, gi, i: (bi, gi, 0, i)),
            pl.BlockSpec((None, None, n_cmp_rows, HEAD_DIM), per_bg),
            pl.BlockSpec((None, None, V_ROWS, n_cmp_rows), per_bg),
            pl.BlockSpec(mapt.shape, lambda bi, gi, i: (0, 0)),
            pl.BlockSpec((None, None, seq, HEAD_DIM + SLC_LANES), per_bg),
            pl.BlockSpec((None, None, V_ROWS, seq), per_bg),
            pl.BlockSpec((None, None, seq, HEAD_DIM), per_bg),
            pl.BlockSpec((None, None, V_ROWS, seq), per_bg),
        ],
        out_specs=pl.BlockSpec((None, None, None, HEAD_DIM, Q_COLS),
                               lambda bi, gi, i: (bi, gi, i, 0, 0)),
        out_shape=jax.ShapeDtypeStruct((b, g, nt, HEAD_DIM, Q_COLS), F32),
        scratch_shapes=[pltpu.VMEM((HEAD_DIM + SLC_LANES, Q_COLS), BF16)],
        compiler_params=_cparams(3),
        name="nsa_attn",
    )(qt, gt, kc, vct, mapt, ks, vst, kw, vwt)


def _segmented_scans(g):
    rows = g.shape[0]
    row = lax.broadcasted_iota(jnp.int32, g.shape, 0)
    out = {1: (g, g)}
    pre, suf = g, g
    h = 1
    while h < rows:
        pos = row & (2 * h - 1)
        if h < 8:
            add_pre = jnp.zeros_like(g)
            add_suf = jnp.zeros_like(g)
            for shift in range(1, h + 1):
                add_pre = jnp.where(pos == h + shift - 1, pltpu.roll(pre, shift, 0), add_pre)
                add_suf = jnp.where(pos == h - shift, pltpu.roll(suf, rows - shift, 0), add_suf)
            pre = pre + add_pre
            suf = suf + add_suf
        else:
            pre_parts, suf_parts = [], []
            for b in range(rows // (2 * h)):
                lo, mid, hi = b * 2 * h, b * 2 * h + h, (b + 1) * 2 * h
                pre_parts += [pre[lo:mid], pre[mid:hi] + pre[mid - 1:mid, :]]
                suf_parts += [suf[lo:mid] + suf[mid:mid + 1, :], suf[mid:hi]]
            pre = jnp.concatenate(pre_parts, axis=0)
            suf = jnp.concatenate(suf_parts, axis=0)
        h *= 2
        out[h] = (pre, suf)
    return out


def _hgrn_kernel(hg_ref, lb_ref, gain_ref, o_ref, state_ref):
    c = pl.program_id(1)

    @pl.when(c == 0)
    def _():
        state_ref[...] = jnp.zeros_like(state_ref)

    w = HG_WIDTH
    q = hg_ref[:, 0:w]
    f_logit = hg_ref[:, w:2 * w]
    v = hg_ref[:, 2 * w:3 * w]
    og = hg_ref[:, 3 * w:4 * w]
    lb = lb_ref[...]
    f = lb + (1.0 - lb) * jax.nn.sigmoid(f_logit)
    logf = jnp.log(f)
    k = 1.0 - f
    qf = q * jax.nn.sigmoid(q)
    vb = v.astype(BF16)

    scans = _segmented_scans(logf)
    rows = HG_CHUNK
    t_idx = lax.broadcasted_iota(jnp.int32, (rows, rows), 0)
    s_idx = lax.broadcasted_iota(jnp.int32, (rows, rows), 1)
    diff = jnp.where(t_idx > s_idx, t_idx ^ s_idx, 0)

    def heads(x):
        return [x[:, h * HG_DIM:(h + 1) * HG_DIM] for h in range(HG_HEADS)]

    nt_dims = (((1,), (1,)), ((), ()))
    qb, kb = heads(qf.astype(BF16)), heads(k.astype(BF16))
    a = [jnp.where(t_idx == s_idx,
                   lax.dot_general(qb[h], kb[h], nt_dims, preferred_element_type=F32), 0.0)
         for h in range(HG_HEADS)]
    level, h_size = 0, 1
    while h_size < rows:
        pre, suf = scans[h_size]
        ql = heads((qf * jnp.exp(pre)).astype(BF16))
        kl = heads((k * jnp.exp(suf - logf)).astype(BF16))
        mask = (diff >> level) == 1
        for h in range(HG_HEADS):
            al = lax.dot_general(ql[h], kl[h], nt_dims, preferred_element_type=F32)
            a[h] = jnp.where(mask, al, a[h])
        level += 1
        h_size *= 2

    cum, rev = scans[rows]
    q_in = heads((qf * jnp.exp(cum)).astype(BF16))
    k_out = heads((k * jnp.exp(rev - logf)).astype(BF16))
    vh = heads(vb)
    gain = gain_ref[...]
    for h in range(HG_HEADS):
        lanes = slice(h * HG_DIM, (h + 1) * HG_DIM)
        state_t = state_ref[h]
        o = jnp.dot(a[h].astype(BF16), vh[h], preferred_element_type=F32)
        o = o + lax.dot_general(q_in[h], state_t.astype(BF16), nt_dims,
                                preferred_element_type=F32)
        decay = jnp.exp(cum[rows - 1:rows, lanes])
        v_t = v[:, lanes].T.astype(BF16)
        state_ref[h] = state_t * decay + jnp.dot(v_t, k_out[h], preferred_element_type=F32)
        ms = jnp.mean(o * o, axis=-1, keepdims=True)
        gate = og[:, lanes]
        o_ref[:, lanes] = o * lax.rsqrt(ms + EPS) * gain[:, lanes] * (gate * jax.nn.sigmoid(gate))


def _hgrn(hg_proj, lb, gain, batch, seq):
    n = hg_proj.shape[0]
    n_chunk = seq // HG_CHUNK
    row = lambda b, c: (b * n_chunk + c, 0)
    const = lambda b, c: (0, 0)
    return pl.pallas_call(
        _hgrn_kernel,
        grid=(batch, n_chunk),
        in_specs=[
            pl.BlockSpec((HG_CHUNK, 4 * HG_WIDTH), row),
            pl.BlockSpec((1, HG_WIDTH), const),
            pl.BlockSpec((1, HG_WIDTH), const),
        ],
        out_specs=pl.BlockSpec((HG_CHUNK, HG_WIDTH), row),
        out_shape=jax.ShapeDtypeStruct((n, HG_WIDTH), F32),
        scratch_shapes=[pltpu.VMEM((HG_HEADS, HG_DIM, HG_DIM), F32)],
        compiler_params=_cparams(2),
        name="hgrn",
    )(hg_proj, lb, gain)


def _out_mlp_kernel(x_ref, yn_ref, yh_ref, gn_ref, gm_ref, wa_ref, wb_ref,
                    w1_ref, w2_ref, o_ref, *, ff_chunk):
    yn = yn_ref[...]
    ms = jnp.mean(yn * yn, axis=-1, keepdims=True)
    yn = (yn * lax.rsqrt(ms + EPS) * gn_ref[...]).astype(BF16)
    x1 = (x_ref[...]
          + jnp.dot(yn, wa_ref[...], preferred_element_type=F32)
          + jnp.dot(yh_ref[...].astype(BF16), wb_ref[...], preferred_element_type=F32))
    ms = jnp.mean(x1 * x1, axis=-1, keepdims=True)
    h = (x1 * lax.rsqrt(ms + EPS) * gm_ref[...]).astype(BF16)
    mlp = None
    for c in range(w1_ref.shape[1] // ff_chunk):
        cols = slice(c * ff_chunk, (c + 1) * ff_chunk)
        u = jnp.maximum(jnp.dot(h, w1_ref[:, cols], preferred_element_type=F32), 0.0)
        d = jnp.dot((u * u).astype(BF16), w2_ref[cols, :], preferred_element_type=F32)
        mlp = d if mlp is None else mlp + d
    o_ref[...] = x1 + mlp


def _out_mlp(x2, y_nsa, y_hg, gn, gm, wa, wb, w1, w2, tm=256, ff_chunk=1024):
    n, d = x2.shape
    row = lambda i: (i, 0)
    const = lambda i: (0, 0)

    def resident(shape):
        return pl.BlockSpec(shape, const)

    return pl.pallas_call(
        functools.partial(_out_mlp_kernel, ff_chunk=ff_chunk),
        grid=(n // tm,),
        in_specs=[
            pl.BlockSpec((tm, d), row),
            pl.BlockSpec((tm, y_nsa.shape[1]), row),
            pl.BlockSpec((tm, y_hg.shape[1]), row),
            pl.BlockSpec((1, y_nsa.shape[1]), const),
            pl.BlockSpec((1, d), const),
            resident(wa.shape),
            resident(wb.shape),
            resident(w1.shape),
            resident(w2.shape),
        ],
        out_specs=pl.BlockSpec((tm, d), row),
        out_shape=jax.ShapeDtypeStruct((n, d), F32),
        compiler_params=_cparams(1),
        name="out_mlp",
    )(x2, y_nsa, y_hg, gn, gm, wa, wb, w1, w2)


def _rope_tables(seq):
    pos = jnp.arange(seq, dtype=F32)
    inv_freq = ROPE_THETA ** (-jnp.arange(0, ROT_DIM, 2, dtype=F32) / ROT_DIM)
    ang = pos[:, None] * inv_freq[None, :]
    cos, sin = jnp.cos(ang), jnp.sin(ang)
    rest = HEAD_DIM - ROT_DIM
    cos_h = jnp.concatenate([cos, cos, jnp.ones((seq, rest), F32)], axis=-1)
    sin_h = jnp.concatenate([-sin, sin, jnp.zeros((seq, rest), F32)], axis=-1)
    return jnp.tile(cos_h, (1, 2)), jnp.tile(sin_h, (1, 2))


def _cmp_to_slc_t(n_cmp_rows, n_cmp, n_slc):
    tok = np.arange(n_cmp)[:, None] * CMP_STRIDE + np.arange(CMP_LEN)[None, :]
    frac = (tok[:, :, None] // SLC_BLOCK == np.arange(n_slc)[None, None, :]).mean(axis=1)
    out = np.zeros((SLC_LANES, n_cmp_rows), np.float32)
    out[:n_slc, :n_cmp] = frac.T
    return jnp.asarray(out, BF16)


def _values_t(v, batch, seq):
    vt = v.reshape(batch, seq, NSA_GROUPS, HEAD_DIM).transpose(0, 2, 3, 1)
    ones = jnp.ones((batch, NSA_GROUPS, 1, seq), v.dtype)
    pad = jnp.zeros((batch, NSA_GROUPS, V_ROWS - HEAD_DIM - 1, seq), v.dtype)
    return jnp.concatenate([vt, ones, pad], axis=2)


def kernel(x, attn_norm, w_in, q_norm, k_norm, cmp_pe_k, cmp_w1_k, cmp_w2_k, cmp_pe_v,
           cmp_w1_v, cmp_w2_v, nsa_out_norm, hgrn_lb, hgrn_out_norm, w_out, mlp_norm,
           w_mlp_in, w_mlp_out):
    batch, seq, d_model = x.shape
    n = batch * seq
    n_slc = seq // SLC_BLOCK
    n_cmp_rows = seq // CMP_STRIDE
    n_cmp = (seq - CMP_LEN) // CMP_STRIDE + 1
    assert n_slc <= SLC_LANES and seq % SLC_KEYS == 0 and seq >= WINDOW + Q_TILE
    assert w_in.shape[0] == 1, "single-layer block"

    kvw = KV_WIDTH
    o_q, o_kc, o_vc, o_ks, o_vs, o_kw, o_vw, o_gate = np.cumsum(
        [0, NSA_WIDTH, kvw, kvw, kvw, kvw, kvw, kvw])
    o_hg = o_gate + NSA_HEADS * 3
    w = w_in[0]
    pad = jnp.zeros((d_model, NSA_COLS - (NSA_WIDTH + 6 * kvw + NSA_HEADS * 3)), w.dtype)
    w_nsa = jnp.concatenate(
        [w[:, o_q:o_kc], w[:, o_kc:o_vc], w[:, o_ks:o_vs], w[:, o_kw:o_vw],
         w[:, o_vc:o_ks], w[:, o_vs:o_kw], w[:, o_vw:o_gate], w[:, o_gate:o_hg], pad],
        axis=1).astype(BF16)
    w_hg = w[:, o_hg:].astype(BF16)
    x2 = x.reshape(n, d_model)
    nsa_proj, hg_proj = _inproj(x2, attn_norm[0][None, :], w_nsa, w_hg)

    cos_t, sin_t = _rope_tables(seq)
    qg = jnp.tile(q_norm[0], 2)[None, :]
    kg = jnp.tile(k_norm[0], 2)[None, :]
    q, kc, ks, kw, v3, gates = _nsa_prep(nsa_proj, cos_t, sin_t, qg, kg, seq)

    nt = seq // Q_TILE
    qt = (q.reshape(batch, nt, Q_TILE, NSA_GROUPS, HEADS_PER_GROUP, HEAD_DIM)
          .transpose(0, 3, 1, 5, 4, 2).reshape(batch, NSA_GROUPS, nt, HEAD_DIM, Q_COLS))
    gt = (gates[:, :NSA_HEADS * 3].reshape(batch, seq, NSA_GROUPS, HEADS_PER_GROUP * 3)
          .transpose(0, 2, 3, 1))
    gt = jnp.pad(gt, ((0, 0), (0, 0), (0, 16 - HEADS_PER_GROUP * 3), (0, 0)))

    def by_group(t):
        return t.reshape(batch, seq, NSA_GROUPS, HEAD_DIM).transpose(0, 2, 1, 3)

    vc, vs, vw = v3[:, 0:kvw], v3[:, kvw:2 * kvw], v3[:, 2 * kvw:3 * kvw]
    blocks = lambda t: by_group(t).reshape(batch * NSA_GROUPS, n_cmp_rows, CMP_STRIDE * HEAD_DIM)
    pe8 = lambda pe: jnp.broadcast_to(pe.reshape(1, CMP_LEN * HEAD_DIM), (8, CMP_LEN * HEAD_DIM)).astype(BF16)
    k_cmp, v_cmp = _compress(
        blocks(kc), blocks(vc), pe8(cmp_pe_k[0]), pe8(cmp_pe_v[0]),
        cmp_w1_k[0].astype(BF16), cmp_w1_v[0].astype(BF16),
        cmp_w2_k[0].astype(BF16), cmp_w2_v[0].astype(BF16))
    k_cmp = k_cmp.reshape(batch, NSA_GROUPS, n_cmp_rows, HEAD_DIM)
    v_cmp_t = v_cmp.reshape(batch, NSA_GROUPS, n_cmp_rows, HEAD_DIM).transpose(0, 1, 3, 2)
    v_cmp_t = jnp.concatenate(
        [v_cmp_t, jnp.zeros((batch, NSA_GROUPS, V_ROWS - HEAD_DIM, n_cmp_rows), BF16)], axis=2)

    one_hot = (jnp.arange(seq)[:, None] // SLC_BLOCK == jnp.arange(SLC_LANES)[None, :]).astype(BF16)
    ks_aug = jnp.concatenate(
        [by_group(ks), jnp.broadcast_to(one_hot, (batch, NSA_GROUPS, seq, SLC_LANES))], axis=-1)
    y_t = _nsa_attn(qt, gt, k_cmp, v_cmp_t, _cmp_to_slc_t(n_cmp_rows, n_cmp, n_slc),
                    ks_aug, _values_t(vs, batch, seq), by_group(kw), _values_t(vw, batch, seq))
    y_nsa = (y_t.reshape(batch, NSA_GROUPS, nt, HEAD_DIM, HEADS_PER_GROUP, Q_TILE)
             .transpose(0, 2, 5, 1, 4, 3).reshape(n, NSA_WIDTH))

    lb_all = jnp.cumsum(jax.nn.softmax(hgrn_lb.astype(F32), axis=0), axis=0)
    y_hg = _hgrn(hg_proj, lb_all[0][None, :], jnp.tile(hgrn_out_norm[0], HG_HEADS)[None, :],
                 batch, seq)

    wo = w_out[0].astype(BF16)
    out = _out_mlp(x2, y_nsa, y_hg, nsa_out_norm[0][None, :], mlp_norm[0][None, :],
                   wo[:NSA_WIDTH], wo[NSA_WIDTH:], w_mlp_in[0].astype(BF16),
                   w_mlp_out[0].astype(BF16))
    return out.reshape(batch, seq, d_model)
```

```python
import functools

import jax
import jax.numpy as jnp
import numpy as np
from jax import lax
from jax.experimental import pallas as pl
from jax.experimental.pallas import tpu as pltpu

F32 = jnp.float32
BF16 = jnp.bfloat16

NSA_HEADS = 8
NSA_GROUPS = 2
HEADS_PER_GROUP = NSA_HEADS // NSA_GROUPS
HEAD_DIM = 64
NSA_WIDTH = NSA_HEADS * HEAD_DIM
KV_WIDTH = NSA_GROUPS * HEAD_DIM
CMP_LEN = 32
CMP_STRIDE = 16
CMP_HIDDEN = 2 * HEAD_DIM
SLC_BLOCK = 64
SLC_SHIFT = 6
SLC_TOPN = 16
WINDOW = 512
ROT_DIM = HEAD_DIM // 4
ROPE_THETA = 500000.0
HG_HEADS = 4
HG_DIM = 128
HG_WIDTH = HG_HEADS * HG_DIM
EPS = 1e-6

LANES = 128
Q_TILE = 256
Q_COLS = HEADS_PER_GROUP * Q_TILE
SLC_KEYS = 512
SLC_LANES = 128
V_ROWS = 80
NSA_COLS = 1408
NEG = -1e30
Q_SCALE = HEAD_DIM ** -0.5 * 1.4426950408889634
BIG = 2.0 ** 100
HG_CHUNK = 128
VMEM_LIMIT = 56 * 1024 * 1024


def _cparams(n_grid):
    return pltpu.CompilerParams(
        dimension_semantics=("arbitrary",) * n_grid, vmem_limit_bytes=VMEM_LIMIT)


def _inproj_kernel(x_ref, g_ref, wn_ref, wh_ref, nsa_ref, hg_ref):
    x = x_ref[...]
    ms = jnp.mean(x * x, axis=-1, keepdims=True)
    h = (x * lax.rsqrt(ms + EPS) * g_ref[...]).astype(BF16)
    nsa_ref[...] = jnp.dot(h, wn_ref[...], preferred_element_type=F32)
    hg_ref[...] = jnp.dot(h, wh_ref[...], preferred_element_type=F32)


def _inproj(x2, gain, w_nsa, w_hg, tm=256):
    n, d = x2.shape
    const = lambda i: (0, 0)
    return pl.pallas_call(
        _inproj_kernel,
        grid=(n // tm,),
        in_specs=[
            pl.BlockSpec((tm, d), lambda i: (i, 0)),
            pl.BlockSpec((1, d), const),
            pl.BlockSpec(w_nsa.shape, const),
            pl.BlockSpec(w_hg.shape, const),
        ],
        out_specs=[
            pl.BlockSpec((tm, w_nsa.shape[1]), lambda i: (i, 0)),
            pl.BlockSpec((tm, w_hg.shape[1]), lambda i: (i, 0)),
        ],
        out_shape=[
            jax.ShapeDtypeStruct((n, w_nsa.shape[1]), F32),
            jax.ShapeDtypeStruct((n, w_hg.shape[1]), F32),
        ],
        compiler_params=_cparams(1),
        name="inproj",
    )(x2, gain, w_nsa, w_hg)


def _nsa_prep_kernel(p_ref, cos_ref, sin_ref, qg_ref, kg_ref,
                     q_ref, kc_ref, ks_ref, kw_ref, v_ref, gate_ref):
    tq = p_ref.shape[0]
    cos = cos_ref[...]
    sin = sin_ref[...]
    lane = lax.broadcasted_iota(jnp.int32, (tq, LANES), 1)
    low_head = lane < HEAD_DIM
    dim = lane & (HEAD_DIM - 1)
    first_half = dim < ROT_DIM // 2

    def norm_rope(xs, gain, scale):
        x2 = xs * xs
        s_lo = jnp.sum(jnp.where(low_head, x2, 0.0), axis=-1, keepdims=True)
        s_hi = jnp.sum(jnp.where(low_head, 0.0, x2), axis=-1, keepdims=True)
        ms = jnp.where(low_head, s_lo, s_hi) * (1.0 / HEAD_DIM)
        y = xs * lax.rsqrt(ms + EPS) * gain
        partner = jnp.where(first_half,
                            pltpu.roll(y, LANES - ROT_DIM // 2, 1),
                            pltpu.roll(y, ROT_DIM // 2, 1))
        out = y * cos + partner * sin
        return (out * scale).astype(BF16)

    qg = qg_ref[...]
    kg = kg_ref[...]
    for s in range(NSA_WIDTH // LANES):
        q_ref[:, s * LANES:(s + 1) * LANES] = norm_rope(
            p_ref[:, s * LANES:(s + 1) * LANES], qg, Q_SCALE)
    base = NSA_WIDTH
    kc_ref[...] = norm_rope(p_ref[:, base:base + LANES], kg, 1.0)
    ks_ref[...] = norm_rope(p_ref[:, base + LANES:base + 2 * LANES], kg, 1.0)
    kw_ref[...] = norm_rope(p_ref[:, base + 2 * LANES:base + 3 * LANES], kg, 1.0)
    base += 3 * LANES
    v_ref[...] = p_ref[:, base:base + 3 * LANES].astype(BF16)
    base += 3 * LANES
    gate_ref[...] = jax.nn.sigmoid(p_ref[:, base:base + LANES])


def _nsa_prep(nsa_proj, cos_t, sin_t, qg, kg, seq, tq=256):
    n = nsa_proj.shape[0]
    n_seq_tiles = seq // tq
    row = lambda i: (i, 0)
    tab = lambda i: (i % n_seq_tiles, 0)
    const = lambda i: (0, 0)
    widths = (NSA_WIDTH, LANES, LANES, LANES, 3 * LANES)
    return pl.pallas_call(
        _nsa_prep_kernel,
        grid=(n // tq,),
        in_specs=[
            pl.BlockSpec((tq, NSA_COLS), row),
            pl.BlockSpec((tq, LANES), tab),
            pl.BlockSpec((tq, LANES), tab),
            pl.BlockSpec((1, LANES), const),
            pl.BlockSpec((1, LANES), const),
        ],
        out_specs=[pl.BlockSpec((tq, w), row) for w in widths]
        + [pl.BlockSpec((tq, LANES), row)],
        out_shape=[jax.ShapeDtypeStruct((n, w), BF16) for w in widths]
        + [jax.ShapeDtypeStruct((n, LANES), F32)],
        compiler_params=_cparams(1),
        name="nsa_prep",
    )(nsa_proj, cos_t, sin_t, qg, kg)


def _compress_kernel(tk_ref, tv_ref, pek_ref, pev_ref, w1k_ref, w1v_ref,
                     w2k_ref, w2v_ref, ok_ref, ov_ref):
    half = CMP_STRIDE * HEAD_DIM

    def run(t_ref, pe_ref, w1_ref, w2_ref, o_ref):
        t = t_ref[...]
        n_rows = t.shape[0]
        top = jnp.dot(t, w1_ref[0:half, :], preferred_element_type=F32)
        bot = jnp.dot(t, w1_ref[half:2 * half, :], preferred_element_type=F32)
        bot_next = pltpu.roll(bot, n_rows - 1, 0)
        pe = jnp.dot(pe_ref[...], w1_ref[...], preferred_element_type=F32)[0:1, :]
        h = top + bot_next + pe
        h = h * jax.nn.sigmoid(h)
        o = jnp.dot(h.astype(BF16), w2_ref[...], preferred_element_type=F32)
        r = lax.broadcasted_iota(jnp.int32, o.shape, 0)
        o_ref[...] = jnp.where(r < n_rows - 1, o, 0.0).astype(BF16)

    run(tk_ref, pek_ref, w1k_ref, w2k_ref, ok_ref)
    run(tv_ref, pev_ref, w1v_ref, w2v_ref, ov_ref)


def _compress(tk, tv, pek, pev, w1k, w1v, w2k, w2v):
    bg, n_rows, width = tk.shape
    blk = lambda i: (i, 0, 0)
    const = lambda i: (0, 0)
    return pl.pallas_call(
        _compress_kernel,
        grid=(bg,),
        in_specs=[
            pl.BlockSpec((None, n_rows, width), blk),
            pl.BlockSpec((None, n_rows, width), blk),
            pl.BlockSpec(pek.shape, const),
            pl.BlockSpec(pev.shape, const),
            pl.BlockSpec(w1k.shape, const),
            pl.BlockSpec(w1v.shape, const),
            pl.BlockSpec(w2k.shape, const),
            pl.BlockSpec(w2v.shape, const),
        ],
        out_specs=[pl.BlockSpec((None, n_rows, HEAD_DIM), blk)] * 2,
        out_shape=[jax.ShapeDtypeStruct((bg, n_rows, HEAD_DIM), BF16)] * 2,
        compiler_params=_cparams(1),
        name="compress",
    )(tk, tv, pek, pev, w1k, w1v, w2k, w2v)


def _nsa_attn_kernel(qt_ref, gt_ref, kc_ref, vct_ref, mapt_ref, ks_ref, vst_ref,
                     kw_ref, vwt_ref, y_ref, qaug_ref, s0_ref, s1_ref, c0_ref, c1_ref,
                     m_ref, acc_ref):
    i = pl.program_id(2)
    q0 = i * Q_TILE
    qt = qt_ref[...]
    col = lax.broadcasted_iota(jnp.int32, (1, Q_COLS), 1)
    tpos = q0 + (col & (Q_TILE - 1))

    n_cmp_rows = kc_ref.shape[0]
    s = jnp.dot(kc_ref[...], qt, preferred_element_type=F32)
    cmp_end = lax.broadcasted_iota(jnp.int32, (n_cmp_rows, 1), 0) * CMP_STRIDE + (CMP_LEN - 1)
    vis = cmp_end <= tpos
    s = jnp.where(vis, s, NEG)
    m = jnp.max(s, axis=0, keepdims=True)
    p = jnp.exp2(s - jnp.maximum(m, 0.1 * NEG))
    acc = jnp.dot(vct_ref[...], p.astype(BF16), preferred_element_type=F32)
    l = jnp.sum(p, axis=0, keepdims=True)
    inv = jnp.where(l > 0.0, 1.0 / l, 0.0)
    o_cmp = acc[0:HEAD_DIM, :] * inv

    pn = p * inv
    psum = pn[:, 0:Q_TILE]
    for h in range(1, HEADS_PER_GROUP):
        psum = psum + pn[:, h * Q_TILE:(h + 1) * Q_TILE]
    p_hi = psum.astype(BF16)
    p_lo = (psum - p_hi.astype(F32)).astype(BF16)
    mapt = mapt_ref[...]
    imp = (jnp.dot(mapt, p_hi, preferred_element_type=F32)
           + jnp.dot(mapt, p_lo, preferred_element_type=F32))

    blk_i = lax.broadcasted_iota(jnp.int32, (SLC_LANES, Q_TILE), 0)
    cur = (q0 + lax.broadcasted_iota(jnp.int32, (SLC_LANES, Q_TILE), 1)) >> SLC_SHIFT
    blk = blk_i.astype(F32)
    val = jnp.where(blk_i <= cur, imp, -1.0)
    for forced in (blk_i == 0, blk_i == cur, blk_i == cur - 1):
        val = jnp.where(forced, 1e30, val)
    sel = jnp.zeros((SLC_LANES, Q_TILE), F32)
    for _ in range(SLC_TOPN):
        mx = jnp.max(val, axis=0, keepdims=True)
        first = jnp.min(jnp.where(val == mx, blk, float(SLC_LANES)), axis=0, keepdims=True)
        first = jnp.where(mx > -0.5, first, -1.0)
        pick = blk == first
        sel = jnp.where(pick, 1.0, sel)
        val = jnp.where(pick, -1.0, val)
    sel_bias = jnp.where(sel > 0.5, 0.0, -BIG).astype(BF16)

    qaug_ref[0:HEAD_DIM, :] = qt
    for h in range(HEADS_PER_GROUP):
        qaug_ref[HEAD_DIM:HEAD_DIM + SLC_LANES, h * Q_TILE:(h + 1) * Q_TILE] = sel_bias

    span = WINDOW + Q_TILE
    start = pl.multiple_of(jnp.maximum(q0 - WINDOW, 0), Q_TILE)
    sw = jnp.dot(kw_ref[pl.ds(start, span), :], qt, preferred_element_type=F32)
    kpos = start + lax.broadcasted_iota(jnp.int32, (span, 1), 0)
    sw = jnp.where(kpos <= tpos, jnp.where(kpos > tpos - WINDOW, sw, NEG), NEG)
    mw = jnp.max(sw, axis=0, keepdims=True)
    pw = jnp.exp2(sw - mw).astype(BF16)
    accw = jnp.dot(vwt_ref[:, pl.ds(start, span)], pw, preferred_element_type=F32)
    o_win = accw[0:HEAD_DIM, :] / accw[HEAD_DIM:HEAD_DIM + 1, :]

    def slc_scores(k0, s_ref, cmax_ref):
        sc = jnp.dot(ks_ref[pl.ds(k0, SLC_KEYS), :], qaug_ref[...],
                     preferred_element_type=F32)
        s_ref[...] = sc
        cmax_ref[...] = jnp.max(sc, axis=0, keepdims=True)

    def slc_update(k0, s_ref, cmax_ref, masked):
        sc = s_ref[...]
        if masked:
            kpos = k0 + lax.broadcasted_iota(jnp.int32, (SLC_KEYS, 1), 0)
            sc = jnp.where(kpos <= tpos, sc, -BIG)
            cmax = jnp.max(sc, axis=0, keepdims=True)
        else:
            cmax = cmax_ref[...]
        m_prev = m_ref[...]
        m_new = jnp.maximum(m_prev, cmax)
        pr = jnp.exp2(sc - m_new).astype(BF16)
        pv = jnp.dot(vst_ref[:, pl.ds(k0, SLC_KEYS)], pr, preferred_element_type=F32)
        acc_ref[...] = acc_ref[...] * jnp.exp2(m_prev - m_new) + pv
        m_ref[...] = m_new

    buffers = ((s0_ref, c0_ref), (s1_ref, c1_ref))

    def slc_body(kt, carry):
        k0 = pl.multiple_of(kt * SLC_KEYS, SLC_KEYS)
        for parity in range(2):
            @pl.when((kt & 1) == parity)
            def _():
                slc_scores(k0 + SLC_KEYS, *buffers[1 - parity])
                slc_update(k0, *buffers[parity], masked=False)
        return carry

    n_full = q0 // SLC_KEYS
    m_ref[...] = jnp.full(m_ref.shape, NEG, F32)
    acc_ref[...] = jnp.zeros(acc_ref.shape, F32)
    slc_scores(0, *buffers[0])
    lax.fori_loop(0, n_full, slc_body, 0)
    k_last = pl.multiple_of(n_full * SLC_KEYS, SLC_KEYS)
    for parity in range(2):
        @pl.when((n_full & 1) == parity)
        def _():
            slc_update(k_last, *buffers[parity], masked=True)
    o_slc = acc_ref[0:HEAD_DIM, :] / acc_ref[HEAD_DIM:HEAD_DIM + 1, :]

    g = gt_ref[...]

    def gate_row(branch):
        return jnp.concatenate(
            [g[h * 3 + branch:h * 3 + branch + 1, :] for h in range(HEADS_PER_GROUP)], axis=1)

    y_ref[...] = gate_row(0) * o_cmp + gate_row(1) * o_slc + gate_row(2) * o_win


def _nsa_attn(qt, gt, kc, vct, mapt, ks, vst, kw, vwt):
    b, g, nt = qt.shape[:3]
    seq = ks.shape[2]
    n_cmp_rows = kc.shape[2]
    per_bg = lambda bi, gi, i: (bi, gi, 0, 0)
    return pl.pallas_call(
        _nsa_attn_kernel,
        grid=(b, g, nt),
        in_specs=[
            pl.BlockSpec((None, None, None, HEAD_DIM, Q_COLS), lambda bi, gi, i: (bi, gi, i, 0, 0)),
            pl.BlockSpec((None, None, 16, Q_TILE), lambda bi, gi, i: (bi, gi, 0, i)),
            pl.BlockSpec((None, None, n_cmp_rows, HEAD_DIM), per_bg),
            pl.BlockSpec((None, None, V_ROWS, n_cmp_rows), per_bg),
            pl.BlockSpec(mapt.shape, lambda bi, gi, i: (0, 0)),
            pl.BlockSpec((None, None, seq, HEAD_DIM + SLC_LANES), per_bg),
            pl.BlockSpec((None, None, V_ROWS, seq), per_bg),
            pl.BlockSpec((None, None, seq, HEAD_DIM), per_bg),
            pl.BlockSpec((None, None, V_ROWS, seq), per_bg),
        ],
        out_specs=pl.BlockSpec((None, None, None, HEAD_DIM, Q_COLS),
                               lambda bi, gi, i: (bi, gi, i, 0, 0)),
        out_shape=jax.ShapeDtypeStruct((b, g, nt, HEAD_DIM, Q_COLS), F32),
        scratch_shapes=[
            pltpu.VMEM((HEAD_DIM + SLC_LANES, Q_COLS), BF16),
            pltpu.VMEM((SLC_KEYS, Q_COLS), F32),
            pltpu.VMEM((SLC_KEYS, Q_COLS), F32),
            pltpu.VMEM((1, Q_COLS), F32),
            pltpu.VMEM((1, Q_COLS), F32),
            pltpu.VMEM((1, Q_COLS), F32),
            pltpu.VMEM((V_ROWS, Q_COLS), F32),
        ],
        compiler_params=_cparams(3),
        name="nsa_attn",
    )(qt, gt, kc, vct, mapt, ks, vst, kw, vwt)


def _segmented_scans(g):
    rows = g.shape[0]
    row = lax.broadcasted_iota(jnp.int32, g.shape, 0)
    out = {1: (g, g)}
    pre, suf = g, g
    h = 1
    while h < rows:
        pos = row & (2 * h - 1)
        if h < 8:
            add_pre = jnp.zeros_like(g)
            add_suf = jnp.zeros_like(g)
            for shift in range(1, h + 1):
                add_pre = jnp.where(pos == h + shift - 1, pltpu.roll(pre, shift, 0), add_pre)
                add_suf = jnp.where(pos == h - shift, pltpu.roll(suf, rows - shift, 0), add_suf)
            pre = pre + add_pre
            suf = suf + add_suf
        else:
            pre_parts, suf_parts = [], []
            for b in range(rows // (2 * h)):
                lo, mid, hi = b * 2 * h, b * 2 * h + h, (b + 1) * 2 * h
                pre_parts += [pre[lo:mid], pre[mid:hi] + pre[mid - 1:mid, :]]
                suf_parts += [suf[lo:mid] + suf[mid:mid + 1, :], suf[mid:hi]]
            pre = jnp.concatenate(pre_parts, axis=0)
            suf = jnp.concatenate(suf_parts, axis=0)
        h *= 2
        out[h] = (pre, suf)
    return out


def _hgrn_kernel(hg_ref, lb_ref, gain_ref, o_ref, state_ref):
    c = pl.program_id(1)

    @pl.when(c == 0)
    def _():
        state_ref[...] = jnp.zeros_like(state_ref)

    w = HG_WIDTH
    q = hg_ref[:, 0:w]
    f_logit = hg_ref[:, w:2 * w]
    v = hg_ref[:, 2 * w:3 * w]
    og = hg_ref[:, 3 * w:4 * w]
    lb = lb_ref[...]
    f = lb + (1.0 - lb) * jax.nn.sigmoid(f_logit)
    logf = jnp.log(f)
    k = 1.0 - f
    qf = q * jax.nn.sigmoid(q)
    vb = v.astype(BF16)

    scans = _segmented_scans(logf)
    rows = HG_CHUNK
    t_idx = lax.broadcasted_iota(jnp.int32, (rows, rows), 0)
    s_idx = lax.broadcasted_iota(jnp.int32, (rows, rows), 1)
    diff = jnp.where(t_idx > s_idx, t_idx ^ s_idx, 0)

    def heads(x):
        return [x[:, h * HG_DIM:(h + 1) * HG_DIM] for h in range(HG_HEADS)]

    nt_dims = (((1,), (1,)), ((), ()))
    qb, kb = heads(qf.astype(BF16)), heads(k.astype(BF16))
    a = [jnp.where(t_idx == s_idx,
                   lax.dot_general(qb[h], kb[h], nt_dims, preferred_element_type=F32), 0.0)
         for h in range(HG_HEADS)]
    level, h_size = 0, 1
    while h_size < rows:
        pre, suf = scans[h_size]
        ql = heads((qf * jnp.exp(pre)).astype(BF16))
        kl = heads((k * jnp.exp(suf - logf)).astype(BF16))
        mask = (diff >> level) == 1
        for h in range(HG_HEADS):
            al = lax.dot_general(ql[h], kl[h], nt_dims, preferred_element_type=F32)
            a[h] = jnp.where(mask, al, a[h])
        level += 1
        h_size *= 2

    cum, rev = scans[rows]
    q_in = heads((qf * jnp.exp(cum)).astype(BF16))
    k_out = heads((k * jnp.exp(rev - logf)).astype(BF16))
    vh = heads(vb)
    gain = gain_ref[...]
    for h in range(HG_HEADS):
        lanes = slice(h * HG_DIM, (h + 1) * HG_DIM)
        state_t = state_ref[h]
        o = jnp.dot(a[h].astype(BF16), vh[h], preferred_element_type=F32)
        o = o + lax.dot_general(q_in[h], state_t.astype(BF16), nt_dims,
                                preferred_element_type=F32)
        decay = jnp.exp(cum[rows - 1:rows, lanes])
        v_t = v[:, lanes].T.astype(BF16)
        state_ref[h] = state_t * decay + jnp.dot(v_t, k_out[h], preferred_element_type=F32)
        ms = jnp.mean(o * o, axis=-1, keepdims=True)
        gate = og[:, lanes]
        o_ref[:, lanes] = o * lax.rsqrt(ms + EPS) * gain[:, lanes] * (gate * jax.nn.sigmoid(gate))


def _hgrn(hg_proj, lb, gain, batch, seq):
    n = hg_proj.shape[0]
    n_chunk = seq // HG_CHUNK
    row = lambda b, c: (b * n_chunk + c, 0)
    const = lambda b, c: (0, 0)
    return pl.pallas_call(
        _hgrn_kernel,
        grid=(batch, n_chunk),
        in_specs=[
            pl.BlockSpec((HG_CHUNK, 4 * HG_WIDTH), row),
            pl.BlockSpec((1, HG_WIDTH), const),
            pl.BlockSpec((1, HG_WIDTH), const),
        ],
        out_specs=pl.BlockSpec((HG_CHUNK, HG_WIDTH), row),
        out_shape=jax.ShapeDtypeStruct((n, HG_WIDTH), F32),
        scratch_shapes=[pltpu.VMEM((HG_HEADS, HG_DIM, HG_DIM), F32)],
        compiler_params=_cparams(2),
        name="hgrn",
    )(hg_proj, lb, gain)


def _out_mlp_kernel(x_ref, yn_ref, yh_ref, gn_ref, gm_ref, wa_ref, wb_ref,
                    w1_ref, w2_ref, o_ref, *, ff_chunk):
    yn = yn_ref[...]
    ms = jnp.mean(yn * yn, axis=-1, keepdims=True)
    yn = (yn * lax.rsqrt(ms + EPS) * gn_ref[...]).astype(BF16)
    x1 = (x_ref[...]
          + jnp.dot(yn, wa_ref[...], preferred_element_type=F32)
          + jnp.dot(yh_ref[...].astype(BF16), wb_ref[...], preferred_element_type=F32))
    ms = jnp.mean(x1 * x1, axis=-1, keepdims=True)
    h = (x1 * lax.rsqrt(ms + EPS) * gm_ref[...]).astype(BF16)
    mlp = None
    for c in range(w1_ref.shape[1] // ff_chunk):
        cols = slice(c * ff_chunk, (c + 1) * ff_chunk)
        u = jnp.maximum(jnp.dot(h, w1_ref[:, cols], preferred_element_type=F32), 0.0)
        d = jnp.dot((u * u).astype(BF16), w2_ref[cols, :], preferred_element_type=F32)
        mlp = d if mlp is None else mlp + d
    o_ref[...] = x1 + mlp


def _out_mlp(x2, y_nsa, y_hg, gn, gm, wa, wb, w1, w2, tm=256, ff_chunk=1024):
    n, d = x2.shape
    row = lambda i: (i, 0)
    const = lambda i: (0, 0)

    def resident(shape):
        return pl.BlockSpec(shape, const)

    return pl.pallas_call(
        functools.partial(_out_mlp_kernel, ff_chunk=ff_chunk),
        grid=(n // tm,),
        in_specs=[
            pl.BlockSpec((tm, d), row),
            pl.BlockSpec((tm, y_nsa.shape[1]), row),
            pl.BlockSpec((tm, y_hg.shape[1]), row),
            pl.BlockSpec((1, y_nsa.shape[1]), const),
            pl.BlockSpec((1, d), const),
            resident(wa.shape),
            resident(wb.shape),
            resident(w1.shape),
            resident(w2.shape),
        ],
        out_specs=pl.BlockSpec((tm, d), row),
        out_shape=jax.ShapeDtypeStruct((n, d), F32),
        compiler_params=_cparams(1),
        name="out_mlp",
    )(x2, y_nsa, y_hg, gn, gm, wa, wb, w1, w2)


def _rope_tables(seq):
    pos = jnp.arange(seq, dtype=F32)
    inv_freq = ROPE_THETA ** (-jnp.arange(0, ROT_DIM, 2, dtype=F32) / ROT_DIM)
    ang = pos[:, None] * inv_freq[None, :]
    cos, sin = jnp.cos(ang), jnp.sin(ang)
    rest = HEAD_DIM - ROT_DIM
    cos_h = jnp.concatenate([cos, cos, jnp.ones((seq, rest), F32)], axis=-1)
    sin_h = jnp.concatenate([-sin, sin, jnp.zeros((seq, rest), F32)], axis=-1)
    return jnp.tile(cos_h, (1, 2)), jnp.tile(sin_h, (1, 2))


def _cmp_to_slc_t(n_cmp_rows, n_cmp, n_slc):
    tok = np.arange(n_cmp)[:, None] * CMP_STRIDE + np.arange(CMP_LEN)[None, :]
    frac = (tok[:, :, None] // SLC_BLOCK == np.arange(n_slc)[None, None, :]).mean(axis=1)
    out = np.zeros((SLC_LANES, n_cmp_rows), np.float32)
    out[:n_slc, :n_cmp] = frac.T
    return jnp.asarray(out, BF16)


def _values_t(v, batch, seq):
    vt = v.reshape(batch, seq, NSA_GROUPS, HEAD_DIM).transpose(0, 2, 3, 1)
    ones = jnp.ones((batch, NSA_GROUPS, 1, seq), v.dtype)
    pad = jnp.zeros((batch, NSA_GROUPS, V_ROWS - HEAD_DIM - 1, seq), v.dtype)
    return jnp.concatenate([vt, ones, pad], axis=2)


def kernel(x, attn_norm, w_in, q_norm, k_norm, cmp_pe_k, cmp_w1_k, cmp_w2_k, cmp_pe_v,
           cmp_w1_v, cmp_w2_v, nsa_out_norm, hgrn_lb, hgrn_out_norm, w_out, mlp_norm,
           w_mlp_in, w_mlp_out):
    batch, seq, d_model = x.shape
    n = batch * seq
    n_slc = seq // SLC_BLOCK
    n_cmp_rows = seq // CMP_STRIDE
    n_cmp = (seq - CMP_LEN) // CMP_STRIDE + 1
    assert n_slc <= SLC_LANES and seq % SLC_KEYS == 0 and seq >= WINDOW + Q_TILE
    assert w_in.shape[0] == 1, "single-layer block"

    kvw = KV_WIDTH
    o_q, o_kc, o_vc, o_ks, o_vs, o_kw, o_vw, o_gate = np.cumsum(
        [0, NSA_WIDTH, kvw, kvw, kvw, kvw, kvw, kvw])
    o_hg = o_gate + NSA_HEADS * 3
    w = w_in[0]
    pad = jnp.zeros((d_model, NSA_COLS - (NSA_WIDTH + 6 * kvw + NSA_HEADS * 3)), w.dtype)
    w_nsa = jnp.concatenate(
        [w[:, o_q:o_kc], w[:, o_kc:o_vc], w[:, o_ks:o_vs], w[:, o_kw:o_vw],
         w[:, o_vc:o_ks], w[:, o_vs:o_kw], w[:, o_vw:o_gate], w[:, o_gate:o_hg], pad],
        axis=1).astype(BF16)
    w_hg = w[:, o_hg:].astype(BF16)
    x2 = x.reshape(n, d_model)
    nsa_proj, hg_proj = _inproj(x2, attn_norm[0][None, :], w_nsa, w_hg)

    cos_t, sin_t = _rope_tables(seq)
    qg = jnp.tile(q_norm[0], 2)[None, :]
    kg = jnp.tile(k_norm[0], 2)[None, :]
    q, kc, ks, kw, v3, gates = _nsa_prep(nsa_proj, cos_t, sin_t, qg, kg, seq)

    nt = seq // Q_TILE
    qt = (q.reshape(batch, nt, Q_TILE, NSA_GROUPS, HEADS_PER_GROUP, HEAD_DIM)
          .transpose(0, 3, 1, 5, 4, 2).reshape(batch, NSA_GROUPS, nt, HEAD_DIM, Q_COLS))
    gt = (gates[:, :NSA_HEADS * 3].reshape(batch, seq, NSA_GROUPS, HEADS_PER_GROUP * 3)
          .transpose(0, 2, 3, 1))
    gt = jnp.pad(gt, ((0, 0), (0, 0), (0, 16 - HEADS_PER_GROUP * 3), (0, 0)))

    def by_group(t):
        return t.reshape(batch, seq, NSA_GROUPS, HEAD_DIM).transpose(0, 2, 1, 3)

    vc, vs, vw = v3[:, 0:kvw], v3[:, kvw:2 * kvw], v3[:, 2 * kvw:3 * kvw]
    blocks = lambda t: by_group(t).reshape(batch * NSA_GROUPS, n_cmp_rows, CMP_STRIDE * HEAD_DIM)
    pe8 = lambda pe: jnp.broadcast_to(pe.reshape(1, CMP_LEN * HEAD_DIM), (8, CMP_LEN * HEAD_DIM)).astype(BF16)
    k_cmp, v_cmp = _compress(
        blocks(kc), blocks(vc), pe8(cmp_pe_k[0]), pe8(cmp_pe_v[0]),
        cmp_w1_k[0].astype(BF16), cmp_w1_v[0].astype(BF16),
        cmp_w2_k[0].astype(BF16), cmp_w2_v[0].astype(BF16))
    k_cmp = k_cmp.reshape(batch, NSA_GROUPS, n_cmp_rows, HEAD_DIM)
    v_cmp_t = v_cmp.reshape(batch, NSA_GROUPS, n_cmp_rows, HEAD_DIM).transpose(0, 1, 3, 2)
    v_cmp_t = jnp.concatenate(
        [v_cmp_t, jnp.zeros((batch, NSA_GROUPS, V_ROWS - HEAD_DIM, n_cmp_rows), BF16)], axis=2)

    one_hot = (jnp.arange(seq)[:, None] // SLC_BLOCK == jnp.arange(SLC_LANES)[None, :]).astype(BF16)
    ks_aug = jnp.concatenate(
        [by_group(ks), jnp.broadcast_to(one_hot, (batch, NSA_GROUPS, seq, SLC_LANES))], axis=-1)
    y_t = _nsa_attn(qt, gt, k_cmp, v_cmp_t, _cmp_to_slc_t(n_cmp_rows, n_cmp, n_slc),
                    ks_aug, _values_t(vs, batch, seq), by_group(kw), _values_t(vw, batch, seq))
    y_nsa = (y_t.reshape(batch, NSA_GROUPS, nt, HEAD_DIM, HEADS_PER_GROUP, Q_TILE)
             .transpose(0, 2, 5, 1, 4, 3).reshape(n, NSA_WIDTH))

    lb_all = jnp.cumsum(jax.nn.softmax(hgrn_lb.astype(F32), axis=0), axis=0)
    y_hg = _hgrn(hg_proj, lb_all[0][None, :], jnp.tile(hgrn_out_norm[0], HG_HEADS)[None, :],
                 batch, seq)

    wo = w_out[0].astype(BF16)
    out = _out_mlp(x2, y_nsa, y_hg, nsa_out_norm[0][None, :], mlp_norm[0][None, :],
                   wo[:NSA_WIDTH], wo[NSA_WIDTH:], w_mlp_in[0].astype(BF16),
                   w_mlp_out[0].astype(BF16))
    return out.reshape(batch, seq, d_model)
```

```python
import functools

import jax
import jax.numpy as jnp
import numpy as np
from jax import lax
from jax.experimental import pallas as pl
from jax.experimental.pallas import tpu as pltpu

F32 = jnp.float32
BF16 = jnp.bfloat16

NSA_HEADS = 8
NSA_GROUPS = 2
HEADS_PER_GROUP = NSA_HEADS // NSA_GROUPS
HEAD_DIM = 64
NSA_WIDTH = NSA_HEADS * HEAD_DIM
KV_WIDTH = NSA_GROUPS * HEAD_DIM
CMP_LEN = 32
CMP_STRIDE = 16
CMP_HIDDEN = 2 * HEAD_DIM
SLC_BLOCK = 64
SLC_SHIFT = 6
SLC_TOPN = 16
WINDOW = 512
ROT_DIM = HEAD_DIM // 4
ROPE_THETA = 500000.0
HG_HEADS = 4
HG_DIM = 128
HG_WIDTH = HG_HEADS * HG_DIM
EPS = 1e-6

LANES = 128
Q_TILE = 256
Q_COLS = HEADS_PER_GROUP * Q_TILE
SLC_KEYS = 512
SLC_LANES = 128
V_ROWS = 80
GATE_ROWS = 16
NSA_COLS = 1408
NEG = -1e30
Q_SCALE = HEAD_DIM ** -0.5 * 1.4426950408889634
BIG = 2.0 ** 100
HG_CHUNK = 128
VMEM_LIMIT = 56 * 1024 * 1024


def _cparams(n_grid):
    return pltpu.CompilerParams(
        dimension_semantics=("arbitrary",) * n_grid, vmem_limit_bytes=VMEM_LIMIT)


def _inproj_kernel(x_ref, g_ref, wn_ref, wh_ref, nsa_ref, hg_ref):
    x = x_ref[...]
    ms = jnp.mean(x * x, axis=-1, keepdims=True)
    h = (x * lax.rsqrt(ms + EPS) * g_ref[...]).astype(BF16)
    nsa_ref[...] = jnp.dot(h, wn_ref[...], preferred_element_type=F32)
    hg_ref[...] = jnp.dot(h, wh_ref[...], preferred_element_type=F32)


def _inproj(x2, gain, w_nsa, w_hg, tm=256):
    n, d = x2.shape
    const = lambda i: (0, 0)
    return pl.pallas_call(
        _inproj_kernel,
        grid=(n // tm,),
        in_specs=[
            pl.BlockSpec((tm, d), lambda i: (i, 0)),
            pl.BlockSpec((1, d), const),
            pl.BlockSpec(w_nsa.shape, const),
            pl.BlockSpec(w_hg.shape, const),
        ],
        out_specs=[
            pl.BlockSpec((tm, w_nsa.shape[1]), lambda i: (i, 0)),
            pl.BlockSpec((tm, w_hg.shape[1]), lambda i: (i, 0)),
        ],
        out_shape=[
            jax.ShapeDtypeStruct((n, w_nsa.shape[1]), F32),
            jax.ShapeDtypeStruct((n, w_hg.shape[1]), F32),
        ],
        compiler_params=_cparams(1),
        name="inproj",
    )(x2, gain, w_nsa, w_hg)


def _value_rows_tail(cols):
    r = lax.broadcasted_iota(jnp.int32, (V_ROWS - HEAD_DIM, cols), 0)
    return jnp.where(r == 0, 1.0, 0.0).astype(BF16)


def _nsa_prep_kernel(p_ref, cos_ref, sin_ref, qg_ref, kg_ref,
                     qt_ref, kc_ref, ks_ref, kw_ref, vc_ref, vst_ref, vwt_ref, gt_ref):
    tq = p_ref.shape[0]
    cos = cos_ref[...]
    sin = sin_ref[...]
    lane = lax.broadcasted_iota(jnp.int32, (tq, LANES), 1)
    low_head = lane < HEAD_DIM
    dim = lane & (HEAD_DIM - 1)
    first_half = dim < ROT_DIM // 2

    def norm_rope(xs, gain, scale):
        x2 = xs * xs
        s_lo = jnp.sum(jnp.where(low_head, x2, 0.0), axis=-1, keepdims=True)
        s_hi = jnp.sum(jnp.where(low_head, 0.0, x2), axis=-1, keepdims=True)
        ms = jnp.where(low_head, s_lo, s_hi) * (1.0 / HEAD_DIM)
        y = xs * lax.rsqrt(ms + EPS) * gain
        partner = jnp.where(first_half,
                            pltpu.roll(y, LANES - ROT_DIM // 2, 1),
                            pltpu.roll(y, ROT_DIM // 2, 1))
        return (y * cos + partner * sin) * scale

    def split_groups(ref, x, lanes=slice(None)):
        for g in range(NSA_GROUPS):
            ref[g, :, lanes] = x[:, g * HEAD_DIM:(g + 1) * HEAD_DIM].astype(BF16)

    def split_groups_t(ref, x):
        xt = x.T
        for g in range(NSA_GROUPS):
            ref[g, 0:HEAD_DIM, :] = xt[g * HEAD_DIM:(g + 1) * HEAD_DIM, :].astype(BF16)
            ref[g, HEAD_DIM:V_ROWS, :] = _value_rows_tail(tq)

    qg = qg_ref[...]
    kg = kg_ref[...]
    for s in range(NSA_WIDTH // LANES):
        yt = norm_rope(p_ref[:, s * LANES:(s + 1) * LANES], qg, Q_SCALE).T
        g, h0 = divmod(2 * s, HEADS_PER_GROUP)
        for j in range(2):
            qt_ref[g, :, (h0 + j) * tq:(h0 + j + 1) * tq] = (
                yt[j * HEAD_DIM:(j + 1) * HEAD_DIM, :].astype(BF16))
    base = NSA_WIDTH
    split_groups(kc_ref, norm_rope(p_ref[:, base:base + LANES], kg, 1.0))
    tok = pl.program_id(1) * tq + lax.broadcasted_iota(jnp.int32, (tq, SLC_LANES), 0)
    blk = lax.broadcasted_iota(jnp.int32, (tq, SLC_LANES), 1)
    one_hot = jnp.where((tok >> SLC_SHIFT) == blk, 1.0, 0.0).astype(BF16)
    for g in range(NSA_GROUPS):
        ks_ref[g, :, 0:SLC_LANES] = one_hot
    split_groups(ks_ref, norm_rope(p_ref[:, base + LANES:base + 2 * LANES], kg, 1.0),
                 slice(SLC_LANES, SLC_LANES + HEAD_DIM))
    split_groups(kw_ref, norm_rope(p_ref[:, base + 2 * LANES:base + 3 * LANES], kg, 1.0))
    base += 3 * LANES
    split_groups(vc_ref, p_ref[:, base:base + LANES])
    split_groups_t(vst_ref, p_ref[:, base + LANES:base + 2 * LANES])
    split_groups_t(vwt_ref, p_ref[:, base + 2 * LANES:base + 3 * LANES])
    base += 3 * LANES
    gates_t = jax.nn.sigmoid(p_ref[:, base:base + LANES]).T
    per_group = HEADS_PER_GROUP * 3
    for g in range(NSA_GROUPS):
        gt_ref[g, 0:per_group, :] = gates_t[g * per_group:(g + 1) * per_group, :]
        gt_ref[g, per_group:GATE_ROWS, :] = jnp.zeros((GATE_ROWS - per_group, tq), F32)


def _nsa_prep(nsa_proj, cos_t, sin_t, qg, kg, batch, seq):
    tq = Q_TILE
    nt = seq // tq
    const = lambda b, i: (0, 0)
    rows = lambda b, i: (b, 0, i, 0)
    cols = lambda b, i: (b, 0, 0, i)
    grp = NSA_GROUPS
    return pl.pallas_call(
        _nsa_prep_kernel,
        grid=(batch, nt),
        in_specs=[
            pl.BlockSpec((tq, NSA_COLS), lambda b, i: (b * nt + i, 0)),
            pl.BlockSpec((tq, LANES), lambda b, i: (i, 0)),
            pl.BlockSpec((tq, LANES), lambda b, i: (i, 0)),
            pl.BlockSpec((1, LANES), const),
            pl.BlockSpec((1, LANES), const),
        ],
        out_specs=[
            pl.BlockSpec((None, grp, None, HEAD_DIM, Q_COLS), lambda b, i: (b, 0, i, 0, 0)),
            pl.BlockSpec((None, grp, tq, HEAD_DIM), rows),
            pl.BlockSpec((None, grp, tq, SLC_LANES + HEAD_DIM), rows),
            pl.BlockSpec((None, grp, tq, HEAD_DIM), rows),
            pl.BlockSpec((None, grp, tq, HEAD_DIM), rows),
            pl.BlockSpec((None, grp, V_ROWS, tq), cols),
            pl.BlockSpec((None, grp, V_ROWS, tq), cols),
            pl.BlockSpec((None, grp, GATE_ROWS, tq), cols),
        ],
        out_shape=[
            jax.ShapeDtypeStruct((batch, grp, nt, HEAD_DIM, Q_COLS), BF16),
            jax.ShapeDtypeStruct((batch, grp, seq, HEAD_DIM), BF16),
            jax.ShapeDtypeStruct((batch, grp, seq, SLC_LANES + HEAD_DIM), BF16),
            jax.ShapeDtypeStruct((batch, grp, seq, HEAD_DIM), BF16),
            jax.ShapeDtypeStruct((batch, grp, seq, HEAD_DIM), BF16),
            jax.ShapeDtypeStruct((batch, grp, V_ROWS, seq), BF16),
            jax.ShapeDtypeStruct((batch, grp, V_ROWS, seq), BF16),
            jax.ShapeDtypeStruct((batch, grp, GATE_ROWS, seq), F32),
        ],
        compiler_params=_cparams(2),
        name="nsa_prep",
    )(nsa_proj, cos_t, sin_t, qg, kg)


def _compress_kernel(tk_ref, tv_ref, pek_ref, pev_ref, w1k_ref, w1v_ref,
                     w2k_ref, w2v_ref, ok_ref, ov_ref):
    half = CMP_STRIDE * HEAD_DIM

    def run(t_ref, pe_ref, w1_ref, w2_ref, o_ref, transposed):
        t = t_ref[...]
        n_rows = t.shape[0]
        top = jnp.dot(t, w1_ref[0:half, :], preferred_element_type=F32)
        bot = jnp.dot(t, w1_ref[half:2 * half, :], preferred_element_type=F32)
        bot_next = pltpu.roll(bot, n_rows - 1, 0)
        pe = jnp.dot(pe_ref[...], w1_ref[...], preferred_element_type=F32)[0:1, :]
        h = top + bot_next + pe
        h = h * jax.nn.sigmoid(h)
        o = jnp.dot(h.astype(BF16), w2_ref[...], preferred_element_type=F32)
        r = lax.broadcasted_iota(jnp.int32, o.shape, 0)
        o = jnp.where(r < n_rows - 1, o, 0.0)
        if transposed:
            ot = jnp.concatenate([o, jnp.zeros_like(o)], axis=1).T
            o_ref[0:HEAD_DIM, :] = ot[0:HEAD_DIM, :].astype(BF16)
            o_ref[HEAD_DIM:V_ROWS, :] = _value_rows_tail(n_rows)
        else:
            o_ref[...] = o.astype(BF16)

    run(tk_ref, pek_ref, w1k_ref, w2k_ref, ok_ref, False)
    run(tv_ref, pev_ref, w1v_ref, w2v_ref, ov_ref, True)


def _compress(tk, tv, pek, pev, w1k, w1v, w2k, w2v):
    bg, n_rows, width = tk.shape
    blk = lambda i: (i, 0, 0)
    const = lambda i: (0, 0)
    return pl.pallas_call(
        _compress_kernel,
        grid=(bg,),
        in_specs=[
            pl.BlockSpec((None, n_rows, width), blk),
            pl.BlockSpec((None, n_rows, width), blk),
            pl.BlockSpec(pek.shape, const),
            pl.BlockSpec(pev.shape, const),
            pl.BlockSpec(w1k.shape, const),
            pl.BlockSpec(w1v.shape, const),
            pl.BlockSpec(w2k.shape, const),
            pl.BlockSpec(w2v.shape, const),
        ],
        out_specs=[pl.BlockSpec((None, n_rows, HEAD_DIM), blk),
                   pl.BlockSpec((None, V_ROWS, n_rows), blk)],
        out_shape=[jax.ShapeDtypeStruct((bg, n_rows, HEAD_DIM), BF16),
                   jax.ShapeDtypeStruct((bg, V_ROWS, n_rows), BF16)],
        compiler_params=_cparams(1),
        name="compress",
    )(tk, tv, pek, pev, w1k, w1v, w2k, w2v)


def _nsa_attn_kernel(qt_ref, gt_ref, kc_ref, vct_ref, mapt_ref, ks_ref, vst_ref,
                     kw_ref, vwt_ref, y_ref, qaug_ref, s0_ref, s1_ref, c0_ref, c1_ref,
                     m_ref, acc_ref):
    i = pl.program_id(2)
    q0 = i * Q_TILE
    qt = qt_ref[...]
    col = lax.broadcasted_iota(jnp.int32, (1, Q_COLS), 1)
    tpos = q0 + (col & (Q_TILE - 1))

    n_cmp_rows = kc_ref.shape[0]
    s = jnp.dot(kc_ref[...], qt, preferred_element_type=F32)
    cmp_end = lax.broadcasted_iota(jnp.int32, (n_cmp_rows, 1), 0) * CMP_STRIDE + (CMP_LEN - 1)
    vis = cmp_end <= tpos
    s = jnp.where(vis, s, NEG)
    m = jnp.max(s, axis=0, keepdims=True)
    p = jnp.exp2(s - jnp.maximum(m, 0.1 * NEG))
    acc = jnp.dot(vct_ref[...], p.astype(BF16), preferred_element_type=F32)
    l = acc[HEAD_DIM:HEAD_DIM + 1, :]
    inv = jnp.where(l > 0.0, 1.0 / l, 0.0)
    o_cmp = acc[0:HEAD_DIM, :] * inv

    pn = p * inv
    psum = pn[:, 0:Q_TILE]
    for h in range(1, HEADS_PER_GROUP):
        psum = psum + pn[:, h * Q_TILE:(h + 1) * Q_TILE]
    p_hi = psum.astype(BF16)
    p_lo = (psum - p_hi.astype(F32)).astype(BF16)
    mapt = mapt_ref[...]
    imp = (jnp.dot(mapt, p_hi, preferred_element_type=F32)
           + jnp.dot(mapt, p_lo, preferred_element_type=F32))

    blk_i = lax.broadcasted_iota(jnp.int32, (SLC_LANES, Q_TILE), 0)
    cur = (q0 + lax.broadcasted_iota(jnp.int32, (SLC_LANES, Q_TILE), 1)) >> SLC_SHIFT
    blk = blk_i.astype(F32)
    val = jnp.where(blk_i <= cur, imp, -1.0)
    for forced in (blk_i == 0, blk_i == cur, blk_i == cur - 1):
        val = jnp.where(forced, 1e30, val)
    sel = jnp.zeros((SLC_LANES, Q_TILE), F32)
    for _ in range(SLC_TOPN):
        mx = jnp.max(val, axis=0, keepdims=True)
        first = jnp.min(jnp.where(val == mx, blk, float(SLC_LANES)), axis=0, keepdims=True)
        first = jnp.where(mx > -0.5, first, -1.0)
        pick = blk == first
        sel = jnp.where(pick, 1.0, sel)
        val = jnp.where(pick, -1.0, val)
    sel_bias = jnp.where(sel > 0.5, 0.0, -BIG).astype(BF16)

    for h in range(HEADS_PER_GROUP):
        qaug_ref[0:SLC_LANES, h * Q_TILE:(h + 1) * Q_TILE] = sel_bias
    qaug_ref[SLC_LANES:SLC_LANES + HEAD_DIM, :] = qt

    span = WINDOW + Q_TILE
    start = pl.multiple_of(jnp.maximum(q0 - WINDOW, 0), Q_TILE)
    sw = jnp.dot(kw_ref[pl.ds(start, span), :], qt, preferred_element_type=F32)
    kpos = start + lax.broadcasted_iota(jnp.int32, (span, 1), 0)
    sw = jnp.where(kpos <= tpos, jnp.where(kpos > tpos - WINDOW, sw, NEG), NEG)
    mw = jnp.max(sw, axis=0, keepdims=True)
    pw = jnp.exp2(sw - mw).astype(BF16)
    accw = jnp.dot(vwt_ref[:, pl.ds(start, span)], pw, preferred_element_type=F32)
    o_win = accw[0:HEAD_DIM, :] / accw[HEAD_DIM:HEAD_DIM + 1, :]

    def slc_scores(k0, s_ref, cmax_ref):
        sc = jnp.dot(ks_ref[pl.ds(k0, SLC_KEYS), :], qaug_ref[...],
                     preferred_element_type=F32)
        s_ref[...] = sc
        cmax_ref[...] = jnp.max(sc, axis=0, keepdims=True)

    def slc_update(k0, s_ref, cmax_ref, masked):
        sc = s_ref[...]
        if masked:
            kpos = k0 + lax.broadcasted_iota(jnp.int32, (SLC_KEYS, 1), 0)
            sc = jnp.where(kpos <= tpos, sc, -BIG)
            cmax = jnp.max(sc, axis=0, keepdims=True)
        else:
            cmax = cmax_ref[...]
        m_prev = m_ref[...]
        m_new = jnp.maximum(m_prev, cmax)
        pr = jnp.exp2(sc - m_new).astype(BF16)
        pv = jnp.dot(vst_ref[:, pl.ds(k0, SLC_KEYS)], pr, preferred_element_type=F32)
        acc_ref[...] = acc_ref[...] * jnp.exp2(m_prev - m_new) + pv
        m_ref[...] = m_new

    buffers = ((s0_ref, c0_ref), (s1_ref, c1_ref))

    def slc_body(kt, carry):
        k0 = pl.multiple_of(kt * SLC_KEYS, SLC_KEYS)
        for parity in range(2):
            @pl.when((kt & 1) == parity)
            def _():
                slc_scores(k0 + SLC_KEYS, *buffers[1 - parity])
                slc_update(k0, *buffers[parity], masked=False)
        return carry

    n_full = q0 // SLC_KEYS
    m_ref[...] = jnp.full(m_ref.shape, NEG, F32)
    acc_ref[...] = jnp.zeros(acc_ref.shape, F32)
    slc_scores(0, *buffers[0])
    lax.fori_loop(0, n_full, slc_body, 0)
    k_last = pl.multiple_of(n_full * SLC_KEYS, SLC_KEYS)
    for parity in range(2):
        @pl.when((n_full & 1) == parity)
        def _():
            slc_update(k_last, *buffers[parity], masked=True)
    o_slc = acc_ref[0:HEAD_DIM, :] / acc_ref[HEAD_DIM:HEAD_DIM + 1, :]

    g = gt_ref[...]

    def gate_row(branch):
        return jnp.concatenate(
            [g[h * 3 + branch:h * 3 + branch + 1, :] for h in range(HEADS_PER_GROUP)], axis=1)

    y = gate_row(0) * o_cmp + gate_row(1) * o_slc + gate_row(2) * o_win
    for pair in range(HEADS_PER_GROUP // 2):
        lo, mid, hi = (2 * pair) * Q_TILE, (2 * pair + 1) * Q_TILE, (2 * pair + 2) * Q_TILE
        stacked = jnp.concatenate([y[:, lo:mid], y[:, mid:hi]], axis=0)
        y_ref[:, pair * LANES:(pair + 1) * LANES] = stacked.T


def _nsa_attn(qt, gt, kc, vct, mapt, ks, vst, kw, vwt):
    b, g, nt = qt.shape[:3]
    seq = ks.shape[2]
    n_cmp_rows = kc.shape[2]
    per_bg = lambda bi, gi, i: (bi, gi, 0, 0)
    return pl.pallas_call(
        _nsa_attn_kernel,
        grid=(b, g, nt),
        in_specs=[
            pl.BlockSpec((None, None, None, HEAD_DIM, Q_COLS), lambda bi, gi, i: (bi, gi, i, 0, 0)),
            pl.BlockSpec((None, None, GATE_ROWS, Q_TILE), lambda bi, gi, i: (bi, gi, 0, i)),
            pl.BlockSpec((None, None, n_cmp_rows, HEAD_DIM), per_bg),
            pl.BlockSpec((None, None, V_ROWS, n_cmp_rows), per_bg),
            pl.BlockSpec(mapt.shape, lambda bi, gi, i: (0, 0)),
            pl.BlockSpec((None, None, seq, HEAD_DIM + SLC_LANES), per_bg),
            pl.BlockSpec((None, None, V_ROWS, seq), per_bg),
            pl.BlockSpec((None, None, seq, HEAD_DIM), per_bg),
            pl.BlockSpec((None, None, V_ROWS, seq), per_bg),
        ],
        out_specs=pl.BlockSpec((Q_TILE, HEADS_PER_GROUP * HEAD_DIM),
                               lambda bi, gi, i: (bi * nt + i, gi)),
        out_shape=jax.ShapeDtypeStruct((b * seq, NSA_WIDTH), F32),
        scratch_shapes=[
            pltpu.VMEM((SLC_LANES + HEAD_DIM, Q_COLS), BF16),
            pltpu.VMEM((SLC_KEYS, Q_COLS), F32),
            pltpu.VMEM((SLC_KEYS, Q_COLS), F32),
            pltpu.VMEM((1, Q_COLS), F32),
            pltpu.VMEM((1, Q_COLS), F32),
            pltpu.VMEM((1, Q_COLS), F32),
            pltpu.VMEM((V_ROWS, Q_COLS), F32),
        ],
        compiler_params=_cparams(3),
        name="nsa_attn",
    )(qt, gt, kc, vct, mapt, ks, vst, kw, vwt)


def _segmented_scans(g):
    rows = g.shape[0]
    row = lax.broadcasted_iota(jnp.int32, g.shape, 0)
    out = {1: (g, g)}
    pre, suf = g, g
    h = 1
    while h < rows:
        pos = row & (2 * h - 1)
        if h < 8:
            add_pre = jnp.zeros_like(g)
            add_suf = jnp.zeros_like(g)
            for shift in range(1, h + 1):
                add_pre = jnp.where(pos == h + shift - 1, pltpu.roll(pre, shift, 0), add_pre)
                add_suf = jnp.where(pos == h - shift, pltpu.roll(suf, rows - shift, 0), add_suf)
            pre = pre + add_pre
            suf = suf + add_suf
        else:
            pre_parts, suf_parts = [], []
            for b in range(rows // (2 * h)):
                lo, mid, hi = b * 2 * h, b * 2 * h + h, (b + 1) * 2 * h
                pre_parts += [pre[lo:mid], pre[mid:hi] + pre[mid - 1:mid, :]]
                suf_parts += [suf[lo:mid] + suf[mid:mid + 1, :], suf[mid:hi]]
            pre = jnp.concatenate(pre_parts, axis=0)
            suf = jnp.concatenate(suf_parts, axis=0)
        h *= 2
        out[h] = (pre, suf)
    return out


def _hgrn_kernel(hg_ref, lb_ref, gain_ref, o_ref, state_ref):
    c = pl.program_id(1)

    @pl.when(c == 0)
    def _():
        state_ref[...] = jnp.zeros_like(state_ref)

    w = HG_WIDTH
    q = hg_ref[:, 0:w]
    f_logit = hg_ref[:, w:2 * w]
    v = hg_ref[:, 2 * w:3 * w]
    og = hg_ref[:, 3 * w:4 * w]
    lb = lb_ref[...]
    f = lb + (1.0 - lb) * jax.nn.sigmoid(f_logit)
    logf = jnp.log(f)
    k = 1.0 - f
    qf = q * jax.nn.sigmoid(q)
    vb = v.astype(BF16)

    scans = _segmented_scans(logf)
    rows = HG_CHUNK
    t_idx = lax.broadcasted_iota(jnp.int32, (rows, rows), 0)
    s_idx = lax.broadcasted_iota(jnp.int32, (rows, rows), 1)
    diff = jnp.where(t_idx > s_idx, t_idx ^ s_idx, 0)

    def heads(x):
        return [x[:, h * HG_DIM:(h + 1) * HG_DIM] for h in range(HG_HEADS)]

    nt_dims = (((1,), (1,)), ((), ()))
    qb, kb = heads(qf.astype(BF16)), heads(k.astype(BF16))
    a = [jnp.where(t_idx == s_idx,
                   lax.dot_general(qb[h], kb[h], nt_dims, preferred_element_type=F32), 0.0)
         for h in range(HG_HEADS)]
    level, h_size = 0, 1
    while h_size < rows:
        pre, suf = scans[h_size]
        ql = heads((qf * jnp.exp(pre)).astype(BF16))
        kl = heads((k * jnp.exp(suf - logf)).astype(BF16))
        mask = (diff >> level) == 1
        for h in range(HG_HEADS):
            al = lax.dot_general(ql[h], kl[h], nt_dims, preferred_element_type=F32)
            a[h] = jnp.where(mask, al, a[h])
        level += 1
        h_size *= 2

    cum, rev = scans[rows]
    q_in = heads((qf * jnp.exp(cum)).astype(BF16))
    k_out = heads((k * jnp.exp(rev - logf)).astype(BF16))
    vh = heads(vb)
    gain = gain_ref[...]
    for h in range(HG_HEADS):
        lanes = slice(h * HG_DIM, (h + 1) * HG_DIM)
        state_t = state_ref[h]
        o = jnp.dot(a[h].astype(BF16), vh[h], preferred_element_type=F32)
        o = o + lax.dot_general(q_in[h], state_t.astype(BF16), nt_dims,
                                preferred_element_type=F32)
        decay = jnp.exp(cum[rows - 1:rows, lanes])
        v_t = v[:, lanes].T.astype(BF16)
        state_ref[h] = state_t * decay + jnp.dot(v_t, k_out[h], preferred_element_type=F32)
        ms = jnp.mean(o * o, axis=-1, keepdims=True)
        gate = og[:, lanes]
        o_ref[:, lanes] = o * lax.rsqrt(ms + EPS) * gain[:, lanes] * (gate * jax.nn.sigmoid(gate))


def _hgrn(hg_proj, lb, gain, batch, seq):
    n = hg_proj.shape[0]
    n_chunk = seq // HG_CHUNK
    row = lambda b, c: (b * n_chunk + c, 0)
    const = lambda b, c: (0, 0)
    return pl.pallas_call(
        _hgrn_kernel,
        grid=(batch, n_chunk),
        in_specs=[
            pl.BlockSpec((HG_CHUNK, 4 * HG_WIDTH), row),
            pl.BlockSpec((1, HG_WIDTH), const),
            pl.BlockSpec((1, HG_WIDTH), const),
        ],
        out_specs=pl.BlockSpec((HG_CHUNK, HG_WIDTH), row),
        out_shape=jax.ShapeDtypeStruct((n, HG_WIDTH), F32),
        scratch_shapes=[pltpu.VMEM((HG_HEADS, HG_DIM, HG_DIM), F32)],
        compiler_params=_cparams(2),
        name="hgrn",
    )(hg_proj, lb, gain)


def _out_mlp_kernel(x_ref, yn_ref, yh_ref, gn_ref, gm_ref, wa_ref, wb_ref,
                    w1_ref, w2_ref, o_ref, *, ff_chunk):
    yn = yn_ref[...]
    ms = jnp.mean(yn * yn, axis=-1, keepdims=True)
    yn = (yn * lax.rsqrt(ms + EPS) * gn_ref[...]).astype(BF16)
    x1 = (x_ref[...]
          + jnp.dot(yn, wa_ref[...], preferred_element_type=F32)
          + jnp.dot(yh_ref[...].astype(BF16), wb_ref[...], preferred_element_type=F32))
    ms = jnp.mean(x1 * x1, axis=-1, keepdims=True)
    h = (x1 * lax.rsqrt(ms + EPS) * gm_ref[...]).astype(BF16)
    mlp = None
    for c in range(w1_ref.shape[1] // ff_chunk):
        cols = slice(c * ff_chunk, (c + 1) * ff_chunk)
        u = jnp.maximum(jnp.dot(h, w1_ref[:, cols], preferred_element_type=F32), 0.0)
        d = jnp.dot((u * u).astype(BF16), w2_ref[cols, :], preferred_element_type=F32)
        mlp = d if mlp is None else mlp + d
    o_ref[...] = x1 + mlp


def _out_mlp(x2, y_nsa, y_hg, gn, gm, wa, wb, w1, w2, tm=256, ff_chunk=1024):
    n, d = x2.shape
    row = lambda i: (i, 0)
    const = lambda i: (0, 0)

    def resident(shape):
        return pl.BlockSpec(shape, const)

    return pl.pallas_call(
        functools.partial(_out_mlp_kernel, ff_chunk=ff_chunk),
        grid=(n // tm,),
        in_specs=[
            pl.BlockSpec((tm, d), row),
            pl.BlockSpec((tm, y_nsa.shape[1]), row),
            pl.BlockSpec((tm, y_hg.shape[1]), row),
            pl.BlockSpec((1, y_nsa.shape[1]), const),
            pl.BlockSpec((1, d), const),
            resident(wa.shape),
            resident(wb.shape),
            resident(w1.shape),
            resident(w2.shape),
        ],
        out_specs=pl.BlockSpec((tm, d), row),
        out_shape=jax.ShapeDtypeStruct((n, d), F32),
        compiler_params=_cparams(1),
        name="out_mlp",
    )(x2, y_nsa, y_hg, gn, gm, wa, wb, w1, w2)


def _rope_tables(seq):
    pos = jnp.arange(seq, dtype=F32)
    inv_freq = ROPE_THETA ** (-jnp.arange(0, ROT_DIM, 2, dtype=F32) / ROT_DIM)
    ang = pos[:, None] * inv_freq[None, :]
    cos, sin = jnp.cos(ang), jnp.sin(ang)
    rest = HEAD_DIM - ROT_DIM
    cos_h = jnp.concatenate([cos, cos, jnp.ones((seq, rest), F32)], axis=-1)
    sin_h = jnp.concatenate([-sin, sin, jnp.zeros((seq, rest), F32)], axis=-1)
    return jnp.tile(cos_h, (1, 2)), jnp.tile(sin_h, (1, 2))


def _cmp_to_slc_t(n_cmp_rows, n_cmp, n_slc):
    tok = np.arange(n_cmp)[:, None] * CMP_STRIDE + np.arange(CMP_LEN)[None, :]
    frac = (tok[:, :, None] // SLC_BLOCK == np.arange(n_slc)[None, None, :]).mean(axis=1)
    out = np.zeros((SLC_LANES, n_cmp_rows), np.float32)
    out[:n_slc, :n_cmp] = frac.T
    return jnp.asarray(out, BF16)


def kernel(x, attn_norm, w_in, q_norm, k_norm, cmp_pe_k, cmp_w1_k, cmp_w2_k, cmp_pe_v,
           cmp_w1_v, cmp_w2_v, nsa_out_norm, hgrn_lb, hgrn_out_norm, w_out, mlp_norm,
           w_mlp_in, w_mlp_out):
    batch, seq, d_model = x.shape
    n = batch * seq
    n_slc = seq // SLC_BLOCK
    n_cmp_rows = seq // CMP_STRIDE
    n_cmp = (seq - CMP_LEN) // CMP_STRIDE + 1
    assert n_slc <= SLC_LANES and seq % SLC_KEYS == 0 and seq >= WINDOW + Q_TILE
    assert w_in.shape[0] == 1, "single-layer block"

    kvw = KV_WIDTH
    o_q, o_kc, o_vc, o_ks, o_vs, o_kw, o_vw, o_gate = np.cumsum(
        [0, NSA_WIDTH, kvw, kvw, kvw, kvw, kvw, kvw])
    o_hg = o_gate + NSA_HEADS * 3
    w = w_in[0]
    pad = jnp.zeros((d_model, NSA_COLS - (NSA_WIDTH + 6 * kvw + NSA_HEADS * 3)), w.dtype)
    w_nsa = jnp.concatenate(
        [w[:, o_q:o_kc], w[:, o_kc:o_vc], w[:, o_ks:o_vs], w[:, o_kw:o_vw],
         w[:, o_vc:o_ks], w[:, o_vs:o_kw], w[:, o_vw:o_gate], w[:, o_gate:o_hg], pad],
        axis=1).astype(BF16)
    w_hg = w[:, o_hg:].astype(BF16)
    x2 = x.reshape(n, d_model)
    nsa_proj, hg_proj = _inproj(x2, attn_norm[0][None, :], w_nsa, w_hg)

    cos_t, sin_t = _rope_tables(seq)
    qg = jnp.tile(q_norm[0], 2)[None, :]
    kg = jnp.tile(k_norm[0], 2)[None, :]
    qt, kc, ks_aug, kw, vc, vs_t, vw_t, gt = _nsa_prep(nsa_proj, cos_t, sin_t, qg, kg, batch, seq)

    blocks = lambda t: t.reshape(batch * NSA_GROUPS, n_cmp_rows, CMP_STRIDE * HEAD_DIM)
    pe8 = lambda pe: jnp.broadcast_to(pe.reshape(1, CMP_LEN * HEAD_DIM), (8, CMP_LEN * HEAD_DIM)).astype(BF16)
    k_cmp, v_cmp_t = _compress(
        blocks(kc), blocks(vc), pe8(cmp_pe_k[0]), pe8(cmp_pe_v[0]),
        cmp_w1_k[0].astype(BF16), cmp_w1_v[0].astype(BF16),
        cmp_w2_k[0].astype(BF16), cmp_w2_v[0].astype(BF16))
    k_cmp = k_cmp.reshape(batch, NSA_GROUPS, n_cmp_rows, HEAD_DIM)
    v_cmp_t = v_cmp_t.reshape(batch, NSA_GROUPS, V_ROWS, n_cmp_rows)
    y_nsa = _nsa_attn(qt, gt, k_cmp, v_cmp_t, _cmp_to_slc_t(n_cmp_rows, n_cmp, n_slc),
                      ks_aug, vs_t, kw, vw_t)

    lb_all = jnp.cumsum(jax.nn.softmax(hgrn_lb.astype(F32), axis=0), axis=0)
    y_hg = _hgrn(hg_proj, lb_all[0][None, :], jnp.tile(hgrn_out_norm[0], HG_HEADS)[None, :],
                 batch, seq)

    wo = w_out[0].astype(BF16)
    out = _out_mlp(x2, y_nsa, y_hg, nsa_out_norm[0][None, :], mlp_norm[0][None, :],
                   wo[:NSA_WIDTH], wo[NSA_WIDTH:], w_mlp_in[0].astype(BF16),
                   w_mlp_out[0].astype(BF16))
    return out.reshape(batch, seq, d_model)
```

```python
import functools

import jax
import jax.numpy as jnp
import numpy as np
from jax import lax
from jax.experimental import pallas as pl
from jax.experimental.pallas import tpu as pltpu

F32 = jnp.float32
BF16 = jnp.bfloat16

NSA_HEADS = 8
NSA_GROUPS = 2
HEADS_PER_GROUP = NSA_HEADS // NSA_GROUPS
HEAD_DIM = 64
NSA_WIDTH = NSA_HEADS * HEAD_DIM
KV_WIDTH = NSA_GROUPS * HEAD_DIM
CMP_LEN = 32
CMP_STRIDE = 16
CMP_HIDDEN = 2 * HEAD_DIM
SLC_BLOCK = 64
SLC_SHIFT = 6
SLC_TOPN = 16
WINDOW = 512
ROT_DIM = HEAD_DIM // 4
ROPE_THETA = 500000.0
HG_HEADS = 4
HG_DIM = 128
HG_WIDTH = HG_HEADS * HG_DIM
EPS = 1e-6

LANES = 128
Q_TILE = 256
Q_COLS = HEADS_PER_GROUP * Q_TILE
SLC_KEYS = 512
SLC_LANES = 128
MXU_COLS = 256
V_ROWS = 80
GATE_ROWS = 16
NSA_COLS = 1408
NEG = -1e30
Q_SCALE = HEAD_DIM ** -0.5 * 1.4426950408889634
BIG = 2.0 ** 100
HG_CHUNK = 128
VMEM_LIMIT = 56 * 1024 * 1024


def _cparams(n_grid):
    return pltpu.CompilerParams(
        dimension_semantics=("arbitrary",) * n_grid, vmem_limit_bytes=VMEM_LIMIT)


def _inproj_kernel(x_ref, g_ref, wn_ref, wh_ref, nsa_ref, hg_ref):
    x = x_ref[...]
    ms = jnp.mean(x * x, axis=-1, keepdims=True)
    h = (x * lax.rsqrt(ms + EPS) * g_ref[...]).astype(BF16)
    nsa_ref[...] = jnp.dot(h, wn_ref[...], preferred_element_type=F32)
    hg_ref[...] = jnp.dot(h, wh_ref[...], preferred_element_type=F32)


def _inproj(x2, gain, w_nsa, w_hg, tm=256):
    n, d = x2.shape
    const = lambda i: (0, 0)
    return pl.pallas_call(
        _inproj_kernel,
        grid=(n // tm,),
        in_specs=[
            pl.BlockSpec((tm, d), lambda i: (i, 0)),
            pl.BlockSpec((1, d), const),
            pl.BlockSpec(w_nsa.shape, const),
            pl.BlockSpec(w_hg.shape, const),
        ],
        out_specs=[
            pl.BlockSpec((tm, w_nsa.shape[1]), lambda i: (i, 0)),
            pl.BlockSpec((tm, w_hg.shape[1]), lambda i: (i, 0)),
        ],
        out_shape=[
            jax.ShapeDtypeStruct((n, w_nsa.shape[1]), F32),
            jax.ShapeDtypeStruct((n, w_hg.shape[1]), F32),
        ],
        compiler_params=_cparams(1),
        name="inproj",
    )(x2, gain, w_nsa, w_hg)


def _value_rows_tail(cols):
    r = lax.broadcasted_iota(jnp.int32, (V_ROWS - HEAD_DIM, cols), 0)
    return jnp.where(r == 0, 1.0, 0.0).astype(BF16)


def _nsa_prep_kernel(p_ref, cos_ref, sin_ref, qg_ref, kg_ref,
                     qt_ref, kc_ref, ks_ref, kw_ref, vc_ref, vst_ref, vwt_ref, gt_ref):
    tq = p_ref.shape[0]
    cos = cos_ref[...]
    sin = sin_ref[...]
    lane = lax.broadcasted_iota(jnp.int32, (tq, LANES), 1)
    low_head = lane < HEAD_DIM
    dim = lane & (HEAD_DIM - 1)
    first_half = dim < ROT_DIM // 2

    def norm_rope(xs, gain, scale):
        x2 = xs * xs
        s_lo = jnp.sum(jnp.where(low_head, x2, 0.0), axis=-1, keepdims=True)
        s_hi = jnp.sum(jnp.where(low_head, 0.0, x2), axis=-1, keepdims=True)
        ms = jnp.where(low_head, s_lo, s_hi) * (1.0 / HEAD_DIM)
        y = xs * lax.rsqrt(ms + EPS) * gain
        partner = jnp.where(first_half,
                            pltpu.roll(y, LANES - ROT_DIM // 2, 1),
                            pltpu.roll(y, ROT_DIM // 2, 1))
        return (y * cos + partner * sin) * scale

    def split_groups(ref, x, lanes=slice(None)):
        for g in range(NSA_GROUPS):
            ref[g, :, lanes] = x[:, g * HEAD_DIM:(g + 1) * HEAD_DIM].astype(BF16)

    def split_groups_t(ref, x):
        xt = x.T
        for g in range(NSA_GROUPS):
            ref[g, 0:HEAD_DIM, :] = xt[g * HEAD_DIM:(g + 1) * HEAD_DIM, :].astype(BF16)
            ref[g, HEAD_DIM:V_ROWS, :] = _value_rows_tail(tq)

    qg = qg_ref[...]
    kg = kg_ref[...]
    for s in range(NSA_WIDTH // LANES):
        yt = norm_rope(p_ref[:, s * LANES:(s + 1) * LANES], qg, Q_SCALE).T
        g, h0 = divmod(2 * s, HEADS_PER_GROUP)
        for j in range(2):
            qt_ref[g, :, (h0 + j) * tq:(h0 + j + 1) * tq] = (
                yt[j * HEAD_DIM:(j + 1) * HEAD_DIM, :].astype(BF16))
    base = NSA_WIDTH
    split_groups(kc_ref, norm_rope(p_ref[:, base:base + LANES], kg, 1.0))
    tok = pl.program_id(1) * tq + lax.broadcasted_iota(jnp.int32, (tq, SLC_LANES), 0)
    blk = lax.broadcasted_iota(jnp.int32, (tq, SLC_LANES), 1)
    one_hot = jnp.where((tok >> SLC_SHIFT) == blk, 1.0, 0.0).astype(BF16)
    for g in range(NSA_GROUPS):
        ks_ref[g, :, 0:SLC_LANES] = one_hot
    split_groups(ks_ref, norm_rope(p_ref[:, base + LANES:base + 2 * LANES], kg, 1.0),
                 slice(SLC_LANES, SLC_LANES + HEAD_DIM))
    split_groups(kw_ref, norm_rope(p_ref[:, base + 2 * LANES:base + 3 * LANES], kg, 1.0))
    base += 3 * LANES
    split_groups(vc_ref, p_ref[:, base:base + LANES])
    split_groups_t(vst_ref, p_ref[:, base + LANES:base + 2 * LANES])
    split_groups_t(vwt_ref, p_ref[:, base + 2 * LANES:base + 3 * LANES])
    base += 3 * LANES
    gates_t = jax.nn.sigmoid(p_ref[:, base:base + LANES]).T
    per_group = HEADS_PER_GROUP * 3
    for g in range(NSA_GROUPS):
        gt_ref[g, 0:per_group, :] = gates_t[g * per_group:(g + 1) * per_group, :]
        gt_ref[g, per_group:GATE_ROWS, :] = jnp.zeros((GATE_ROWS - per_group, tq), F32)


def _nsa_prep(nsa_proj, cos_t, sin_t, qg, kg, batch, seq):
    tq = Q_TILE
    nt = seq // tq
    const = lambda b, i: (0, 0)
    rows = lambda b, i: (b, 0, i, 0)
    cols = lambda b, i: (b, 0, 0, i)
    grp = NSA_GROUPS
    return pl.pallas_call(
        _nsa_prep_kernel,
        grid=(batch, nt),
        in_specs=[
            pl.BlockSpec((tq, NSA_COLS), lambda b, i: (b * nt + i, 0)),
            pl.BlockSpec((tq, LANES), lambda b, i: (i, 0)),
            pl.BlockSpec((tq, LANES), lambda b, i: (i, 0)),
            pl.BlockSpec((1, LANES), const),
            pl.BlockSpec((1, LANES), const),
        ],
        out_specs=[
            pl.BlockSpec((None, grp, None, HEAD_DIM, Q_COLS), lambda b, i: (b, 0, i, 0, 0)),
            pl.BlockSpec((None, grp, tq, HEAD_DIM), rows),
            pl.BlockSpec((None, grp, tq, SLC_LANES + HEAD_DIM), rows),
            pl.BlockSpec((None, grp, tq, HEAD_DIM), rows),
            pl.BlockSpec((None, grp, tq, HEAD_DIM), rows),
            pl.BlockSpec((None, grp, V_ROWS, tq), cols),
            pl.BlockSpec((None, grp, V_ROWS, tq), cols),
            pl.BlockSpec((None, grp, GATE_ROWS, tq), cols),
        ],
        out_shape=[
            jax.ShapeDtypeStruct((batch, grp, nt, HEAD_DIM, Q_COLS), BF16),
            jax.ShapeDtypeStruct((batch, grp, seq, HEAD_DIM), BF16),
            jax.ShapeDtypeStruct((batch, grp, seq, SLC_LANES + HEAD_DIM), BF16),
            jax.ShapeDtypeStruct((batch, grp, seq, HEAD_DIM), BF16),
            jax.ShapeDtypeStruct((batch, grp, seq, HEAD_DIM), BF16),
            jax.ShapeDtypeStruct((batch, grp, V_ROWS, seq), BF16),
            jax.ShapeDtypeStruct((batch, grp, V_ROWS, seq), BF16),
            jax.ShapeDtypeStruct((batch, grp, GATE_ROWS, seq), F32),
        ],
        compiler_params=_cparams(2),
        name="nsa_prep",
    )(nsa_proj, cos_t, sin_t, qg, kg)


def _compress_kernel(tk_ref, tv_ref, pek_ref, pev_ref, w1k_ref, w1v_ref,
                     w2k_ref, w2v_ref, ok_ref, ov_ref):
    half = CMP_STRIDE * HEAD_DIM

    def run(t_ref, pe_ref, w1_ref, w2_ref, o_ref, transposed):
        t = t_ref[...]
        n_rows = t.shape[0]
        top = jnp.dot(t, w1_ref[0:half, :], preferred_element_type=F32)
        bot = jnp.dot(t, w1_ref[half:2 * half, :], preferred_element_type=F32)
        bot_next = pltpu.roll(bot, n_rows - 1, 0)
        pe = jnp.dot(pe_ref[...], w1_ref[...], preferred_element_type=F32)[0:1, :]
        h = top + bot_next + pe
        h = h * jax.nn.sigmoid(h)
        o = jnp.dot(h.astype(BF16), w2_ref[...], preferred_element_type=F32)
        r = lax.broadcasted_iota(jnp.int32, o.shape, 0)
        o = jnp.where(r < n_rows - 1, o, 0.0)
        if transposed:
            ot = jnp.concatenate([o, jnp.zeros_like(o)], axis=1).T
            o_ref[0:HEAD_DIM, :] = ot[0:HEAD_DIM, :].astype(BF16)
            o_ref[HEAD_DIM:V_ROWS, :] = _value_rows_tail(n_rows)
        else:
            o_ref[...] = o.astype(BF16)

    run(tk_ref, pek_ref, w1k_ref, w2k_ref, ok_ref, False)
    run(tv_ref, pev_ref, w1v_ref, w2v_ref, ov_ref, True)


def _compress(tk, tv, pek, pev, w1k, w1v, w2k, w2v):
    bg, n_rows, width = tk.shape
    blk = lambda i: (i, 0, 0)
    const = lambda i: (0, 0)
    return pl.pallas_call(
        _compress_kernel,
        grid=(bg,),
        in_specs=[
            pl.BlockSpec((None, n_rows, width), blk),
            pl.BlockSpec((None, n_rows, width), blk),
            pl.BlockSpec(pek.shape, const),
            pl.BlockSpec(pev.shape, const),
            pl.BlockSpec(w1k.shape, const),
            pl.BlockSpec(w1v.shape, const),
            pl.BlockSpec(w2k.shape, const),
            pl.BlockSpec(w2v.shape, const),
        ],
        out_specs=[pl.BlockSpec((None, n_rows, HEAD_DIM), blk),
                   pl.BlockSpec((None, V_ROWS, n_rows), blk)],
        out_shape=[jax.ShapeDtypeStruct((bg, n_rows, HEAD_DIM), BF16),
                   jax.ShapeDtypeStruct((bg, V_ROWS, n_rows), BF16)],
        compiler_params=_cparams(1),
        name="compress",
    )(tk, tv, pek, pev, w1k, w1v, w2k, w2v)


def _nsa_attn_kernel(qt_ref, gt_ref, kc_ref, vct_ref, mapt_ref, ks_ref, vst_ref,
                     kw_ref, vwt_ref, y_ref, qaug_ref, s0_ref, s1_ref, c0_ref, c1_ref,
                     m_ref, acc_ref, ocmp_ref, owin_ref):
    i = pl.program_id(2)
    q0 = i * Q_TILE
    qt = qt_ref[...]
    col = lax.broadcasted_iota(jnp.int32, (1, Q_COLS), 1)
    tpos = q0 + (col & (Q_TILE - 1))

    def cmp_and_select(n_rows, n_blk):
        s = jnp.dot(kc_ref[0:n_rows, :], qt, preferred_element_type=F32)
        cmp_end = lax.broadcasted_iota(jnp.int32, (n_rows, 1), 0) * CMP_STRIDE + (CMP_LEN - 1)
        s = jnp.where(cmp_end <= tpos, s, NEG)
        m = jnp.max(s, axis=0, keepdims=True)
        p = jnp.exp2(s - jnp.maximum(m, 0.1 * NEG)).astype(BF16)
        acc = jnp.dot(vct_ref[:, 0:n_rows], p, preferred_element_type=F32)
        l = acc[HEAD_DIM:HEAD_DIM + 1, :]
        inv = jnp.where(l > 0.0, 1.0 / l, 0.0)
        ocmp_ref[...] = acc[0:HEAD_DIM, :] * inv

        mapt = mapt_ref[0:n_blk, 0:n_rows]
        imp = None
        for h in range(HEADS_PER_GROUP):
            cols = slice(h * Q_TILE, (h + 1) * Q_TILE)
            part = jnp.dot(mapt, p[:, cols], preferred_element_type=F32) * inv[:, cols]
            imp = part if imp is None else imp + part

        blk_i = lax.broadcasted_iota(jnp.int32, (n_blk, Q_TILE), 0)
        cur = (q0 + lax.broadcasted_iota(jnp.int32, (n_blk, Q_TILE), 1)) >> SLC_SHIFT
        blk = blk_i.astype(F32)
        val = jnp.where(blk_i <= cur, imp, -1.0)
        for forced in (blk_i == 0, blk_i == cur, blk_i == cur - 1):
            val = jnp.where(forced, 1e30, val)
        sel = jnp.zeros((n_blk, Q_TILE), F32)
        for _ in range(min(SLC_TOPN, n_blk)):
            mx = jnp.max(val, axis=0, keepdims=True)
            first = jnp.min(jnp.where(val == mx, blk, float(n_blk)), axis=0, keepdims=True)
            first = jnp.where(mx > -0.5, first, -1.0)
            pick = blk == first
            sel = jnp.where(pick, 1.0, sel)
            val = jnp.where(pick, -1.0, val)
        sel_bias = jnp.where(sel > 0.5, 0.0, -BIG).astype(BF16)
        for h in range(HEADS_PER_GROUP):
            qaug_ref[0:n_blk, h * Q_TILE:(h + 1) * Q_TILE] = sel_bias
        if n_blk < SLC_LANES:
            qaug_ref[n_blk:SLC_LANES, :] = jnp.full((SLC_LANES - n_blk, Q_COLS), -BIG, BF16)

    col_tiles = [slice(c0, c0 + MXU_COLS) for c0 in range(0, Q_COLS, MXU_COLS)]
    buffers = ((s0_ref, c0_ref), (s1_ref, c1_ref))

    def slc_scores(k0, s_ref, cmax_ref, cols):
        sc = jnp.dot(ks_ref[pl.ds(k0, SLC_KEYS), :], qaug_ref[:, cols],
                     preferred_element_type=F32)
        s_ref[:, cols] = sc
        cmax_ref[:, cols] = jnp.max(sc, axis=0, keepdims=True)

    span = WINDOW + Q_TILE
    t_loc = col & (Q_TILE - 1)

    def window_scores(interior):
        if not interior:
            sw = jnp.dot(kw_ref[0:span, :], qt, preferred_element_type=F32)
            kpos = lax.broadcasted_iota(jnp.int32, (span, 1), 0)
            sw = jnp.where(kpos <= tpos, jnp.where(kpos > tpos - WINDOW, sw, NEG), NEG)
            return [sw], 0
        start = pl.multiple_of(q0 - WINDOW, Q_TILE)
        i_loc = lax.broadcasted_iota(jnp.int32, (Q_TILE, 1), 0)
        n_parts = span // Q_TILE
        parts = []
        for r in range(n_parts):
            sw = jnp.dot(kw_ref[pl.ds(start + r * Q_TILE, Q_TILE), :], qt,
                         preferred_element_type=F32)
            if r == 0:
                sw = jnp.where(i_loc > t_loc, sw, NEG)
            elif r == n_parts - 1:
                sw = jnp.where(i_loc <= t_loc, sw, NEG)
            parts.append(sw)
        return parts, start

    def window_finish(parts, v_start):
        mw = None
        for sw in parts:
            cm = jnp.max(sw, axis=0, keepdims=True)
            mw = cm if mw is None else jnp.maximum(mw, cm)
        accw, off = None, 0
        for sw in parts:
            pw = jnp.exp2(sw - mw).astype(BF16)
            d = jnp.dot(vwt_ref[:, pl.ds(v_start + off, sw.shape[0])], pw,
                        preferred_element_type=F32)
            accw = d if accw is None else accw + d
            off += sw.shape[0]
        owin_ref[...] = accw[0:HEAD_DIM, :] / accw[HEAD_DIM:HEAD_DIM + 1, :]

    def front(n_rows, n_blk, interior):
        parts, v_start = window_scores(interior)
        cmp_and_select(n_rows, n_blk)
        qaug_ref[SLC_LANES:SLC_LANES + HEAD_DIM, :] = qt
        for cols in col_tiles:
            slc_scores(0, *buffers[0], cols)
        window_finish(parts, v_start)

    n_cmp_rows = kc_ref.shape[0]
    row_steps = list(range(LANES, n_cmp_rows, LANES)) + [n_cmp_rows]
    q_end = q0 + Q_TILE
    lo = 0
    for n_rows in row_steps:
        hi = n_rows * CMP_STRIDE
        n_blk = min(SLC_LANES, -(-(hi // SLC_BLOCK) // 16) * 16)
        cuts = [lo, WINDOW, hi] if lo < WINDOW < hi else [lo, hi]
        for c_lo, c_hi in zip(cuts[:-1], cuts[1:]):
            pl.when(jnp.logical_and(q_end > c_lo, q_end <= c_hi))(
                functools.partial(front, n_rows, n_blk, c_lo >= WINDOW))
        lo = hi

    def slc_update(k0, s_ref, cmax_ref, cols, masked):
        sc = s_ref[:, cols]
        if masked:
            kpos = k0 + lax.broadcasted_iota(jnp.int32, (SLC_KEYS, 1), 0)
            sc = jnp.where(kpos <= tpos[:, cols], sc, -BIG)
            cmax = jnp.max(sc, axis=0, keepdims=True)
        else:
            cmax = cmax_ref[:, cols]
        m_prev = m_ref[:, cols]
        m_new = jnp.maximum(m_prev, cmax)
        pr = jnp.exp2(sc - m_new).astype(BF16)
        pv = jnp.dot(vst_ref[:, pl.ds(k0, SLC_KEYS)], pr, preferred_element_type=F32)
        acc_ref[:, cols] = acc_ref[:, cols] * jnp.exp2(m_prev - m_new) + pv
        m_ref[:, cols] = m_new

    def slc_body(kt, carry):
        k0 = pl.multiple_of(kt * SLC_KEYS, SLC_KEYS)
        for parity in range(2):
            @pl.when((kt & 1) == parity)
            def _():
                for cols in col_tiles:
                    slc_scores(k0 + SLC_KEYS, *buffers[1 - parity], cols)
                    slc_update(k0, *buffers[parity], cols, masked=False)
        return carry

    n_full = q0 // SLC_KEYS
    m_ref[...] = jnp.full(m_ref.shape, NEG, F32)
    acc_ref[...] = jnp.zeros(acc_ref.shape, F32)
    lax.fori_loop(0, n_full, slc_body, 0)
    k_last = pl.multiple_of(n_full * SLC_KEYS, SLC_KEYS)
    for parity in range(2):
        @pl.when((n_full & 1) == parity)
        def _():
            for cols in col_tiles:
                slc_update(k_last, *buffers[parity], cols, masked=True)
    o_slc = acc_ref[0:HEAD_DIM, :] / acc_ref[HEAD_DIM:HEAD_DIM + 1, :]

    g = gt_ref[...]

    def gate_row(branch):
        return jnp.concatenate(
            [g[h * 3 + branch:h * 3 + branch + 1, :] for h in range(HEADS_PER_GROUP)], axis=1)

    y = (gate_row(0) * ocmp_ref[...] + gate_row(1) * o_slc
         + gate_row(2) * owin_ref[...])
    for pair in range(HEADS_PER_GROUP // 2):
        lo, mid, hi = (2 * pair) * Q_TILE, (2 * pair + 1) * Q_TILE, (2 * pair + 2) * Q_TILE
        stacked = jnp.concatenate([y[:, lo:mid], y[:, mid:hi]], axis=0)
        y_ref[:, pair * LANES:(pair + 1) * LANES] = stacked.T


def _nsa_attn(qt, gt, kc, vct, mapt, ks, vst, kw, vwt):
    b, g, nt = qt.shape[:3]
    seq = ks.shape[2]
    n_cmp_rows = kc.shape[2]
    per_bg = lambda bi, gi, i: (bi, gi, 0, 0)
    return pl.pallas_call(
        _nsa_attn_kernel,
        grid=(b, g, nt),
        in_specs=[
            pl.BlockSpec((None, None, None, HEAD_DIM, Q_COLS), lambda bi, gi, i: (bi, gi, i, 0, 0)),
            pl.BlockSpec((None, None, GATE_ROWS, Q_TILE), lambda bi, gi, i: (bi, gi, 0, i)),
            pl.BlockSpec((None, None, n_cmp_rows, HEAD_DIM), per_bg),
            pl.BlockSpec((None, None, V_ROWS, n_cmp_rows), per_bg),
            pl.BlockSpec(mapt.shape, lambda bi, gi, i: (0, 0)),
            pl.BlockSpec((None, None, seq, HEAD_DIM + SLC_LANES), per_bg),
            pl.BlockSpec((None, None, V_ROWS, seq), per_bg),
            pl.BlockSpec((None, None, seq, HEAD_DIM), per_bg),
            pl.BlockSpec((None, None, V_ROWS, seq), per_bg),
        ],
        out_specs=pl.BlockSpec((Q_TILE, HEADS_PER_GROUP * HEAD_DIM),
                               lambda bi, gi, i: (bi * nt + i, gi)),
        out_shape=jax.ShapeDtypeStruct((b * seq, NSA_WIDTH), F32),
        scratch_shapes=[
            pltpu.VMEM((SLC_LANES + HEAD_DIM, Q_COLS), BF16),
            pltpu.VMEM((SLC_KEYS, Q_COLS), F32),
            pltpu.VMEM((SLC_KEYS, Q_COLS), F32),
            pltpu.VMEM((1, Q_COLS), F32),
            pltpu.VMEM((1, Q_COLS), F32),
            pltpu.VMEM((1, Q_COLS), F32),
            pltpu.VMEM((V_ROWS, Q_COLS), F32),
            pltpu.VMEM((HEAD_DIM, Q_COLS), F32),
            pltpu.VMEM((HEAD_DIM, Q_COLS), F32),
        ],
        compiler_params=_cparams(3),
        name="nsa_attn",
    )(qt, gt, kc, vct, mapt, ks, vst, kw, vwt)


def _segmented_scans(g):
    rows = g.shape[0]
    row = lax.broadcasted_iota(jnp.int32, g.shape, 0)
    out = {1: (g, g)}
    pre, suf = g, g
    h = 1
    while h < rows:
        pos = row & (2 * h - 1)
        if h < 8:
            add_pre = jnp.zeros_like(g)
            add_suf = jnp.zeros_like(g)
            for shift in range(1, h + 1):
                add_pre = jnp.where(pos == h + shift - 1, pltpu.roll(pre, shift, 0), add_pre)
                add_suf = jnp.where(pos == h - shift, pltpu.roll(suf, rows - shift, 0), add_suf)
            pre = pre + add_pre
            suf = suf + add_suf
        else:
            pre_parts, suf_parts = [], []
            for b in range(rows // (2 * h)):
                lo, mid, hi = b * 2 * h, b * 2 * h + h, (b + 1) * 2 * h
                pre_parts += [pre[lo:mid], pre[mid:hi] + pre[mid - 1:mid, :]]
                suf_parts += [suf[lo:mid] + suf[mid:mid + 1, :], suf[mid:hi]]
            pre = jnp.concatenate(pre_parts, axis=0)
            suf = jnp.concatenate(suf_parts, axis=0)
        h *= 2
        out[h] = (pre, suf)
    return out


def _hgrn_kernel(hg_ref, lb_ref, gain_ref, o_ref, state_ref):
    c = pl.program_id(1)

    @pl.when(c == 0)
    def _():
        state_ref[...] = jnp.zeros_like(state_ref)

    w = HG_WIDTH
    q = hg_ref[:, 0:w]
    f_logit = hg_ref[:, w:2 * w]
    v = hg_ref[:, 2 * w:3 * w]
    og = hg_ref[:, 3 * w:4 * w]
    lb = lb_ref[...]
    f = lb + (1.0 - lb) * jax.nn.sigmoid(f_logit)
    logf = jnp.log(f)
    k = 1.0 - f
    qf = q * jax.nn.sigmoid(q)
    vb = v.astype(BF16)

    scans = _segmented_scans(logf)
    rows = HG_CHUNK
    t_idx = lax.broadcasted_iota(jnp.int32, (rows, rows), 0)
    s_idx = lax.broadcasted_iota(jnp.int32, (rows, rows), 1)
    diff = jnp.where(t_idx > s_idx, t_idx ^ s_idx, 0)

    def heads(x):
        return [x[:, h * HG_DIM:(h + 1) * HG_DIM] for h in range(HG_HEADS)]

    nt_dims = (((1,), (1,)), ((), ()))
    qb, kb = heads(qf.astype(BF16)), heads(k.astype(BF16))
    a = [jnp.where(t_idx == s_idx,
                   lax.dot_general(qb[h], kb[h], nt_dims, preferred_element_type=F32), 0.0)
         for h in range(HG_HEADS)]
    level, h_size = 0, 1
    while h_size < rows:
        pre, suf = scans[h_size]
        ql = heads((qf * jnp.exp(pre)).astype(BF16))
        kl = heads((k * jnp.exp(suf - logf)).astype(BF16))
        mask = (diff >> level) == 1
        for h in range(HG_HEADS):
            al = lax.dot_general(ql[h], kl[h], nt_dims, preferred_element_type=F32)
            a[h] = jnp.where(mask, al, a[h])
        level += 1
        h_size *= 2

    cum, rev = scans[rows]
    q_in = heads((qf * jnp.exp(cum)).astype(BF16))
    k_out = heads((k * jnp.exp(rev - logf)).astype(BF16))
    vh = heads(vb)
    gain = gain_ref[...]
    for h in range(HG_HEADS):
        lanes = slice(h * HG_DIM, (h + 1) * HG_DIM)
        state_t = state_ref[h]
        o = jnp.dot(a[h].astype(BF16), vh[h], preferred_element_type=F32)
        o = o + lax.dot_general(q_in[h], state_t.astype(BF16), nt_dims,
                                preferred_element_type=F32)
        decay = jnp.exp(cum[rows - 1:rows, lanes])
        v_t = v[:, lanes].T.astype(BF16)
        state_ref[h] = state_t * decay + jnp.dot(v_t, k_out[h], preferred_element_type=F32)
        ms = jnp.mean(o * o, axis=-1, keepdims=True)
        gate = og[:, lanes]
        o_ref[:, lanes] = o * lax.rsqrt(ms + EPS) * gain[:, lanes] * (gate * jax.nn.sigmoid(gate))


def _hgrn(hg_proj, lb, gain, batch, seq):
    n = hg_proj.shape[0]
    n_chunk = seq // HG_CHUNK
    row = lambda b, c: (b * n_chunk + c, 0)
    const = lambda b, c: (0, 0)
    return pl.pallas_call(
        _hgrn_kernel,
        grid=(batch, n_chunk),
        in_specs=[
            pl.BlockSpec((HG_CHUNK, 4 * HG_WIDTH), row),
            pl.BlockSpec((1, HG_WIDTH), const),
            pl.BlockSpec((1, HG_WIDTH), const),
        ],
        out_specs=pl.BlockSpec((HG_CHUNK, HG_WIDTH), row),
        out_shape=jax.ShapeDtypeStruct((n, HG_WIDTH), F32),
        scratch_shapes=[pltpu.VMEM((HG_HEADS, HG_DIM, HG_DIM), F32)],
        compiler_params=_cparams(2),
        name="hgrn",
    )(hg_proj, lb, gain)


def _out_mlp_kernel(x_ref, yn_ref, yh_ref, gn_ref, gm_ref, wa_ref, wb_ref,
                    w1_ref, w2_ref, o_ref, *, ff_chunk):
    yn = yn_ref[...]
    ms = jnp.mean(yn * yn, axis=-1, keepdims=True)
    yn = (yn * lax.rsqrt(ms + EPS) * gn_ref[...]).astype(BF16)
    x1 = (x_ref[...]
          + jnp.dot(yn, wa_ref[...], preferred_element_type=F32)
          + jnp.dot(yh_ref[...].astype(BF16), wb_ref[...], preferred_element_type=F32))
    ms = jnp.mean(x1 * x1, axis=-1, keepdims=True)
    h = (x1 * lax.rsqrt(ms + EPS) * gm_ref[...]).astype(BF16)
    mlp = None
    for c in range(w1_ref.shape[1] // ff_chunk):
        cols = slice(c * ff_chunk, (c + 1) * ff_chunk)
        u = jnp.maximum(jnp.dot(h, w1_ref[:, cols], preferred_element_type=F32), 0.0)
        d = jnp.dot((u * u).astype(BF16), w2_ref[cols, :], preferred_element_type=F32)
        mlp = d if mlp is None else mlp + d
    o_ref[...] = x1 + mlp


def _out_mlp(x2, y_nsa, y_hg, gn, gm, wa, wb, w1, w2, tm=256, ff_chunk=1024):
    n, d = x2.shape
    row = lambda i: (i, 0)
    const = lambda i: (0, 0)

    def resident(shape):
        return pl.BlockSpec(shape, const)

    return pl.pallas_call(
        functools.partial(_out_mlp_kernel, ff_chunk=ff_chunk),
        grid=(n // tm,),
        in_specs=[
            pl.BlockSpec((tm, d), row),
            pl.BlockSpec((tm, y_nsa.shape[1]), row),
            pl.BlockSpec((tm, y_hg.shape[1]), row),
            pl.BlockSpec((1, y_nsa.shape[1]), const),
            pl.BlockSpec((1, d), const),
            resident(wa.shape),
            resident(wb.shape),
            resident(w1.shape),
            resident(w2.shape),
        ],
        out_specs=pl.BlockSpec((tm, d), row),
        out_shape=jax.ShapeDtypeStruct((n, d), F32),
        compiler_params=_cparams(1),
        name="out_mlp",
    )(x2, y_nsa, y_hg, gn, gm, wa, wb, w1, w2)


def _rope_tables(seq):
    pos = jnp.arange(seq, dtype=F32)
    inv_freq = ROPE_THETA ** (-jnp.arange(0, ROT_DIM, 2, dtype=F32) / ROT_DIM)
    ang = pos[:, None] * inv_freq[None, :]
    cos, sin = jnp.cos(ang), jnp.sin(ang)
    rest = HEAD_DIM - ROT_DIM
    cos_h = jnp.concatenate([cos, cos, jnp.ones((seq, rest), F32)], axis=-1)
    sin_h = jnp.concatenate([-sin, sin, jnp.zeros((seq, rest), F32)], axis=-1)
    return jnp.tile(cos_h, (1, 2)), jnp.tile(sin_h, (1, 2))


def _cmp_to_slc_t(n_cmp_rows, n_cmp, n_slc):
    tok = np.arange(n_cmp)[:, None] * CMP_STRIDE + np.arange(CMP_LEN)[None, :]
    frac = (tok[:, :, None] // SLC_BLOCK == np.arange(n_slc)[None, None, :]).mean(axis=1)
    out = np.zeros((SLC_LANES, n_cmp_rows), np.float32)
    out[:n_slc, :n_cmp] = frac.T
    return jnp.asarray(out, BF16)


def kernel(x, attn_norm, w_in, q_norm, k_norm, cmp_pe_k, cmp_w1_k, cmp_w2_k, cmp_pe_v,
           cmp_w1_v, cmp_w2_v, nsa_out_norm, hgrn_lb, hgrn_out_norm, w_out, mlp_norm,
           w_mlp_in, w_mlp_out):
    batch, seq, d_model = x.shape
    n = batch * seq
    n_slc = seq // SLC_BLOCK
    n_cmp_rows = seq // CMP_STRIDE
    n_cmp = (seq - CMP_LEN) // CMP_STRIDE + 1
    assert n_slc <= SLC_LANES and seq % SLC_KEYS == 0 and seq >= WINDOW + Q_TILE
    assert w_in.shape[0] == 1, "single-layer block"

    kvw = KV_WIDTH
    o_q, o_kc, o_vc, o_ks, o_vs, o_kw, o_vw, o_gate = np.cumsum(
        [0, NSA_WIDTH, kvw, kvw, kvw, kvw, kvw, kvw])
    o_hg = o_gate + NSA_HEADS * 3
    w = w_in[0]
    pad = jnp.zeros((d_model, NSA_COLS - (NSA_WIDTH + 6 * kvw + NSA_HEADS * 3)), w.dtype)
    w_nsa = jnp.concatenate(
        [w[:, o_q:o_kc], w[:, o_kc:o_vc], w[:, o_ks:o_vs], w[:, o_kw:o_vw],
         w[:, o_vc:o_ks], w[:, o_vs:o_kw], w[:, o_vw:o_gate], w[:, o_gate:o_hg], pad],
        axis=1).astype(BF16)
    w_hg = w[:, o_hg:].astype(BF16)
    x2 = x.reshape(n, d_model)
    nsa_proj, hg_proj = _inproj(x2, attn_norm[0][None, :], w_nsa, w_hg)

    cos_t, sin_t = _rope_tables(seq)
    qg = jnp.tile(q_norm[0], 2)[None, :]
    kg = jnp.tile(k_norm[0], 2)[None, :]
    qt, kc, ks_aug, kw, vc, vs_t, vw_t, gt = _nsa_prep(nsa_proj, cos_t, sin_t, qg, kg, batch, seq)

    blocks = lambda t: t.reshape(batch * NSA_GROUPS, n_cmp_rows, CMP_STRIDE * HEAD_DIM)
    pe8 = lambda pe: jnp.broadcast_to(pe.reshape(1, CMP_LEN * HEAD_DIM), (8, CMP_LEN * HEAD_DIM)).astype(BF16)
    k_cmp, v_cmp_t = _compress(
        blocks(kc), blocks(vc), pe8(cmp_pe_k[0]), pe8(cmp_pe_v[0]),
        cmp_w1_k[0].astype(BF16), cmp_w1_v[0].astype(BF16),
        cmp_w2_k[0].astype(BF16), cmp_w2_v[0].astype(BF16))
    k_cmp = k_cmp.reshape(batch, NSA_GROUPS, n_cmp_rows, HEAD_DIM)
    v_cmp_t = v_cmp_t.reshape(batch, NSA_GROUPS, V_ROWS, n_cmp_rows)
    y_nsa = _nsa_attn(qt, gt, k_cmp, v_cmp_t, _cmp_to_slc_t(n_cmp_rows, n_cmp, n_slc),
                      ks_aug, vs_t, kw, vw_t)

    lb_all = jnp.cumsum(jax.nn.softmax(hgrn_lb.astype(F32), axis=0), axis=0)
    y_hg = _hgrn(hg_proj, lb_all[0][None, :], jnp.tile(hgrn_out_norm[0], HG_HEADS)[None, :],
                 batch, seq)

    wo = w_out[0].astype(BF16)
    out = _out_mlp(x2, y_nsa, y_hg, nsa_out_norm[0][None, :], mlp_norm[0][None, :],
                   wo[:NSA_WIDTH], wo[NSA_WIDTH:], w_mlp_in[0].astype(BF16),
                   w_mlp_out[0].astype(BF16))
    return out.reshape(batch, seq, d_model)
```

```python
import functools

import jax
import jax.numpy as jnp
import numpy as np
from jax import lax
from jax.experimental import pallas as pl
from jax.experimental.pallas import tpu as pltpu

F32 = jnp.float32
BF16 = jnp.bfloat16

NSA_HEADS = 8
NSA_GROUPS = 2
HEADS_PER_GROUP = NSA_HEADS // NSA_GROUPS
HEAD_DIM = 64
NSA_WIDTH = NSA_HEADS * HEAD_DIM
KV_WIDTH = NSA_GROUPS * HEAD_DIM
CMP_LEN = 32
CMP_STRIDE = 16
CMP_HIDDEN = 2 * HEAD_DIM
SLC_BLOCK = 64
SLC_SHIFT = 6
SLC_TOPN = 16
WINDOW = 512
ROT_DIM = HEAD_DIM // 4
ROPE_THETA = 500000.0
HG_HEADS = 4
HG_DIM = 128
HG_WIDTH = HG_HEADS * HG_DIM
EPS = 1e-6

LANES = 128
Q_TILE = 256
Q_COLS = HEADS_PER_GROUP * Q_TILE
SLC_KEYS = 512
SLC_LANES = 128
MXU_COLS = 256
V_ROWS = 80
GATE_ROWS = 16
NSA_COLS = 1408
NEG = -1e30
Q_SCALE = HEAD_DIM ** -0.5 * 1.4426950408889634
BIG = 2.0 ** 100
HG_CHUNK = 128
VMEM_LIMIT = 56 * 1024 * 1024


def _cparams(n_grid):
    return pltpu.CompilerParams(
        dimension_semantics=("arbitrary",) * n_grid, vmem_limit_bytes=VMEM_LIMIT)


def _value_rows_tail(cols):
    r = lax.broadcasted_iota(jnp.int32, (V_ROWS - HEAD_DIM, cols), 0)
    return jnp.where(r == 0, 1.0, 0.0).astype(BF16)


def _inproj_kernel(x_ref, g_ref, wn_ref, wh_ref, cos_ref, sin_ref, qg_ref, kg_ref,
                   hg_ref, qt_ref, kc_ref, ks_ref, kw_ref, vc_ref, vst_ref, vwt_ref, gt_ref):
    tq = x_ref.shape[0]
    x = x_ref[...]
    ms = jnp.mean(x * x, axis=-1, keepdims=True)
    h = (x * lax.rsqrt(ms + EPS) * g_ref[...]).astype(BF16)
    p = jnp.dot(h, wn_ref[...], preferred_element_type=F32)
    hg_ref[...] = jnp.dot(h, wh_ref[...], preferred_element_type=F32)
    cos = cos_ref[...]
    sin = sin_ref[...]
    lane = lax.broadcasted_iota(jnp.int32, (tq, LANES), 1)
    low_head = lane < HEAD_DIM
    dim = lane & (HEAD_DIM - 1)
    first_half = dim < ROT_DIM // 2

    def norm_rope(xs, gain, scale):
        x2 = xs * xs
        s_lo = jnp.sum(jnp.where(low_head, x2, 0.0), axis=-1, keepdims=True)
        s_hi = jnp.sum(jnp.where(low_head, 0.0, x2), axis=-1, keepdims=True)
        ms = jnp.where(low_head, s_lo, s_hi) * (1.0 / HEAD_DIM)
        y = xs * lax.rsqrt(ms + EPS) * gain
        partner = jnp.where(first_half,
                            pltpu.roll(y, LANES - ROT_DIM // 2, 1),
                            pltpu.roll(y, ROT_DIM // 2, 1))
        return (y * cos + partner * sin) * scale

    def split_groups(ref, x, lanes=slice(None)):
        for g in range(NSA_GROUPS):
            ref[g, :, lanes] = x[:, g * HEAD_DIM:(g + 1) * HEAD_DIM].astype(BF16)

    def split_groups_t(ref, x):
        xt = x.T
        for g in range(NSA_GROUPS):
            ref[g, 0:HEAD_DIM, :] = xt[g * HEAD_DIM:(g + 1) * HEAD_DIM, :].astype(BF16)
            ref[g, HEAD_DIM:V_ROWS, :] = _value_rows_tail(tq)

    qg = qg_ref[...]
    kg = kg_ref[...]
    for s in range(NSA_WIDTH // LANES):
        yt = norm_rope(p[:, s * LANES:(s + 1) * LANES], qg, Q_SCALE).T
        g, h0 = divmod(2 * s, HEADS_PER_GROUP)
        for j in range(2):
            qt_ref[g, :, (h0 + j) * tq:(h0 + j + 1) * tq] = (
                yt[j * HEAD_DIM:(j + 1) * HEAD_DIM, :].astype(BF16))
    base = NSA_WIDTH
    split_groups(kc_ref, norm_rope(p[:, base:base + LANES], kg, 1.0))
    tok = pl.program_id(1) * tq + lax.broadcasted_iota(jnp.int32, (tq, SLC_LANES), 0)
    blk = lax.broadcasted_iota(jnp.int32, (tq, SLC_LANES), 1)
    one_hot = jnp.where((tok >> SLC_SHIFT) == blk, 1.0, 0.0).astype(BF16)
    for g in range(NSA_GROUPS):
        ks_ref[g, :, 0:SLC_LANES] = one_hot
    split_groups(ks_ref, norm_rope(p[:, base + LANES:base + 2 * LANES], kg, 1.0),
                 slice(SLC_LANES, SLC_LANES + HEAD_DIM))
    split_groups(kw_ref, norm_rope(p[:, base + 2 * LANES:base + 3 * LANES], kg, 1.0))
    base += 3 * LANES
    split_groups(vc_ref, p[:, base:base + LANES])
    split_groups_t(vst_ref, p[:, base + LANES:base + 2 * LANES])
    split_groups_t(vwt_ref, p[:, base + 2 * LANES:base + 3 * LANES])
    base += 3 * LANES
    gates_t = jax.nn.sigmoid(p[:, base:base + LANES]).T
    per_group = HEADS_PER_GROUP * 3
    for g in range(NSA_GROUPS):
        gt_ref[g, 0:per_group, :] = gates_t[g * per_group:(g + 1) * per_group, :]
        gt_ref[g, per_group:GATE_ROWS, :] = jnp.zeros((GATE_ROWS - per_group, tq), F32)


def _inproj(x2, gain, w_nsa, w_hg, cos_t, sin_t, qg, kg, batch, seq):
    tq = Q_TILE
    nt = seq // tq
    d = x2.shape[1]
    tok = lambda b, i: (b * nt + i, 0)
    const = lambda b, i: (0, 0)
    rows = lambda b, i: (b, 0, i, 0)
    cols = lambda b, i: (b, 0, 0, i)
    grp = NSA_GROUPS
    return pl.pallas_call(
        _inproj_kernel,
        grid=(batch, nt),
        in_specs=[
            pl.BlockSpec((tq, d), tok),
            pl.BlockSpec((1, d), const),
            pl.BlockSpec(w_nsa.shape, const),
            pl.BlockSpec(w_hg.shape, const),
            pl.BlockSpec((tq, LANES), lambda b, i: (i, 0)),
            pl.BlockSpec((tq, LANES), lambda b, i: (i, 0)),
            pl.BlockSpec((1, LANES), const),
            pl.BlockSpec((1, LANES), const),
        ],
        out_specs=[
            pl.BlockSpec((tq, w_hg.shape[1]), tok),
            pl.BlockSpec((None, grp, None, HEAD_DIM, Q_COLS), lambda b, i: (b, 0, i, 0, 0)),
            pl.BlockSpec((None, grp, tq, HEAD_DIM), rows),
            pl.BlockSpec((None, grp, tq, SLC_LANES + HEAD_DIM), rows),
            pl.BlockSpec((None, grp, tq, HEAD_DIM), rows),
            pl.BlockSpec((None, grp, tq, HEAD_DIM), rows),
            pl.BlockSpec((None, grp, V_ROWS, tq), cols),
            pl.BlockSpec((None, grp, V_ROWS, tq), cols),
            pl.BlockSpec((None, grp, GATE_ROWS, tq), cols),
        ],
        out_shape=[
            jax.ShapeDtypeStruct((batch * seq, w_hg.shape[1]), F32),
            jax.ShapeDtypeStruct((batch, grp, nt, HEAD_DIM, Q_COLS), BF16),
            jax.ShapeDtypeStruct((batch, grp, seq, HEAD_DIM), BF16),
            jax.ShapeDtypeStruct((batch, grp, seq, SLC_LANES + HEAD_DIM), BF16),
            jax.ShapeDtypeStruct((batch, grp, seq, HEAD_DIM), BF16),
            jax.ShapeDtypeStruct((batch, grp, seq, HEAD_DIM), BF16),
            jax.ShapeDtypeStruct((batch, grp, V_ROWS, seq), BF16),
            jax.ShapeDtypeStruct((batch, grp, V_ROWS, seq), BF16),
            jax.ShapeDtypeStruct((batch, grp, GATE_ROWS, seq), F32),
        ],
        compiler_params=_cparams(2),
        name="inproj",
    )(x2, gain, w_nsa, w_hg, cos_t, sin_t, qg, kg)


def _compress_kernel(tk_ref, tv_ref, pek_ref, pev_ref, w1k_ref, w1v_ref,
                     w2k_ref, w2v_ref, ok_ref, ov_ref):
    half = CMP_STRIDE * HEAD_DIM

    def run(t_ref, pe_ref, w1_ref, w2_ref, o_ref, transposed):
        t = t_ref[...]
        n_rows = t.shape[0]
        top = jnp.dot(t, w1_ref[0:half, :], preferred_element_type=F32)
        bot = jnp.dot(t, w1_ref[half:2 * half, :], preferred_element_type=F32)
        bot_next = pltpu.roll(bot, n_rows - 1, 0)
        pe = jnp.dot(pe_ref[...], w1_ref[...], preferred_element_type=F32)[0:1, :]
        h = top + bot_next + pe
        h = h * jax.nn.sigmoid(h)
        o = jnp.dot(h.astype(BF16), w2_ref[...], preferred_element_type=F32)
        r = lax.broadcasted_iota(jnp.int32, o.shape, 0)
        o = jnp.where(r < n_rows - 1, o, 0.0)
        if transposed:
            ot = jnp.concatenate([o, jnp.zeros_like(o)], axis=1).T
            o_ref[0:HEAD_DIM, :] = ot[0:HEAD_DIM, :].astype(BF16)
            o_ref[HEAD_DIM:V_ROWS, :] = _value_rows_tail(n_rows)
        else:
            o_ref[...] = o.astype(BF16)

    run(tk_ref, pek_ref, w1k_ref, w2k_ref, ok_ref, False)
    run(tv_ref, pev_ref, w1v_ref, w2v_ref, ov_ref, True)


def _compress(tk, tv, pek, pev, w1k, w1v, w2k, w2v):
    bg, n_rows, width = tk.shape
    blk = lambda i: (i, 0, 0)
    const = lambda i: (0, 0)
    return pl.pallas_call(
        _compress_kernel,
        grid=(bg,),
        in_specs=[
            pl.BlockSpec((None, n_rows, width), blk),
            pl.BlockSpec((None, n_rows, width), blk),
            pl.BlockSpec(pek.shape, const),
            pl.BlockSpec(pev.shape, const),
            pl.BlockSpec(w1k.shape, const),
            pl.BlockSpec(w1v.shape, const),
            pl.BlockSpec(w2k.shape, const),
            pl.BlockSpec(w2v.shape, const),
        ],
        out_specs=[pl.BlockSpec((None, n_rows, HEAD_DIM), blk),
                   pl.BlockSpec((None, V_ROWS, n_rows), blk)],
        out_shape=[jax.ShapeDtypeStruct((bg, n_rows, HEAD_DIM), BF16),
                   jax.ShapeDtypeStruct((bg, V_ROWS, n_rows), BF16)],
        compiler_params=_cparams(1),
        name="compress",
    )(tk, tv, pek, pev, w1k, w1v, w2k, w2v)


def _nsa_attn_kernel(qt_ref, gt_ref, kc_ref, vct_ref, mapt_ref, ks_ref, vst_ref,
                     kw_ref, vwt_ref, y_ref, qaug_ref, s0_ref, s1_ref, c0_ref, c1_ref,
                     m_ref, acc_ref, ocmp_ref, owin_ref):
    i = pl.program_id(2)
    q0 = i * Q_TILE
    qt = qt_ref[...]
    col = lax.broadcasted_iota(jnp.int32, (1, Q_COLS), 1)
    tpos = q0 + (col & (Q_TILE - 1))

    def cmp_and_select(n_rows, n_blk):
        s = jnp.dot(kc_ref[0:n_rows, :], qt, preferred_element_type=F32)
        cmp_end = lax.broadcasted_iota(jnp.int32, (n_rows, 1), 0) * CMP_STRIDE + (CMP_LEN - 1)
        s = jnp.where(cmp_end <= tpos, s, NEG)
        m = jnp.max(s, axis=0, keepdims=True)
        p = jnp.exp2(s - jnp.maximum(m, 0.1 * NEG)).astype(BF16)
        acc = jnp.dot(vct_ref[:, 0:n_rows], p, preferred_element_type=F32)
        l = acc[HEAD_DIM:HEAD_DIM + 1, :]
        inv = jnp.where(l > 0.0, 1.0 / l, 0.0)
        ocmp_ref[...] = acc[0:HEAD_DIM, :] * inv

        mapt = mapt_ref[0:n_blk, 0:n_rows]
        imp = None
        for h in range(HEADS_PER_GROUP):
            cols = slice(h * Q_TILE, (h + 1) * Q_TILE)
            part = jnp.dot(mapt, p[:, cols], preferred_element_type=F32) * inv[:, cols]
            imp = part if imp is None else imp + part

        blk_i = lax.broadcasted_iota(jnp.int32, (n_blk, Q_TILE), 0)
        cur = (q0 + lax.broadcasted_iota(jnp.int32, (n_blk, Q_TILE), 1)) >> SLC_SHIFT
        blk = blk_i.astype(F32)
        val = jnp.where(blk_i <= cur, imp, -1.0)
        for forced in (blk_i == 0, blk_i == cur, blk_i == cur - 1):
            val = jnp.where(forced, 1e30, val)
        sel = jnp.zeros((n_blk, Q_TILE), F32)
        for _ in range(min(SLC_TOPN, n_blk)):
            mx = jnp.max(val, axis=0, keepdims=True)
            first = jnp.min(jnp.where(val == mx, blk, float(n_blk)), axis=0, keepdims=True)
            first = jnp.where(mx > -0.5, first, -1.0)
            pick = blk == first
            sel = jnp.where(pick, 1.0, sel)
            val = jnp.where(pick, -1.0, val)
        sel_bias = jnp.where(sel > 0.5, 0.0, -BIG).astype(BF16)
        for h in range(HEADS_PER_GROUP):
            qaug_ref[0:n_blk, h * Q_TILE:(h + 1) * Q_TILE] = sel_bias
        if n_blk < SLC_LANES:
            qaug_ref[n_blk:SLC_LANES, :] = jnp.full((SLC_LANES - n_blk, Q_COLS), -BIG, BF16)

    col_tiles = [slice(c0, c0 + MXU_COLS) for c0 in range(0, Q_COLS, MXU_COLS)]
    buffers = ((s0_ref, c0_ref), (s1_ref, c1_ref))

    def slc_scores(k0, s_ref, cmax_ref, cols):
        sc = jnp.dot(ks_ref[pl.ds(k0, SLC_KEYS), :], qaug_ref[:, cols],
                     preferred_element_type=F32)
        s_ref[:, cols] = sc
        cmax_ref[:, cols] = jnp.max(sc, axis=0, keepdims=True)

    span = WINDOW + Q_TILE
    t_loc = col & (Q_TILE - 1)

    def window_scores(interior):
        if not interior:
            sw = jnp.dot(kw_ref[0:span, :], qt, preferred_element_type=F32)
            kpos = lax.broadcasted_iota(jnp.int32, (span, 1), 0)
            sw = jnp.where(kpos <= tpos, jnp.where(kpos > tpos - WINDOW, sw, NEG), NEG)
            return [sw], 0
        start = pl.multiple_of(q0 - WINDOW, Q_TILE)
        i_loc = lax.broadcasted_iota(jnp.int32, (Q_TILE, 1), 0)
        n_parts = span // Q_TILE
        parts = []
        for r in range(n_parts):
            sw = jnp.dot(kw_ref[pl.ds(start + r * Q_TILE, Q_TILE), :], qt,
                         preferred_element_type=F32)
            if r == 0:
                sw = jnp.where(i_loc > t_loc, sw, NEG)
            elif r == n_parts - 1:
                sw = jnp.where(i_loc <= t_loc, sw, NEG)
            parts.append(sw)
        return parts, start

    def window_finish(parts, v_start):
        mw = None
        for sw in parts:
            cm = jnp.max(sw, axis=0, keepdims=True)
            mw = cm if mw is None else jnp.maximum(mw, cm)
        accw, off = None, 0
        for sw in parts:
            pw = jnp.exp2(sw - mw).astype(BF16)
            d = jnp.dot(vwt_ref[:, pl.ds(v_start + off, sw.shape[0])], pw,
                        preferred_element_type=F32)
            accw = d if accw is None else accw + d
            off += sw.shape[0]
        owin_ref[...] = accw[0:HEAD_DIM, :] / accw[HEAD_DIM:HEAD_DIM + 1, :]

    def front(n_rows, n_blk, interior):
        parts, v_start = window_scores(interior)
        cmp_and_select(n_rows, n_blk)
        qaug_ref[SLC_LANES:SLC_LANES + HEAD_DIM, :] = qt
        for cols in col_tiles:
            slc_scores(0, *buffers[0], cols)
        window_finish(parts, v_start)

    n_cmp_rows = kc_ref.shape[0]
    row_steps = list(range(LANES, n_cmp_rows, LANES)) + [n_cmp_rows]
    q_end = q0 + Q_TILE
    lo = 0
    for n_rows in row_steps:
        hi = n_rows * CMP_STRIDE
        n_blk = min(SLC_LANES, -(-(hi // SLC_BLOCK) // 16) * 16)
        cuts = [lo, WINDOW, hi] if lo < WINDOW < hi else [lo, hi]
        for c_lo, c_hi in zip(cuts[:-1], cuts[1:]):
            pl.when(jnp.logical_and(q_end > c_lo, q_end <= c_hi))(
                functools.partial(front, n_rows, n_blk, c_lo >= WINDOW))
        lo = hi

    def slc_update(k0, s_ref, cmax_ref, cols, masked):
        sc = s_ref[:, cols]
        if masked:
            kpos = k0 + lax.broadcasted_iota(jnp.int32, (SLC_KEYS, 1), 0)
            sc = jnp.where(kpos <= tpos[:, cols], sc, -BIG)
            cmax = jnp.max(sc, axis=0, keepdims=True)
        else:
            cmax = cmax_ref[:, cols]
        m_prev = m_ref[:, cols]
        m_new = jnp.maximum(m_prev, cmax)
        pr = jnp.exp2(sc - m_new).astype(BF16)
        pv = jnp.dot(vst_ref[:, pl.ds(k0, SLC_KEYS)], pr, preferred_element_type=F32)
        acc_ref[:, cols] = acc_ref[:, cols] * jnp.exp2(m_prev - m_new) + pv
        m_ref[:, cols] = m_new

    def slc_body(kt, carry):
        k0 = pl.multiple_of(kt * SLC_KEYS, SLC_KEYS)
        for parity in range(2):
            @pl.when((kt & 1) == parity)
            def _():
                for cols in col_tiles:
                    slc_scores(k0 + SLC_KEYS, *buffers[1 - parity], cols)
                    slc_update(k0, *buffers[parity], cols, masked=False)
        return carry

    n_full = q0 // SLC_KEYS
    m_ref[...] = jnp.full(m_ref.shape, NEG, F32)
    acc_ref[...] = jnp.zeros(acc_ref.shape, F32)
    lax.fori_loop(0, n_full, slc_body, 0)
    k_last = pl.multiple_of(n_full * SLC_KEYS, SLC_KEYS)
    for parity in range(2):
        @pl.when((n_full & 1) == parity)
        def _():
            for cols in col_tiles:
                slc_update(k_last, *buffers[parity], cols, masked=True)
    o_slc = acc_ref[0:HEAD_DIM, :] / acc_ref[HEAD_DIM:HEAD_DIM + 1, :]

    g = gt_ref[...]

    def gate_row(branch):
        return jnp.concatenate(
            [g[h * 3 + branch:h * 3 + branch + 1, :] for h in range(HEADS_PER_GROUP)], axis=1)

    y = (gate_row(0) * ocmp_ref[...] + gate_row(1) * o_slc
         + gate_row(2) * owin_ref[...])
    for pair in range(HEADS_PER_GROUP // 2):
        lo, mid, hi = (2 * pair) * Q_TILE, (2 * pair + 1) * Q_TILE, (2 * pair + 2) * Q_TILE
        stacked = jnp.concatenate([y[:, lo:mid], y[:, mid:hi]], axis=0)
        y_ref[:, pair * LANES:(pair + 1) * LANES] = stacked.T


def _nsa_attn(qt, gt, kc, vct, mapt, ks, vst, kw, vwt):
    b, g, nt = qt.shape[:3]
    seq = ks.shape[2]
    n_cmp_rows = kc.shape[2]
    per_bg = lambda bi, gi, i: (bi, gi, 0, 0)
    return pl.pallas_call(
        _nsa_attn_kernel,
        grid=(b, g, nt),
        in_specs=[
            pl.BlockSpec((None, None, None, HEAD_DIM, Q_COLS), lambda bi, gi, i: (bi, gi, i, 0, 0)),
            pl.BlockSpec((None, None, GATE_ROWS, Q_TILE), lambda bi, gi, i: (bi, gi, 0, i)),
            pl.BlockSpec((None, None, n_cmp_rows, HEAD_DIM), per_bg),
            pl.BlockSpec((None, None, V_ROWS, n_cmp_rows), per_bg),
            pl.BlockSpec(mapt.shape, lambda bi, gi, i: (0, 0)),
            pl.BlockSpec((None, None, seq, HEAD_DIM + SLC_LANES), per_bg),
            pl.BlockSpec((None, None, V_ROWS, seq), per_bg),
            pl.BlockSpec((None, None, seq, HEAD_DIM), per_bg),
            pl.BlockSpec((None, None, V_ROWS, seq), per_bg),
        ],
        out_specs=pl.BlockSpec((Q_TILE, HEADS_PER_GROUP * HEAD_DIM),
                               lambda bi, gi, i: (bi * nt + i, gi)),
        out_shape=jax.ShapeDtypeStruct((b * seq, NSA_WIDTH), F32),
        scratch_shapes=[
            pltpu.VMEM((SLC_LANES + HEAD_DIM, Q_COLS), BF16),
            pltpu.VMEM((SLC_KEYS, Q_COLS), F32),
            pltpu.VMEM((SLC_KEYS, Q_COLS), F32),
            pltpu.VMEM((1, Q_COLS), F32),
            pltpu.VMEM((1, Q_COLS), F32),
            pltpu.VMEM((1, Q_COLS), F32),
            pltpu.VMEM((V_ROWS, Q_COLS), F32),
            pltpu.VMEM((HEAD_DIM, Q_COLS), F32),
            pltpu.VMEM((HEAD_DIM, Q_COLS), F32),
        ],
        compiler_params=_cparams(3),
        name="nsa_attn",
    )(qt, gt, kc, vct, mapt, ks, vst, kw, vwt)


def _segmented_scans(g):
    rows = g.shape[0]
    row = lax.broadcasted_iota(jnp.int32, g.shape, 0)
    out = {1: (g, g)}
    pre, suf = g, g
    h = 1
    while h < rows:
        pos = row & (2 * h - 1)
        if h < 8:
            add_pre = jnp.zeros_like(g)
            add_suf = jnp.zeros_like(g)
            for shift in range(1, h + 1):
                add_pre = jnp.where(pos == h + shift - 1, pltpu.roll(pre, shift, 0), add_pre)
                add_suf = jnp.where(pos == h - shift, pltpu.roll(suf, rows - shift, 0), add_suf)
            pre = pre + add_pre
            suf = suf + add_suf
        else:
            pre_parts, suf_parts = [], []
            for b in range(rows // (2 * h)):
                lo, mid, hi = b * 2 * h, b * 2 * h + h, (b + 1) * 2 * h
                pre_parts += [pre[lo:mid], pre[mid:hi] + pre[mid - 1:mid, :]]
                suf_parts += [suf[lo:mid] + suf[mid:mid + 1, :], suf[mid:hi]]
            pre = jnp.concatenate(pre_parts, axis=0)
            suf = jnp.concatenate(suf_parts, axis=0)
        h *= 2
        out[h] = (pre, suf)
    return out


def _hgrn_kernel(hg_ref, lb_ref, gain_ref, o_ref, state_ref):
    c = pl.program_id(1)

    @pl.when(c == 0)
    def _():
        state_ref[...] = jnp.zeros_like(state_ref)

    w = HG_WIDTH
    q = hg_ref[:, 0:w]
    f_logit = hg_ref[:, w:2 * w]
    v = hg_ref[:, 2 * w:3 * w]
    og = hg_ref[:, 3 * w:4 * w]
    lb = lb_ref[...]
    f = lb + (1.0 - lb) * jax.nn.sigmoid(f_logit)
    logf = jnp.log(f)
    k = 1.0 - f
    qf = q * jax.nn.sigmoid(q)
    vb = v.astype(BF16)

    scans = _segmented_scans(logf)
    rows = HG_CHUNK
    t_idx = lax.broadcasted_iota(jnp.int32, (rows, rows), 0)
    s_idx = lax.broadcasted_iota(jnp.int32, (rows, rows), 1)
    diff = jnp.where(t_idx > s_idx, t_idx ^ s_idx, 0)

    def heads(x):
        return [x[:, h * HG_DIM:(h + 1) * HG_DIM] for h in range(HG_HEADS)]

    nt_dims = (((1,), (1,)), ((), ()))
    qb, kb = heads(qf.astype(BF16)), heads(k.astype(BF16))
    a = [jnp.where(t_idx == s_idx,
                   lax.dot_general(qb[h], kb[h], nt_dims, preferred_element_type=F32), 0.0)
         for h in range(HG_HEADS)]
    level, h_size = 0, 1
    while h_size < rows:
        pre, suf = scans[h_size]
        ql = heads((qf * jnp.exp(pre)).astype(BF16))
        kl = heads((k * jnp.exp(suf - logf)).astype(BF16))
        mask = (diff >> level) == 1
        for h in range(HG_HEADS):
            al = lax.dot_general(ql[h], kl[h], nt_dims, preferred_element_type=F32)
            a[h] = jnp.where(mask, al, a[h])
        level += 1
        h_size *= 2

    cum, rev = scans[rows]
    q_in = heads((qf * jnp.exp(cum)).astype(BF16))
    k_out = heads((k * jnp.exp(rev - logf)).astype(BF16))
    vh = heads(vb)
    gain = gain_ref[...]
    for h in range(HG_HEADS):
        lanes = slice(h * HG_DIM, (h + 1) * HG_DIM)
        state_t = state_ref[h]
        o = jnp.dot(a[h].astype(BF16), vh[h], preferred_element_type=F32)
        o = o + lax.dot_general(q_in[h], state_t.astype(BF16), nt_dims,
                                preferred_element_type=F32)
        decay = jnp.exp(cum[rows - 1:rows, lanes])
        v_t = v[:, lanes].T.astype(BF16)
        state_ref[h] = state_t * decay + jnp.dot(v_t, k_out[h], preferred_element_type=F32)
        ms = jnp.mean(o * o, axis=-1, keepdims=True)
        gate = og[:, lanes]
        o_ref[:, lanes] = o * lax.rsqrt(ms + EPS) * gain[:, lanes] * (gate * jax.nn.sigmoid(gate))


def _hgrn(hg_proj, lb, gain, batch, seq):
    n = hg_proj.shape[0]
    n_chunk = seq // HG_CHUNK
    row = lambda b, c: (b * n_chunk + c, 0)
    const = lambda b, c: (0, 0)
    return pl.pallas_call(
        _hgrn_kernel,
        grid=(batch, n_chunk),
        in_specs=[
            pl.BlockSpec((HG_CHUNK, 4 * HG_WIDTH), row),
            pl.BlockSpec((1, HG_WIDTH), const),
            pl.BlockSpec((1, HG_WIDTH), const),
        ],
        out_specs=pl.BlockSpec((HG_CHUNK, HG_WIDTH), row),
        out_shape=jax.ShapeDtypeStruct((n, HG_WIDTH), F32),
        scratch_shapes=[pltpu.VMEM((HG_HEADS, HG_DIM, HG_DIM), F32)],
        compiler_params=_cparams(2),
        name="hgrn",
    )(hg_proj, lb, gain)


def _out_mlp_kernel(x_ref, yn_ref, yh_ref, gn_ref, gm_ref, wa_ref, wb_ref,
                    w1_ref, w2_ref, o_ref, *, ff_chunk):
    yn = yn_ref[...]
    ms = jnp.mean(yn * yn, axis=-1, keepdims=True)
    yn = (yn * lax.rsqrt(ms + EPS) * gn_ref[...]).astype(BF16)
    x1 = (x_ref[...]
          + jnp.dot(yn, wa_ref[...], preferred_element_type=F32)
          + jnp.dot(yh_ref[...].astype(BF16), wb_ref[...], preferred_element_type=F32))
    ms = jnp.mean(x1 * x1, axis=-1, keepdims=True)
    h = (x1 * lax.rsqrt(ms + EPS) * gm_ref[...]).astype(BF16)
    mlp = None
    for c in range(w1_ref.shape[1] // ff_chunk):
        cols = slice(c * ff_chunk, (c + 1) * ff_chunk)
        u = jnp.maximum(jnp.dot(h, w1_ref[:, cols], preferred_element_type=F32), 0.0)
        d = jnp.dot((u * u).astype(BF16), w2_ref[cols, :], preferred_element_type=F32)
        mlp = d if mlp is None else mlp + d
    o_ref[...] = x1 + mlp


def _out_mlp(x2, y_nsa, y_hg, gn, gm, wa, wb, w1, w2, tm=256, ff_chunk=1024):
    n, d = x2.shape
    row = lambda i: (i, 0)
    const = lambda i: (0, 0)

    def resident(shape):
        return pl.BlockSpec(shape, const)

    return pl.pallas_call(
        functools.partial(_out_mlp_kernel, ff_chunk=ff_chunk),
        grid=(n // tm,),
        in_specs=[
            pl.BlockSpec((tm, d), row),
            pl.BlockSpec((tm, y_nsa.shape[1]), row),
            pl.BlockSpec((tm, y_hg.shape[1]), row),
            pl.BlockSpec((1, y_nsa.shape[1]), const),
            pl.BlockSpec((1, d), const),
            resident(wa.shape),
            resident(wb.shape),
            resident(w1.shape),
            resident(w2.shape),
        ],
        out_specs=pl.BlockSpec((tm, d), row),
        out_shape=jax.ShapeDtypeStruct((n, d), F32),
        compiler_params=_cparams(1),
        name="out_mlp",
    )(x2, y_nsa, y_hg, gn, gm, wa, wb, w1, w2)


def _rope_tables(seq):
    pos = jnp.arange(seq, dtype=F32)
    inv_freq = ROPE_THETA ** (-jnp.arange(0, ROT_DIM, 2, dtype=F32) / ROT_DIM)
    ang = pos[:, None] * inv_freq[None, :]
    cos, sin = jnp.cos(ang), jnp.sin(ang)
    rest = HEAD_DIM - ROT_DIM
    cos_h = jnp.concatenate([cos, cos, jnp.ones((seq, rest), F32)], axis=-1)
    sin_h = jnp.concatenate([-sin, sin, jnp.zeros((seq, rest), F32)], axis=-1)
    return jnp.tile(cos_h, (1, 2)), jnp.tile(sin_h, (1, 2))


def _cmp_to_slc_t(n_cmp_rows, n_cmp, n_slc):
    tok = np.arange(n_cmp)[:, None] * CMP_STRIDE + np.arange(CMP_LEN)[None, :]
    frac = (tok[:, :, None] // SLC_BLOCK == np.arange(n_slc)[None, None, :]).mean(axis=1)
    out = np.zeros((SLC_LANES, n_cmp_rows), np.float32)
    out[:n_slc, :n_cmp] = frac.T
    return jnp.asarray(out, BF16)


def kernel(x, attn_norm, w_in, q_norm, k_norm, cmp_pe_k, cmp_w1_k, cmp_w2_k, cmp_pe_v,
           cmp_w1_v, cmp_w2_v, nsa_out_norm, hgrn_lb, hgrn_out_norm, w_out, mlp_norm,
           w_mlp_in, w_mlp_out):
    batch, seq, d_model = x.shape
    n = batch * seq
    n_slc = seq // SLC_BLOCK
    n_cmp_rows = seq // CMP_STRIDE
    n_cmp = (seq - CMP_LEN) // CMP_STRIDE + 1
    assert n_slc <= SLC_LANES and seq % SLC_KEYS == 0 and seq >= WINDOW + Q_TILE
    assert w_in.shape[0] == 1, "single-layer block"

    kvw = KV_WIDTH
    o_q, o_kc, o_vc, o_ks, o_vs, o_kw, o_vw, o_gate = np.cumsum(
        [0, NSA_WIDTH, kvw, kvw, kvw, kvw, kvw, kvw])
    o_hg = o_gate + NSA_HEADS * 3
    w = w_in[0]
    pad = jnp.zeros((d_model, NSA_COLS - (NSA_WIDTH + 6 * kvw + NSA_HEADS * 3)), w.dtype)
    w_nsa = jnp.concatenate(
        [w[:, o_q:o_kc], w[:, o_kc:o_vc], w[:, o_ks:o_vs], w[:, o_kw:o_vw],
         w[:, o_vc:o_ks], w[:, o_vs:o_kw], w[:, o_vw:o_gate], w[:, o_gate:o_hg], pad],
        axis=1).astype(BF16)
    w_hg = w[:, o_hg:].astype(BF16)
    x2 = x.reshape(n, d_model)
    cos_t, sin_t = _rope_tables(seq)
    qg = jnp.tile(q_norm[0], 2)[None, :]
    kg = jnp.tile(k_norm[0], 2)[None, :]
    hg_proj, qt, kc, ks_aug, kw, vc, vs_t, vw_t, gt = _inproj(
        x2, attn_norm[0][None, :], w_nsa, w_hg, cos_t, sin_t, qg, kg, batch, seq)


    blocks = lambda t: t.reshape(batch * NSA_GROUPS, n_cmp_rows, CMP_STRIDE * HEAD_DIM)
    pe8 = lambda pe: jnp.broadcast_to(pe.reshape(1, CMP_LEN * HEAD_DIM), (8, CMP_LEN * HEAD_DIM)).astype(BF16)
    k_cmp, v_cmp_t = _compress(
        blocks(kc), blocks(vc), pe8(cmp_pe_k[0]), pe8(cmp_pe_v[0]),
        cmp_w1_k[0].astype(BF16), cmp_w1_v[0].astype(BF16),
        cmp_w2_k[0].astype(BF16), cmp_w2_v[0].astype(BF16))
    k_cmp = k_cmp.reshape(batch, NSA_GROUPS, n_cmp_rows, HEAD_DIM)
    v_cmp_t = v_cmp_t.reshape(batch, NSA_GROUPS, V_ROWS, n_cmp_rows)
    y_nsa = _nsa_attn(qt, gt, k_cmp, v_cmp_t, _cmp_to_slc_t(n_cmp_rows, n_cmp, n_slc),
                      ks_aug, vs_t, kw, vw_t)

    lb_all = jnp.cumsum(jax.nn.softmax(hgrn_lb.astype(F32), axis=0), axis=0)
    y_hg = _hgrn(hg_proj, lb_all[0][None, :], jnp.tile(hgrn_out_norm[0], HG_HEADS)[None, :],
                 batch, seq)

    wo = w_out[0].astype(BF16)
    out = _out_mlp(x2, y_nsa, y_hg, nsa_out_norm[0][None, :], mlp_norm[0][None, :],
                   wo[:NSA_WIDTH], wo[NSA_WIDTH:], w_mlp_in[0].astype(BF16),
                   w_mlp_out[0].astype(BF16))
    return out.reshape(batch, seq, d_model)
```

```python
import functools

import jax
import jax.numpy as jnp
import numpy as np
from jax import lax
from jax.experimental import pallas as pl
from jax.experimental.pallas import tpu as pltpu

F32 = jnp.float32
BF16 = jnp.bfloat16

NSA_HEADS = 8
NSA_GROUPS = 2
HEADS_PER_GROUP = NSA_HEADS // NSA_GROUPS
HEAD_DIM = 64
NSA_WIDTH = NSA_HEADS * HEAD_DIM
KV_WIDTH = NSA_GROUPS * HEAD_DIM
CMP_LEN = 32
CMP_STRIDE = 16
CMP_HIDDEN = 2 * HEAD_DIM
SLC_BLOCK = 64
SLC_SHIFT = 6
SLC_TOPN = 16
WINDOW = 512
ROT_DIM = HEAD_DIM // 4
ROPE_THETA = 500000.0
HG_HEADS = 4
HG_DIM = 128
HG_WIDTH = HG_HEADS * HG_DIM
EPS = 1e-6

LANES = 128
Q_TILE = 256
Q_COLS = HEADS_PER_GROUP * Q_TILE
SLC_KEYS = 512
SLC_LANES = 128
MXU_COLS = 256
V_ROWS = 80
GATE_ROWS = 16
NSA_COLS = 1408
NEG = -1e30
Q_SCALE = HEAD_DIM ** -0.5 * 1.4426950408889634
BIG = 2.0 ** 100
HG_CHUNK = 128
HG_STEP = 512
VMEM_LIMIT = 56 * 1024 * 1024


def _cparams(n_grid):
    return pltpu.CompilerParams(
        dimension_semantics=("arbitrary",) * n_grid, vmem_limit_bytes=VMEM_LIMIT)


def _value_rows_tail(cols):
    r = lax.broadcasted_iota(jnp.int32, (V_ROWS - HEAD_DIM, cols), 0)
    return jnp.where(r == 0, 1.0, 0.0).astype(BF16)


def _inproj_kernel(x_ref, g_ref, wn_ref, wh_ref, cos_ref, sin_ref, qg_ref, kg_ref,
                   hg_ref, qt_ref, kc_ref, ks_ref, kw_ref, vc_ref, vst_ref, vwt_ref, gt_ref):
    tq = x_ref.shape[0]
    x = x_ref[...]
    ms = jnp.mean(x * x, axis=-1, keepdims=True)
    h = (x * lax.rsqrt(ms + EPS) * g_ref[...]).astype(BF16)
    p = jnp.dot(h, wn_ref[...], preferred_element_type=F32)
    hg_ref[...] = jnp.dot(h, wh_ref[...], preferred_element_type=F32)
    cos = cos_ref[...]
    sin = sin_ref[...]
    lane = lax.broadcasted_iota(jnp.int32, (tq, LANES), 1)
    low_head = lane < HEAD_DIM
    dim = lane & (HEAD_DIM - 1)
    first_half = dim < ROT_DIM // 2

    def norm_rope(xs, gain, scale):
        x2 = xs * xs
        s_lo = jnp.sum(jnp.where(low_head, x2, 0.0), axis=-1, keepdims=True)
        s_hi = jnp.sum(jnp.where(low_head, 0.0, x2), axis=-1, keepdims=True)
        ms = jnp.where(low_head, s_lo, s_hi) * (1.0 / HEAD_DIM)
        y = xs * lax.rsqrt(ms + EPS) * gain
        partner = jnp.where(first_half,
                            pltpu.roll(y, LANES - ROT_DIM // 2, 1),
                            pltpu.roll(y, ROT_DIM // 2, 1))
        return (y * cos + partner * sin) * scale

    def split_groups(ref, x, lanes=slice(None)):
        for g in range(NSA_GROUPS):
            ref[g, :, lanes] = x[:, g * HEAD_DIM:(g + 1) * HEAD_DIM].astype(BF16)

    def split_groups_t(ref, x):
        xt = x.T
        for g in range(NSA_GROUPS):
            ref[g, 0:HEAD_DIM, :] = xt[g * HEAD_DIM:(g + 1) * HEAD_DIM, :].astype(BF16)
            ref[g, HEAD_DIM:V_ROWS, :] = _value_rows_tail(tq)

    qg = qg_ref[...]
    kg = kg_ref[...]
    for s in range(NSA_WIDTH // LANES):
        yt = norm_rope(p[:, s * LANES:(s + 1) * LANES], qg, Q_SCALE).T
        g, h0 = divmod(2 * s, HEADS_PER_GROUP)
        for j in range(2):
            qt_ref[g, :, (h0 + j) * tq:(h0 + j + 1) * tq] = (
                yt[j * HEAD_DIM:(j + 1) * HEAD_DIM, :].astype(BF16))
    base = NSA_WIDTH
    split_groups(kc_ref, norm_rope(p[:, base:base + LANES], kg, 1.0))
    tok = pl.program_id(1) * tq + lax.broadcasted_iota(jnp.int32, (tq, SLC_LANES), 0)
    blk = lax.broadcasted_iota(jnp.int32, (tq, SLC_LANES), 1)
    one_hot = jnp.where((tok >> SLC_SHIFT) == blk, 1.0, 0.0).astype(BF16)
    for g in range(NSA_GROUPS):
        ks_ref[g, :, 0:SLC_LANES] = one_hot
    split_groups(ks_ref, norm_rope(p[:, base + LANES:base + 2 * LANES], kg, 1.0),
                 slice(SLC_LANES, SLC_LANES + HEAD_DIM))
    split_groups(kw_ref, norm_rope(p[:, base + 2 * LANES:base + 3 * LANES], kg, 1.0))
    base += 3 * LANES
    split_groups(vc_ref, p[:, base:base + LANES])
    split_groups_t(vst_ref, p[:, base + LANES:base + 2 * LANES])
    split_groups_t(vwt_ref, p[:, base + 2 * LANES:base + 3 * LANES])
    base += 3 * LANES
    gates_t = jax.nn.sigmoid(p[:, base:base + LANES]).T
    per_group = HEADS_PER_GROUP * 3
    for g in range(NSA_GROUPS):
        gt_ref[g, 0:per_group, :] = gates_t[g * per_group:(g + 1) * per_group, :]
        gt_ref[g, per_group:GATE_ROWS, :] = jnp.zeros((GATE_ROWS - per_group, tq), F32)


def _inproj(x2, gain, w_nsa, w_hg, cos_t, sin_t, qg, kg, batch, seq):
    tq = Q_TILE
    nt = seq // tq
    d = x2.shape[1]
    tok = lambda b, i: (b * nt + i, 0)
    const = lambda b, i: (0, 0)
    rows = lambda b, i: (b, 0, i, 0)
    cols = lambda b, i: (b, 0, 0, i)
    grp = NSA_GROUPS
    return pl.pallas_call(
        _inproj_kernel,
        grid=(batch, nt),
        in_specs=[
            pl.BlockSpec((tq, d), tok),
            pl.BlockSpec((1, d), const),
            pl.BlockSpec(w_nsa.shape, const),
            pl.BlockSpec(w_hg.shape, const),
            pl.BlockSpec((tq, LANES), lambda b, i: (i, 0)),
            pl.BlockSpec((tq, LANES), lambda b, i: (i, 0)),
            pl.BlockSpec((1, LANES), const),
            pl.BlockSpec((1, LANES), const),
        ],
        out_specs=[
            pl.BlockSpec((tq, w_hg.shape[1]), tok),
            pl.BlockSpec((None, grp, None, HEAD_DIM, Q_COLS), lambda b, i: (b, 0, i, 0, 0)),
            pl.BlockSpec((None, grp, tq, HEAD_DIM), rows),
            pl.BlockSpec((None, grp, tq, SLC_LANES + HEAD_DIM), rows),
            pl.BlockSpec((None, grp, tq, HEAD_DIM), rows),
            pl.BlockSpec((None, grp, tq, HEAD_DIM), rows),
            pl.BlockSpec((None, grp, V_ROWS, tq), cols),
            pl.BlockSpec((None, grp, V_ROWS, tq), cols),
            pl.BlockSpec((None, grp, GATE_ROWS, tq), cols),
        ],
        out_shape=[
            jax.ShapeDtypeStruct((batch * seq, w_hg.shape[1]), F32),
            jax.ShapeDtypeStruct((batch, grp, nt, HEAD_DIM, Q_COLS), BF16),
            jax.ShapeDtypeStruct((batch, grp, seq, HEAD_DIM), BF16),
            jax.ShapeDtypeStruct((batch, grp, seq, SLC_LANES + HEAD_DIM), BF16),
            jax.ShapeDtypeStruct((batch, grp, seq, HEAD_DIM), BF16),
            jax.ShapeDtypeStruct((batch, grp, seq, HEAD_DIM), BF16),
            jax.ShapeDtypeStruct((batch, grp, V_ROWS, seq), BF16),
            jax.ShapeDtypeStruct((batch, grp, V_ROWS, seq), BF16),
            jax.ShapeDtypeStruct((batch, grp, GATE_ROWS, seq), F32),
        ],
        compiler_params=_cparams(2),
        name="inproj",
    )(x2, gain, w_nsa, w_hg, cos_t, sin_t, qg, kg)


def _compress_kernel(tk_ref, tv_ref, pek_ref, pev_ref, w1k_ref, w1v_ref,
                     w2k_ref, w2v_ref, ok_ref, ov_ref):
    half = CMP_STRIDE * HEAD_DIM

    def run(t_ref, pe_ref, w1_ref, w2_ref, o_ref, transposed):
        t = t_ref[...]
        n_rows = t.shape[0]
        top = jnp.dot(t, w1_ref[0:half, :], preferred_element_type=F32)
        bot = jnp.dot(t, w1_ref[half:2 * half, :], preferred_element_type=F32)
        bot_next = pltpu.roll(bot, n_rows - 1, 0)
        pe = jnp.dot(pe_ref[...], w1_ref[...], preferred_element_type=F32)[0:1, :]
        h = top + bot_next + pe
        h = h * jax.nn.sigmoid(h)
        o = jnp.dot(h.astype(BF16), w2_ref[...], preferred_element_type=F32)
        r = lax.broadcasted_iota(jnp.int32, o.shape, 0)
        o = jnp.where(r < n_rows - 1, o, 0.0)
        if transposed:
            ot = jnp.concatenate([o, jnp.zeros_like(o)], axis=1).T
            o_ref[0:HEAD_DIM, :] = ot[0:HEAD_DIM, :].astype(BF16)
            o_ref[HEAD_DIM:V_ROWS, :] = _value_rows_tail(n_rows)
        else:
            o_ref[...] = o.astype(BF16)

    run(tk_ref, pek_ref, w1k_ref, w2k_ref, ok_ref, False)
    run(tv_ref, pev_ref, w1v_ref, w2v_ref, ov_ref, True)


def _compress(tk, tv, pek, pev, w1k, w1v, w2k, w2v):
    bg, n_rows, width = tk.shape
    blk = lambda i: (i, 0, 0)
    const = lambda i: (0, 0)
    return pl.pallas_call(
        _compress_kernel,
        grid=(bg,),
        in_specs=[
            pl.BlockSpec((None, n_rows, width), blk),
            pl.BlockSpec((None, n_rows, width), blk),
            pl.BlockSpec(pek.shape, const),
            pl.BlockSpec(pev.shape, const),
            pl.BlockSpec(w1k.shape, const),
            pl.BlockSpec(w1v.shape, const),
            pl.BlockSpec(w2k.shape, const),
            pl.BlockSpec(w2v.shape, const),
        ],
        out_specs=[pl.BlockSpec((None, n_rows, HEAD_DIM), blk),
                   pl.BlockSpec((None, V_ROWS, n_rows), blk)],
        out_shape=[jax.ShapeDtypeStruct((bg, n_rows, HEAD_DIM), BF16),
                   jax.ShapeDtypeStruct((bg, V_ROWS, n_rows), BF16)],
        compiler_params=_cparams(1),
        name="compress",
    )(tk, tv, pek, pev, w1k, w1v, w2k, w2v)


def _nsa_attn_kernel(qt_ref, gt_ref, kc_ref, vct_ref, mapt_ref, ks_ref, vst_ref,
                     kw_ref, vwt_ref, y_ref, qaug_ref, s0_ref, s1_ref, c0_ref, c1_ref,
                     m_ref, acc_ref, ocmp_ref, owin_ref):
    i = pl.program_id(2)
    q0 = i * Q_TILE
    qt = qt_ref[...]
    col = lax.broadcasted_iota(jnp.int32, (1, Q_COLS), 1)
    tpos = q0 + (col & (Q_TILE - 1))

    def cmp_and_select(n_rows, n_blk):
        s = jnp.dot(kc_ref[0:n_rows, :], qt, preferred_element_type=F32)
        cmp_end = lax.broadcasted_iota(jnp.int32, (n_rows, 1), 0) * CMP_STRIDE + (CMP_LEN - 1)
        s = jnp.where(cmp_end <= tpos, s, NEG)
        m = jnp.max(s, axis=0, keepdims=True)
        p = jnp.exp2(s - jnp.maximum(m, 0.1 * NEG)).astype(BF16)
        acc = jnp.dot(vct_ref[:, 0:n_rows], p, preferred_element_type=F32)
        l = acc[HEAD_DIM:HEAD_DIM + 1, :]
        inv = jnp.where(l > 0.0, 1.0 / l, 0.0)
        ocmp_ref[...] = acc[0:HEAD_DIM, :] * inv

        mapt = mapt_ref[0:n_blk, 0:n_rows]
        imp = None
        for h in range(HEADS_PER_GROUP):
            cols = slice(h * Q_TILE, (h + 1) * Q_TILE)
            part = jnp.dot(mapt, p[:, cols], preferred_element_type=F32) * inv[:, cols]
            imp = part if imp is None else imp + part

        blk_i = lax.broadcasted_iota(jnp.int32, (n_blk, Q_TILE), 0)
        cur = (q0 + lax.broadcasted_iota(jnp.int32, (n_blk, Q_TILE), 1)) >> SLC_SHIFT
        blk = blk_i.astype(F32)
        val = jnp.where(blk_i <= cur, imp, -1.0)
        for forced in (blk_i == 0, blk_i == cur, blk_i == cur - 1):
            val = jnp.where(forced, 1e30, val)
        sel = jnp.zeros((n_blk, Q_TILE), F32)
        for _ in range(min(SLC_TOPN, n_blk)):
            mx = jnp.max(val, axis=0, keepdims=True)
            first = jnp.min(jnp.where(val == mx, blk, float(n_blk)), axis=0, keepdims=True)
            first = jnp.where(mx > -0.5, first, -1.0)
            pick = blk == first
            sel = jnp.where(pick, 1.0, sel)
            val = jnp.where(pick, -1.0, val)
        sel_bias = jnp.where(sel > 0.5, 0.0, -BIG).astype(BF16)
        for h in range(HEADS_PER_GROUP):
            qaug_ref[0:n_blk, h * Q_TILE:(h + 1) * Q_TILE] = sel_bias
        if n_blk < SLC_LANES:
            qaug_ref[n_blk:SLC_LANES, :] = jnp.full((SLC_LANES - n_blk, Q_COLS), -BIG, BF16)

    col_tiles = [slice(c0, c0 + MXU_COLS) for c0 in range(0, Q_COLS, MXU_COLS)]
    buffers = ((s0_ref, c0_ref), (s1_ref, c1_ref))

    def slc_scores(k0, s_ref, cmax_ref, cols):
        sc = jnp.dot(ks_ref[pl.ds(k0, SLC_KEYS), :], qaug_ref[:, cols],
                     preferred_element_type=F32)
        s_ref[:, cols] = sc
        cmax_ref[:, cols] = jnp.max(sc, axis=0, keepdims=True)

    span = WINDOW + Q_TILE
    t_loc = col & (Q_TILE - 1)

    def window_scores(interior):
        if not interior:
            sw = jnp.dot(kw_ref[0:span, :], qt, preferred_element_type=F32)
            kpos = lax.broadcasted_iota(jnp.int32, (span, 1), 0)
            sw = jnp.where(kpos <= tpos, jnp.where(kpos > tpos - WINDOW, sw, NEG), NEG)
            return [sw], 0
        start = pl.multiple_of(q0 - WINDOW, Q_TILE)
        i_loc = lax.broadcasted_iota(jnp.int32, (Q_TILE, 1), 0)
        n_parts = span // Q_TILE
        parts = []
        for r in range(n_parts):
            sw = jnp.dot(kw_ref[pl.ds(start + r * Q_TILE, Q_TILE), :], qt,
                         preferred_element_type=F32)
            if r == 0:
                sw = jnp.where(i_loc > t_loc, sw, NEG)
            elif r == n_parts - 1:
                sw = jnp.where(i_loc <= t_loc, sw, NEG)
            parts.append(sw)
        return parts, start

    def window_finish(parts, v_start):
        mw = None
        for sw in parts:
            cm = jnp.max(sw, axis=0, keepdims=True)
            mw = cm if mw is None else jnp.maximum(mw, cm)
        accw, off = None, 0
        for sw in parts:
            pw = jnp.exp2(sw - mw).astype(BF16)
            d = jnp.dot(vwt_ref[:, pl.ds(v_start + off, sw.shape[0])], pw,
                        preferred_element_type=F32)
            accw = d if accw is None else accw + d
            off += sw.shape[0]
        owin_ref[...] = accw[0:HEAD_DIM, :] / accw[HEAD_DIM:HEAD_DIM + 1, :]

    def front(n_rows, n_blk, interior):
        parts, v_start = window_scores(interior)
        cmp_and_select(n_rows, n_blk)
        qaug_ref[SLC_LANES:SLC_LANES + HEAD_DIM, :] = qt
        for cols in col_tiles:
            slc_scores(0, *buffers[0], cols)
        window_finish(parts, v_start)

    n_cmp_rows = kc_ref.shape[0]
    row_steps = list(range(LANES, n_cmp_rows, LANES)) + [n_cmp_rows]
    q_end = q0 + Q_TILE
    lo = 0
    for n_rows in row_steps:
        hi = n_rows * CMP_STRIDE
        n_blk = min(SLC_LANES, -(-(hi // SLC_BLOCK) // 16) * 16)
        cuts = [lo, WINDOW, hi] if lo < WINDOW < hi else [lo, hi]
        for c_lo, c_hi in zip(cuts[:-1], cuts[1:]):
            pl.when(jnp.logical_and(q_end > c_lo, q_end <= c_hi))(
                functools.partial(front, n_rows, n_blk, c_lo >= WINDOW))
        lo = hi

    def slc_update(k0, s_ref, cmax_ref, cols, masked):
        sc = s_ref[:, cols]
        if masked:
            kpos = k0 + lax.broadcasted_iota(jnp.int32, (SLC_KEYS, 1), 0)
            sc = jnp.where(kpos <= tpos[:, cols], sc, -BIG)
            cmax = jnp.max(sc, axis=0, keepdims=True)
        else:
            cmax = cmax_ref[:, cols]
        m_prev = m_ref[:, cols]
        m_new = jnp.maximum(m_prev, cmax)
        pr = jnp.exp2(sc - m_new).astype(BF16)
        pv = jnp.dot(vst_ref[:, pl.ds(k0, SLC_KEYS)], pr, preferred_element_type=F32)
        acc_ref[:, cols] = acc_ref[:, cols] * jnp.exp2(m_prev - m_new) + pv
        m_ref[:, cols] = m_new

    def slc_body(kt, carry):
        k0 = pl.multiple_of(kt * SLC_KEYS, SLC_KEYS)
        for parity in range(2):
            @pl.when((kt & 1) == parity)
            def _():
                for cols in col_tiles:
                    slc_scores(k0 + SLC_KEYS, *buffers[1 - parity], cols)
                    slc_update(k0, *buffers[parity], cols, masked=False)
        return carry

    n_full = q0 // SLC_KEYS
    m_ref[...] = jnp.full(m_ref.shape, NEG, F32)
    acc_ref[...] = jnp.zeros(acc_ref.shape, F32)
    lax.fori_loop(0, n_full, slc_body, 0)
    k_last = pl.multiple_of(n_full * SLC_KEYS, SLC_KEYS)
    for parity in range(2):
        @pl.when((n_full & 1) == parity)
        def _():
            for cols in col_tiles:
                slc_update(k_last, *buffers[parity], cols, masked=True)
    o_slc = acc_ref[0:HEAD_DIM, :] / acc_ref[HEAD_DIM:HEAD_DIM + 1, :]

    g = gt_ref[...]

    def gate_row(branch):
        return jnp.concatenate(
            [g[h * 3 + branch:h * 3 + branch + 1, :] for h in range(HEADS_PER_GROUP)], axis=1)

    y = (gate_row(0) * ocmp_ref[...] + gate_row(1) * o_slc
         + gate_row(2) * owin_ref[...])
    for pair in range(HEADS_PER_GROUP // 2):
        lo, mid, hi = (2 * pair) * Q_TILE, (2 * pair + 1) * Q_TILE, (2 * pair + 2) * Q_TILE
        stacked = jnp.concatenate([y[:, lo:mid], y[:, mid:hi]], axis=0)
        y_ref[:, pair * LANES:(pair + 1) * LANES] = stacked.T


def _nsa_attn(qt, gt, kc, vct, mapt, ks, vst, kw, vwt):
    b, g, nt = qt.shape[:3]
    seq = ks.shape[2]
    n_cmp_rows = kc.shape[2]
    per_bg = lambda bi, gi, i: (bi, gi, 0, 0)
    return pl.pallas_call(
        _nsa_attn_kernel,
        grid=(b, g, nt),
        in_specs=[
            pl.BlockSpec((None, None, None, HEAD_DIM, Q_COLS), lambda bi, gi, i: (bi, gi, i, 0, 0)),
            pl.BlockSpec((None, None, GATE_ROWS, Q_TILE), lambda bi, gi, i: (bi, gi, 0, i)),
            pl.BlockSpec((None, None, n_cmp_rows, HEAD_DIM), per_bg),
            pl.BlockSpec((None, None, V_ROWS, n_cmp_rows), per_bg),
            pl.BlockSpec(mapt.shape, lambda bi, gi, i: (0, 0)),
            pl.BlockSpec((None, None, seq, HEAD_DIM + SLC_LANES), per_bg),
            pl.BlockSpec((None, None, V_ROWS, seq), per_bg),
            pl.BlockSpec((None, None, seq, HEAD_DIM), per_bg),
            pl.BlockSpec((None, None, V_ROWS, seq), per_bg),
        ],
        out_specs=pl.BlockSpec((Q_TILE, HEADS_PER_GROUP * HEAD_DIM),
                               lambda bi, gi, i: (bi * nt + i, gi)),
        out_shape=jax.ShapeDtypeStruct((b * seq, NSA_WIDTH), F32),
        scratch_shapes=[
            pltpu.VMEM((SLC_LANES + HEAD_DIM, Q_COLS), BF16),
            pltpu.VMEM((SLC_KEYS, Q_COLS), F32),
            pltpu.VMEM((SLC_KEYS, Q_COLS), F32),
            pltpu.VMEM((1, Q_COLS), F32),
            pltpu.VMEM((1, Q_COLS), F32),
            pltpu.VMEM((1, Q_COLS), F32),
            pltpu.VMEM((V_ROWS, Q_COLS), F32),
            pltpu.VMEM((HEAD_DIM, Q_COLS), F32),
            pltpu.VMEM((HEAD_DIM, Q_COLS), F32),
        ],
        compiler_params=_cparams(3),
        name="nsa_attn",
    )(qt, gt, kc, vct, mapt, ks, vst, kw, vwt)


SCAN_MXU_SIZES = (2, 4, 8)


def _scan_matrices(rows):
    t = np.arange(rows)[:, None]
    s = np.arange(rows)[None, :]
    mats = [(s // h == t // h) & (s <= t) for h in SCAN_MXU_SIZES]
    mats += [(s // h == t // h) & (s >= t) for h in SCAN_MXU_SIZES]
    return jnp.asarray(np.concatenate([np.tile(m, (1, 3)) for m in mats], axis=0), BF16)


def _segmented_scans(g, mats_ref):
    rows = g.shape[0]
    hi = g.astype(BF16)
    rest = g - hi.astype(F32)
    mid = rest.astype(BF16)
    lo = (rest - mid.astype(F32)).astype(BF16)
    small = jnp.dot(mats_ref[...], jnp.concatenate([hi, mid, lo], axis=0),
                    preferred_element_type=F32)
    n = len(SCAN_MXU_SIZES)
    out = {1: (g, g)}
    for j, h in enumerate(SCAN_MXU_SIZES):
        out[h] = (small[j * rows:(j + 1) * rows], small[(n + j) * rows:(n + j + 1) * rows])
    h = SCAN_MXU_SIZES[-1]
    pre, suf = out[h]
    while h < rows:
        pre_parts, suf_parts = [], []
        for b in range(rows // (2 * h)):
            lo_r, mid_r, hi_r = b * 2 * h, b * 2 * h + h, (b + 1) * 2 * h
            pre_parts += [pre[lo_r:mid_r], pre[mid_r:hi_r] + pre[mid_r - 1:mid_r, :]]
            suf_parts += [suf[lo_r:mid_r] + suf[mid_r:mid_r + 1, :], suf[mid_r:hi_r]]
        pre = jnp.concatenate(pre_parts, axis=0)
        suf = jnp.concatenate(suf_parts, axis=0)
        h *= 2
        out[h] = (pre, suf)
    return out


def _hgrn_kernel(hg_ref, lb_ref, gain_ref, mats_ref, o_ref, state_ref):
    c = pl.program_id(1)

    @pl.when(c == 0)
    def _():
        state_ref[...] = jnp.zeros_like(state_ref)

    rows = HG_CHUNK
    t_idx = lax.broadcasted_iota(jnp.int32, (rows, rows), 0)
    s_idx = lax.broadcasted_iota(jnp.int32, (rows, rows), 1)
    diff = jnp.where(t_idx > s_idx, t_idx ^ s_idx, 0)
    nt_dims = (((1,), (1,)), ((), ()))
    lb = lb_ref[...]
    gain = gain_ref[...]

    def heads(x):
        return [x[:, h * HG_DIM:(h + 1) * HG_DIM] for h in range(HG_HEADS)]

    def gates_and_scans(tok):
        w = HG_WIDTH
        q = hg_ref[tok, 0:w]
        f = lb + (1.0 - lb) * jax.nn.sigmoid(hg_ref[tok, w:2 * w])
        logf = jnp.log(f)
        return dict(tok=tok, qf=q * jax.nn.sigmoid(q), k=1.0 - f, logf=logf,
                    scans=_segmented_scans(logf, mats_ref))

    def intra_chunk(c):
        qf, k, logf, scans = c["qf"], c["k"], c["logf"], c["scans"]
        qb, kb = heads(qf.astype(BF16)), heads(k.astype(BF16))
        a = [jnp.where(t_idx == s_idx,
                       lax.dot_general(qb[h], kb[h], nt_dims, preferred_element_type=F32), 0.0)
             for h in range(HG_HEADS)]
        level, h_size = 0, 1
        while h_size < rows:
            pre, suf = scans[h_size]
            ql = heads((qf * jnp.exp(pre)).astype(BF16))
            kl = heads((k * jnp.exp(suf - logf)).astype(BF16))
            mask = (diff >> level) == 1
            for h in range(HG_HEADS):
                al = lax.dot_general(ql[h], kl[h], nt_dims, preferred_element_type=F32)
                a[h] = jnp.where(mask, al, a[h])
            level += 1
            h_size *= 2
        cum, rev = scans[rows]
        c.update(a=a, cum=cum,
                 q_in=heads((qf * jnp.exp(cum)).astype(BF16)),
                 k_out=heads((k * jnp.exp(rev - logf)).astype(BF16)))
        return c

    def outputs_and_state(c, states):
        tok, w = c["tok"], HG_WIDTH
        v = hg_ref[tok, 2 * w:3 * w]
        og = hg_ref[tok, 3 * w:4 * w]
        vh = heads(v.astype(BF16))
        new_states = []
        for h in range(HG_HEADS):
            lanes = slice(h * HG_DIM, (h + 1) * HG_DIM)
            state_t = states[h]
            o = jnp.dot(c["a"][h].astype(BF16), vh[h], preferred_element_type=F32)
            o = o + lax.dot_general(c["q_in"][h], state_t.astype(BF16), nt_dims,
                                    preferred_element_type=F32)
            decay = jnp.exp(c["cum"][rows - 1:rows, lanes])
            v_t = v[:, lanes].T.astype(BF16)
            new_states.append(
                state_t * decay + jnp.dot(v_t, c["k_out"][h], preferred_element_type=F32))
            ms = jnp.mean(o * o, axis=-1, keepdims=True)
            gate = og[:, lanes]
            o_ref[tok, lanes] = (o * lax.rsqrt(ms + EPS) * gain[:, lanes]
                                 * (gate * jax.nn.sigmoid(gate)))
        return new_states

    toks = [slice(j * rows, (j + 1) * rows) for j in range(hg_ref.shape[0] // rows)]
    chunks = [gates_and_scans(tok) for tok in toks]
    chunks = [intra_chunk(c) for c in chunks]
    states = [state_ref[h] for h in range(HG_HEADS)]
    for c in chunks:
        states = outputs_and_state(c, states)
    for h in range(HG_HEADS):
        state_ref[h] = states[h]


def _hgrn(hg_proj, lb, gain, batch, seq):
    n = hg_proj.shape[0]
    n_chunk = seq // HG_STEP
    mats = _scan_matrices(HG_CHUNK)
    row = lambda b, c: (b * n_chunk + c, 0)
    const = lambda b, c: (0, 0)
    return pl.pallas_call(
        _hgrn_kernel,
        grid=(batch, n_chunk),
        in_specs=[
            pl.BlockSpec((HG_STEP, 4 * HG_WIDTH), row),
            pl.BlockSpec((1, HG_WIDTH), const),
            pl.BlockSpec((1, HG_WIDTH), const),
            pl.BlockSpec(mats.shape, const),
        ],
        out_specs=pl.BlockSpec((HG_STEP, HG_WIDTH), row),
        out_shape=jax.ShapeDtypeStruct((n, HG_WIDTH), F32),
        scratch_shapes=[pltpu.VMEM((HG_HEADS, HG_DIM, HG_DIM), F32)],
        compiler_params=_cparams(2),
        name="hgrn",
    )(hg_proj, lb, gain, mats)


def _out_mlp_kernel(x_ref, yn_ref, yh_ref, gn_ref, gm_ref, wa_ref, wb_ref,
                    w1_ref, w2_ref, o_ref, *, ff_chunk):
    yn = yn_ref[...]
    ms = jnp.mean(yn * yn, axis=-1, keepdims=True)
    yn = (yn * lax.rsqrt(ms + EPS) * gn_ref[...]).astype(BF16)
    x1 = (x_ref[...]
          + jnp.dot(yn, wa_ref[...], preferred_element_type=F32)
          + jnp.dot(yh_ref[...].astype(BF16), wb_ref[...], preferred_element_type=F32))
    ms = jnp.mean(x1 * x1, axis=-1, keepdims=True)
    h = (x1 * lax.rsqrt(ms + EPS) * gm_ref[...]).astype(BF16)
    mlp = None
    for c in range(w1_ref.shape[1] // ff_chunk):
        cols = slice(c * ff_chunk, (c + 1) * ff_chunk)
        u = jnp.maximum(jnp.dot(h, w1_ref[:, cols], preferred_element_type=F32), 0.0)
        d = jnp.dot((u * u).astype(BF16), w2_ref[cols, :], preferred_element_type=F32)
        mlp = d if mlp is None else mlp + d
    o_ref[...] = x1 + mlp


def _out_mlp(x2, y_nsa, y_hg, gn, gm, wa, wb, w1, w2, tm=256, ff_chunk=1024):
    n, d = x2.shape
    row = lambda i: (i, 0)
    const = lambda i: (0, 0)

    def resident(shape):
        return pl.BlockSpec(shape, const)

    return pl.pallas_call(
        functools.partial(_out_mlp_kernel, ff_chunk=ff_chunk),
        grid=(n // tm,),
        in_specs=[
            pl.BlockSpec((tm, d), row),
            pl.BlockSpec((tm, y_nsa.shape[1]), row),
            pl.BlockSpec((tm, y_hg.shape[1]), row),
            pl.BlockSpec((1, y_nsa.shape[1]), const),
            pl.BlockSpec((1, d), const),
            resident(wa.shape),
            resident(wb.shape),
            resident(w1.shape),
            resident(w2.shape),
        ],
        out_specs=pl.BlockSpec((tm, d), row),
        out_shape=jax.ShapeDtypeStruct((n, d), F32),
        compiler_params=_cparams(1),
        name="out_mlp",
    )(x2, y_nsa, y_hg, gn, gm, wa, wb, w1, w2)


def _rope_tables(seq):
    pos = jnp.arange(seq, dtype=F32)
    inv_freq = ROPE_THETA ** (-jnp.arange(0, ROT_DIM, 2, dtype=F32) / ROT_DIM)
    ang = pos[:, None] * inv_freq[None, :]
    cos, sin = jnp.cos(ang), jnp.sin(ang)
    rest = HEAD_DIM - ROT_DIM
    cos_h = jnp.concatenate([cos, cos, jnp.ones((seq, rest), F32)], axis=-1)
    sin_h = jnp.concatenate([-sin, sin, jnp.zeros((seq, rest), F32)], axis=-1)
    return jnp.tile(cos_h, (1, 2)), jnp.tile(sin_h, (1, 2))


def _cmp_to_slc_t(n_cmp_rows, n_cmp, n_slc):
    tok = np.arange(n_cmp)[:, None] * CMP_STRIDE + np.arange(CMP_LEN)[None, :]
    frac = (tok[:, :, None] // SLC_BLOCK == np.arange(n_slc)[None, None, :]).mean(axis=1)
    out = np.zeros((SLC_LANES, n_cmp_rows), np.float32)
    out[:n_slc, :n_cmp] = frac.T
    return jnp.asarray(out, BF16)


def kernel(x, attn_norm, w_in, q_norm, k_norm, cmp_pe_k, cmp_w1_k, cmp_w2_k, cmp_pe_v,
           cmp_w1_v, cmp_w2_v, nsa_out_norm, hgrn_lb, hgrn_out_norm, w_out, mlp_norm,
           w_mlp_in, w_mlp_out):
    batch, seq, d_model = x.shape
    n = batch * seq
    n_slc = seq // SLC_BLOCK
    n_cmp_rows = seq // CMP_STRIDE
    n_cmp = (seq - CMP_LEN) // CMP_STRIDE + 1
    assert n_slc <= SLC_LANES and seq % SLC_KEYS == 0 and seq >= WINDOW + Q_TILE
    assert w_in.shape[0] == 1, "single-layer block"

    kvw = KV_WIDTH
    o_q, o_kc, o_vc, o_ks, o_vs, o_kw, o_vw, o_gate = np.cumsum(
        [0, NSA_WIDTH, kvw, kvw, kvw, kvw, kvw, kvw])
    o_hg = o_gate + NSA_HEADS * 3
    w = w_in[0]
    pad = jnp.zeros((d_model, NSA_COLS - (NSA_WIDTH + 6 * kvw + NSA_HEADS * 3)), w.dtype)
    w_nsa = jnp.concatenate(
        [w[:, o_q:o_kc], w[:, o_kc:o_vc], w[:, o_ks:o_vs], w[:, o_kw:o_vw],
         w[:, o_vc:o_ks], w[:, o_vs:o_kw], w[:, o_vw:o_gate], w[:, o_gate:o_hg], pad],
        axis=1).astype(BF16)
    w_hg = w[:, o_hg:].astype(BF16)
    x2 = x.reshape(n, d_model)
    cos_t, sin_t = _rope_tables(seq)
    qg = jnp.tile(q_norm[0], 2)[None, :]
    kg = jnp.tile(k_norm[0], 2)[None, :]
    hg_proj, qt, kc, ks_aug, kw, vc, vs_t, vw_t, gt = _inproj(
        x2, attn_norm[0][None, :], w_nsa, w_hg, cos_t, sin_t, qg, kg, batch, seq)


    blocks = lambda t: t.reshape(batch * NSA_GROUPS, n_cmp_rows, CMP_STRIDE * HEAD_DIM)
    pe8 = lambda pe: jnp.broadcast_to(pe.reshape(1, CMP_LEN * HEAD_DIM), (8, CMP_LEN * HEAD_DIM)).astype(BF16)
    k_cmp, v_cmp_t = _compress(
        blocks(kc), blocks(vc), pe8(cmp_pe_k[0]), pe8(cmp_pe_v[0]),
        cmp_w1_k[0].astype(BF16), cmp_w1_v[0].astype(BF16),
        cmp_w2_k[0].astype(BF16), cmp_w2_v[0].astype(BF16))
    k_cmp = k_cmp.reshape(batch, NSA_GROUPS, n_cmp_rows, HEAD_DIM)
    v_cmp_t = v_cmp_t.reshape(batch, NSA_GROUPS, V_ROWS, n_cmp_rows)
    y_nsa = _nsa_attn(qt, gt, k_cmp, v_cmp_t, _cmp_to_slc_t(n_cmp_rows, n_cmp, n_slc),
                      ks_aug, vs_t, kw, vw_t)

    lb_all = jnp.cumsum(jax.nn.softmax(hgrn_lb.astype(F32), axis=0), axis=0)
    y_hg = _hgrn(hg_proj, lb_all[0][None, :], jnp.tile(hgrn_out_norm[0], HG_HEADS)[None, :],
                 batch, seq)

    wo = w_out[0].astype(BF16)
    out = _out_mlp(x2, y_nsa, y_hg, nsa_out_norm[0][None, :], mlp_norm[0][None, :],
                   wo[:NSA_WIDTH], wo[NSA_WIDTH:], w_mlp_in[0].astype(BF16),
                   w_mlp_out[0].astype(BF16))
    return out.reshape(batch, seq, d_model)
```

```python
import functools

import jax
import jax.numpy as jnp
import numpy as np
from jax import lax
from jax.experimental import pallas as pl
from jax.experimental.pallas import tpu as pltpu

F32 = jnp.float32
BF16 = jnp.bfloat16

NSA_HEADS = 8
NSA_GROUPS = 2
HEADS_PER_GROUP = NSA_HEADS // NSA_GROUPS
HEAD_DIM = 64
NSA_WIDTH = NSA_HEADS * HEAD_DIM
KV_WIDTH = NSA_GROUPS * HEAD_DIM
CMP_LEN = 32
CMP_STRIDE = 16
CMP_HIDDEN = 2 * HEAD_DIM
SLC_BLOCK = 64
SLC_SHIFT = 6
SLC_TOPN = 16
WINDOW = 512
ROT_DIM = HEAD_DIM // 4
ROPE_THETA = 500000.0
HG_HEADS = 4
HG_DIM = 128
HG_WIDTH = HG_HEADS * HG_DIM
EPS = 1e-6

LANES = 128
Q_TILE = 256
Q_COLS = HEADS_PER_GROUP * Q_TILE
SLC_KEYS = 512
SLC_LANES = 128
MXU_COLS = 256
V_ROWS = 80
GATE_ROWS = 16
NSA_COLS = 1408
NEG = -1e30
Q_SCALE = HEAD_DIM ** -0.5 * 1.4426950408889634
BIG = 2.0 ** 100
HG_CHUNK = 128
HG_STEP = 512
VMEM_LIMIT = 56 * 1024 * 1024


def _cparams(n_grid):
    return pltpu.CompilerParams(
        dimension_semantics=("arbitrary",) * n_grid, vmem_limit_bytes=VMEM_LIMIT)


def _value_rows_tail(cols):
    r = lax.broadcasted_iota(jnp.int32, (V_ROWS - HEAD_DIM, cols), 0)
    return jnp.where(r == 0, 1.0, 0.0).astype(BF16)


def _inproj_kernel(x_ref, g_ref, wn_ref, wh_ref, cos_ref, sin_ref, qg_ref, kg_ref,
                   hg_ref, qt_ref, kc_ref, ks_ref, kw_ref, vc_ref, vst_ref, vwt_ref, gt_ref):
    tq = x_ref.shape[0]
    x = x_ref[...]
    ms = jnp.mean(x * x, axis=-1, keepdims=True)
    h = (x * lax.rsqrt(ms + EPS) * g_ref[...]).astype(BF16)
    p = jnp.dot(h, wn_ref[...], preferred_element_type=F32)
    hg_ref[...] = jnp.dot(h, wh_ref[...], preferred_element_type=F32)
    cos = cos_ref[...]
    sin = sin_ref[...]
    lane = lax.broadcasted_iota(jnp.int32, (tq, LANES), 1)
    low_head = lane < HEAD_DIM
    dim = lane & (HEAD_DIM - 1)
    first_half = dim < ROT_DIM // 2

    def norm_rope(xs, gain, scale):
        x2 = xs * xs
        s_lo = jnp.sum(jnp.where(low_head, x2, 0.0), axis=-1, keepdims=True)
        s_hi = jnp.sum(jnp.where(low_head, 0.0, x2), axis=-1, keepdims=True)
        ms = jnp.where(low_head, s_lo, s_hi) * (1.0 / HEAD_DIM)
        y = xs * lax.rsqrt(ms + EPS) * gain
        partner = jnp.where(first_half,
                            pltpu.roll(y, LANES - ROT_DIM // 2, 1),
                            pltpu.roll(y, ROT_DIM // 2, 1))
        return (y * cos + partner * sin) * scale

    def split_groups(ref, x, lanes=slice(None)):
        for g in range(NSA_GROUPS):
            ref[g, :, lanes] = x[:, g * HEAD_DIM:(g + 1) * HEAD_DIM].astype(BF16)

    def split_groups_t(ref, x):
        xt = x.T
        for g in range(NSA_GROUPS):
            ref[g, 0:HEAD_DIM, :] = xt[g * HEAD_DIM:(g + 1) * HEAD_DIM, :].astype(BF16)
            ref[g, HEAD_DIM:V_ROWS, :] = _value_rows_tail(tq)

    qg = qg_ref[...]
    kg = kg_ref[...]
    for s in range(NSA_WIDTH // LANES):
        yt = norm_rope(p[:, s * LANES:(s + 1) * LANES], qg, Q_SCALE).T
        g, h0 = divmod(2 * s, HEADS_PER_GROUP)
        for j in range(2):
            qt_ref[g, :, (h0 + j) * tq:(h0 + j + 1) * tq] = (
                yt[j * HEAD_DIM:(j + 1) * HEAD_DIM, :].astype(BF16))
    base = NSA_WIDTH
    split_groups(kc_ref, norm_rope(p[:, base:base + LANES], kg, 1.0))
    tok = pl.program_id(1) * tq + lax.broadcasted_iota(jnp.int32, (tq, SLC_LANES), 0)
    blk = lax.broadcasted_iota(jnp.int32, (tq, SLC_LANES), 1)
    one_hot = jnp.where((tok >> SLC_SHIFT) == blk, 1.0, 0.0).astype(BF16)
    for g in range(NSA_GROUPS):
        ks_ref[g, :, 0:SLC_LANES] = one_hot
    split_groups(ks_ref, norm_rope(p[:, base + LANES:base + 2 * LANES], kg, 1.0),
                 slice(SLC_LANES, SLC_LANES + HEAD_DIM))
    split_groups(kw_ref, norm_rope(p[:, base + 2 * LANES:base + 3 * LANES], kg, 1.0))
    base += 3 * LANES
    split_groups(vc_ref, p[:, base:base + LANES])
    split_groups_t(vst_ref, p[:, base + LANES:base + 2 * LANES])
    split_groups_t(vwt_ref, p[:, base + 2 * LANES:base + 3 * LANES])
    base += 3 * LANES
    gates_t = jax.nn.sigmoid(p[:, base:base + LANES]).T
    per_group = HEADS_PER_GROUP * 3
    for g in range(NSA_GROUPS):
        gt_ref[g, 0:per_group, :] = gates_t[g * per_group:(g + 1) * per_group, :]
        gt_ref[g, per_group:GATE_ROWS, :] = jnp.zeros((GATE_ROWS - per_group, tq), F32)


def _inproj(x2, gain, w_nsa, w_hg, cos_t, sin_t, qg, kg, batch, seq):
    tq = Q_TILE
    nt = seq // tq
    d = x2.shape[1]
    tok = lambda b, i: (b * nt + i, 0)
    const = lambda b, i: (0, 0)
    rows = lambda b, i: (b, 0, i, 0)
    cols = lambda b, i: (b, 0, 0, i)
    grp = NSA_GROUPS
    return pl.pallas_call(
        _inproj_kernel,
        grid=(batch, nt),
        in_specs=[
            pl.BlockSpec((tq, d), tok),
            pl.BlockSpec((1, d), const),
            pl.BlockSpec(w_nsa.shape, const),
            pl.BlockSpec(w_hg.shape, const),
            pl.BlockSpec((tq, LANES), lambda b, i: (i, 0)),
            pl.BlockSpec((tq, LANES), lambda b, i: (i, 0)),
            pl.BlockSpec((1, LANES), const),
            pl.BlockSpec((1, LANES), const),
        ],
        out_specs=[
            pl.BlockSpec((tq, w_hg.shape[1]), tok),
            pl.BlockSpec((None, grp, None, HEAD_DIM, Q_COLS), lambda b, i: (b, 0, i, 0, 0)),
            pl.BlockSpec((None, grp, tq, HEAD_DIM), rows),
            pl.BlockSpec((None, grp, tq, SLC_LANES + HEAD_DIM), rows),
            pl.BlockSpec((None, grp, tq, HEAD_DIM), rows),
            pl.BlockSpec((None, grp, tq, HEAD_DIM), rows),
            pl.BlockSpec((None, grp, V_ROWS, tq), cols),
            pl.BlockSpec((None, grp, V_ROWS, tq), cols),
            pl.BlockSpec((None, grp, GATE_ROWS, tq), cols),
        ],
        out_shape=[
            jax.ShapeDtypeStruct((batch * seq, w_hg.shape[1]), F32),
            jax.ShapeDtypeStruct((batch, grp, nt, HEAD_DIM, Q_COLS), BF16),
            jax.ShapeDtypeStruct((batch, grp, seq, HEAD_DIM), BF16),
            jax.ShapeDtypeStruct((batch, grp, seq, SLC_LANES + HEAD_DIM), BF16),
            jax.ShapeDtypeStruct((batch, grp, seq, HEAD_DIM), BF16),
            jax.ShapeDtypeStruct((batch, grp, seq, HEAD_DIM), BF16),
            jax.ShapeDtypeStruct((batch, grp, V_ROWS, seq), BF16),
            jax.ShapeDtypeStruct((batch, grp, V_ROWS, seq), BF16),
            jax.ShapeDtypeStruct((batch, grp, GATE_ROWS, seq), F32),
        ],
        compiler_params=_cparams(2),
        name="inproj",
    )(x2, gain, w_nsa, w_hg, cos_t, sin_t, qg, kg)


def _compress_kernel(tk_ref, tv_ref, pek_ref, pev_ref, w1k_ref, w1v_ref,
                     w2k_ref, w2v_ref, ok_ref, ov_ref):
    half = CMP_STRIDE * HEAD_DIM

    def run(t_ref, pe_ref, w1_ref, w2_ref, o_ref, transposed):
        t = t_ref[...]
        n_rows = t.shape[0]
        top = jnp.dot(t, w1_ref[0:half, :], preferred_element_type=F32)
        bot = jnp.dot(t, w1_ref[half:2 * half, :], preferred_element_type=F32)
        bot_next = pltpu.roll(bot, n_rows - 1, 0)
        pe = jnp.dot(pe_ref[...], w1_ref[...], preferred_element_type=F32)[0:1, :]
        h = top + bot_next + pe
        h = h * jax.nn.sigmoid(h)
        o = jnp.dot(h.astype(BF16), w2_ref[...], preferred_element_type=F32)
        r = lax.broadcasted_iota(jnp.int32, o.shape, 0)
        o = jnp.where(r < n_rows - 1, o, 0.0)
        if transposed:
            ot = jnp.concatenate([o, jnp.zeros_like(o)], axis=1).T
            o_ref[0:HEAD_DIM, :] = ot[0:HEAD_DIM, :].astype(BF16)
            o_ref[HEAD_DIM:V_ROWS, :] = _value_rows_tail(n_rows)
        else:
            o_ref[...] = o.astype(BF16)

    run(tk_ref, pek_ref, w1k_ref, w2k_ref, ok_ref, False)
    run(tv_ref, pev_ref, w1v_ref, w2v_ref, ov_ref, True)


def _compress(tk, tv, pek, pev, w1k, w1v, w2k, w2v):
    bg, n_rows, width = tk.shape
    blk = lambda i: (i, 0, 0)
    const = lambda i: (0, 0)
    return pl.pallas_call(
        _compress_kernel,
        grid=(bg,),
        in_specs=[
            pl.BlockSpec((None, n_rows, width), blk),
            pl.BlockSpec((None, n_rows, width), blk),
            pl.BlockSpec(pek.shape, const),
            pl.BlockSpec(pev.shape, const),
            pl.BlockSpec(w1k.shape, const),
            pl.BlockSpec(w1v.shape, const),
            pl.BlockSpec(w2k.shape, const),
            pl.BlockSpec(w2v.shape, const),
        ],
        out_specs=[pl.BlockSpec((None, n_rows, HEAD_DIM), blk),
                   pl.BlockSpec((None, V_ROWS, n_rows), blk)],
        out_shape=[jax.ShapeDtypeStruct((bg, n_rows, HEAD_DIM), BF16),
                   jax.ShapeDtypeStruct((bg, V_ROWS, n_rows), BF16)],
        compiler_params=_cparams(1),
        name="compress",
    )(tk, tv, pek, pev, w1k, w1v, w2k, w2v)


def _nsa_attn_kernel(qt_ref, gt_ref, kc_ref, vct_ref, mapt_ref, ks_ref, vst_ref,
                     kw_ref, vwt_ref, y_ref, qaug_ref, s0_ref, s1_ref, c0_ref, c1_ref,
                     m_ref, acc_ref, ocmp_ref, owin_ref):
    i = pl.program_id(2)
    q0 = i * Q_TILE
    qt = qt_ref[...]
    col = lax.broadcasted_iota(jnp.int32, (1, Q_COLS), 1)
    tpos = q0 + (col & (Q_TILE - 1))

    def cmp_and_select(n_rows, n_blk):
        s = jnp.dot(kc_ref[0:n_rows, :], qt, preferred_element_type=F32)
        cmp_end = lax.broadcasted_iota(jnp.int32, (n_rows, 1), 0) * CMP_STRIDE + (CMP_LEN - 1)
        s = jnp.where(cmp_end <= tpos, s, NEG)
        m = jnp.max(s, axis=0, keepdims=True)
        p = jnp.exp2(s - jnp.maximum(m, 0.1 * NEG)).astype(BF16)
        acc = jnp.dot(vct_ref[:, 0:n_rows], p, preferred_element_type=F32)
        l = acc[HEAD_DIM:HEAD_DIM + 1, :]
        inv = jnp.where(l > 0.0, 1.0 / l, 0.0)
        ocmp_ref[...] = acc[0:HEAD_DIM, :] * inv

        mapt = mapt_ref[0:n_blk, 0:n_rows]
        imp = None
        for h in range(HEADS_PER_GROUP):
            cols = slice(h * Q_TILE, (h + 1) * Q_TILE)
            part = jnp.dot(mapt, p[:, cols], preferred_element_type=F32) * inv[:, cols]
            imp = part if imp is None else imp + part

        blk_i = lax.broadcasted_iota(jnp.int32, (n_blk, Q_TILE), 0)
        cur = (q0 + lax.broadcasted_iota(jnp.int32, (n_blk, Q_TILE), 1)) >> SLC_SHIFT
        blk = blk_i.astype(F32)
        val = jnp.where(blk_i <= cur, imp, -1.0)
        for forced in (blk_i == 0, blk_i == cur, blk_i == cur - 1):
            val = jnp.where(forced, 1e30, val)
        sel = jnp.zeros((n_blk, Q_TILE), F32)
        for _ in range(min(SLC_TOPN, n_blk)):
            mx = jnp.max(val, axis=0, keepdims=True)
            first = jnp.min(jnp.where(val == mx, blk, float(n_blk)), axis=0, keepdims=True)
            first = jnp.where(mx > -0.5, first, -1.0)
            pick = blk == first
            sel = jnp.where(pick, 1.0, sel)
            val = jnp.where(pick, -1.0, val)
        sel_bias = jnp.where(sel > 0.5, 0.0, -BIG).astype(BF16)
        for h in range(HEADS_PER_GROUP):
            qaug_ref[0:n_blk, h * Q_TILE:(h + 1) * Q_TILE] = sel_bias
        if n_blk < SLC_LANES:
            qaug_ref[n_blk:SLC_LANES, :] = jnp.full((SLC_LANES - n_blk, Q_COLS), -BIG, BF16)

    col_tiles = [slice(c0, c0 + MXU_COLS) for c0 in range(0, Q_COLS, MXU_COLS)]
    buffers = ((s0_ref, c0_ref), (s1_ref, c1_ref))

    def slc_scores(k0, s_ref, cmax_ref, cols):
        sc = jnp.dot(ks_ref[pl.ds(k0, SLC_KEYS), :], qaug_ref[:, cols],
                     preferred_element_type=F32)
        s_ref[:, cols] = sc
        cmax_ref[:, cols] = jnp.max(sc, axis=0, keepdims=True)

    span = WINDOW + Q_TILE
    t_loc = col & (Q_TILE - 1)

    def window_scores(interior):
        if not interior:
            sw = jnp.dot(kw_ref[0:span, :], qt, preferred_element_type=F32)
            kpos = lax.broadcasted_iota(jnp.int32, (span, 1), 0)
            sw = jnp.where(kpos <= tpos, jnp.where(kpos > tpos - WINDOW, sw, NEG), NEG)
            return [sw], 0
        start = pl.multiple_of(q0 - WINDOW, Q_TILE)
        i_loc = lax.broadcasted_iota(jnp.int32, (Q_TILE, 1), 0)
        n_parts = span // Q_TILE
        parts = []
        for r in range(n_parts):
            sw = jnp.dot(kw_ref[pl.ds(start + r * Q_TILE, Q_TILE), :], qt,
                         preferred_element_type=F32)
            if r == 0:
                sw = jnp.where(i_loc > t_loc, sw, NEG)
            elif r == n_parts - 1:
                sw = jnp.where(i_loc <= t_loc, sw, NEG)
            parts.append(sw)
        return parts, start

    def window_finish(parts, v_start):
        mw = None
        for sw in parts:
            cm = jnp.max(sw, axis=0, keepdims=True)
            mw = cm if mw is None else jnp.maximum(mw, cm)
        accw, off = None, 0
        for sw in parts:
            pw = jnp.exp2(sw - mw).astype(BF16)
            d = jnp.dot(vwt_ref[:, pl.ds(v_start + off, sw.shape[0])], pw,
                        preferred_element_type=F32)
            accw = d if accw is None else accw + d
            off += sw.shape[0]
        owin_ref[...] = accw[0:HEAD_DIM, :] / accw[HEAD_DIM:HEAD_DIM + 1, :]

    def front(n_rows, n_blk, interior):
        parts, v_start = window_scores(interior)
        cmp_and_select(n_rows, n_blk)
        qaug_ref[SLC_LANES:SLC_LANES + HEAD_DIM, :] = qt
        for cols in col_tiles:
            slc_scores(0, *buffers[0], cols)
        for cols in col_tiles[:-1]:
            slc_scores(SLC_KEYS, *buffers[1], cols)
        window_finish(parts, v_start)

    n_cmp_rows = kc_ref.shape[0]
    row_steps = list(range(LANES, n_cmp_rows, LANES)) + [n_cmp_rows]
    q_end = q0 + Q_TILE
    lo = 0
    for n_rows in row_steps:
        hi = n_rows * CMP_STRIDE
        n_blk = min(SLC_LANES, -(-(hi // SLC_BLOCK) // 16) * 16)
        cuts = [lo, WINDOW, hi] if lo < WINDOW < hi else [lo, hi]
        for c_lo, c_hi in zip(cuts[:-1], cuts[1:]):
            pl.when(jnp.logical_and(q_end > c_lo, q_end <= c_hi))(
                functools.partial(front, n_rows, n_blk, c_lo >= WINDOW))
        lo = hi

    def slc_update(k0, s_ref, cmax_ref, cols, masked):
        sc = s_ref[:, cols]
        if masked:
            kpos = k0 + lax.broadcasted_iota(jnp.int32, (SLC_KEYS, 1), 0)
            sc = jnp.where(kpos <= tpos[:, cols], sc, -BIG)
            cmax = jnp.max(sc, axis=0, keepdims=True)
        else:
            cmax = cmax_ref[:, cols]
        m_prev = m_ref[:, cols]
        m_new = jnp.maximum(m_prev, cmax)
        pr = jnp.exp2(sc - m_new).astype(BF16)
        pv = jnp.dot(vst_ref[:, pl.ds(k0, SLC_KEYS)], pr, preferred_element_type=F32)
        acc_ref[:, cols] = acc_ref[:, cols] * jnp.exp2(m_prev - m_new) + pv
        m_ref[:, cols] = m_new

    n_full = q0 // SLC_KEYS
    last_tile = ks_ref.shape[0] // SLC_KEYS - 1

    def tile_start(t):
        return pl.multiple_of(jnp.minimum(t, last_tile) * SLC_KEYS, SLC_KEYS)

    even, odd = buffers

    def slc_body(j, carry):
        k_even, k_odd = tile_start(2 * j), tile_start(2 * j + 1)
        k_even_next, k_odd_next = tile_start(2 * j + 2), tile_start(2 * j + 3)
        slc_scores(k_odd, *odd, col_tiles[-1])
        for cols in col_tiles:
            slc_update(k_even, *even, cols, masked=False)
            slc_scores(k_even_next, *even, cols)
        for cols in col_tiles[:-1]:
            slc_update(k_odd, *odd, cols, masked=False)
            slc_scores(k_odd_next, *odd, cols)
        slc_update(k_odd, *odd, col_tiles[-1], masked=False)
        return carry

    m_ref[...] = jnp.full(m_ref.shape, NEG, F32)
    acc_ref[...] = jnp.zeros(acc_ref.shape, F32)
    lax.fori_loop(0, n_full // 2, slc_body, 0)
    k_last = tile_start(n_full)

    @pl.when((n_full & 1) == 0)
    def _():
        for cols in col_tiles:
            slc_update(k_last, *even, cols, masked=True)

    @pl.when((n_full & 1) == 1)
    def _():
        slc_scores(k_last, *odd, col_tiles[-1])
        for cols in col_tiles:
            slc_update(tile_start(n_full - 1), *even, cols, masked=False)
        for cols in col_tiles:
            slc_update(k_last, *odd, cols, masked=True)

    o_slc = acc_ref[0:HEAD_DIM, :] / acc_ref[HEAD_DIM:HEAD_DIM + 1, :]

    g = gt_ref[...]

    def gate_row(branch):
        return jnp.concatenate(
            [g[h * 3 + branch:h * 3 + branch + 1, :] for h in range(HEADS_PER_GROUP)], axis=1)

    y = (gate_row(0) * ocmp_ref[...] + gate_row(1) * o_slc
         + gate_row(2) * owin_ref[...])
    for pair in range(HEADS_PER_GROUP // 2):
        lo, mid, hi = (2 * pair) * Q_TILE, (2 * pair + 1) * Q_TILE, (2 * pair + 2) * Q_TILE
        stacked = jnp.concatenate([y[:, lo:mid], y[:, mid:hi]], axis=0)
        y_ref[:, pair * LANES:(pair + 1) * LANES] = stacked.T


def _nsa_attn(qt, gt, kc, vct, mapt, ks, vst, kw, vwt):
    b, g, nt = qt.shape[:3]
    seq = ks.shape[2]
    n_cmp_rows = kc.shape[2]
    per_bg = lambda bi, gi, i: (bi, gi, 0, 0)
    return pl.pallas_call(
        _nsa_attn_kernel,
        grid=(b, g, nt),
        in_specs=[
            pl.BlockSpec((None, None, None, HEAD_DIM, Q_COLS), lambda bi, gi, i: (bi, gi, i, 0, 0)),
            pl.BlockSpec((None, None, GATE_ROWS, Q_TILE), lambda bi, gi, i: (bi, gi, 0, i)),
            pl.BlockSpec((None, None, n_cmp_rows, HEAD_DIM), per_bg),
            pl.BlockSpec((None, None, V_ROWS, n_cmp_rows), per_bg),
            pl.BlockSpec(mapt.shape, lambda bi, gi, i: (0, 0)),
            pl.BlockSpec((None, None, seq, HEAD_DIM + SLC_LANES), per_bg),
            pl.BlockSpec((None, None, V_ROWS, seq), per_bg),
            pl.BlockSpec((None, None, seq, HEAD_DIM), per_bg),
            pl.BlockSpec((None, None, V_ROWS, seq), per_bg),
        ],
        out_specs=pl.BlockSpec((Q_TILE, HEADS_PER_GROUP * HEAD_DIM),
                               lambda bi, gi, i: (bi * nt + i, gi)),
        out_shape=jax.ShapeDtypeStruct((b * seq, NSA_WIDTH), F32),
        scratch_shapes=[
            pltpu.VMEM((SLC_LANES + HEAD_DIM, Q_COLS), BF16),
            pltpu.VMEM((SLC_KEYS, Q_COLS), F32),
            pltpu.VMEM((SLC_KEYS, Q_COLS), F32),
            pltpu.VMEM((1, Q_COLS), F32),
            pltpu.VMEM((1, Q_COLS), F32),
            pltpu.VMEM((1, Q_COLS), F32),
            pltpu.VMEM((V_ROWS, Q_COLS), F32),
            pltpu.VMEM((HEAD_DIM, Q_COLS), F32),
            pltpu.VMEM((HEAD_DIM, Q_COLS), F32),
        ],
        compiler_params=_cparams(3),
        name="nsa_attn",
    )(qt, gt, kc, vct, mapt, ks, vst, kw, vwt)


SCAN_MXU_SIZES = (2, 4, 8)


def _scan_matrices(rows):
    t = np.arange(rows)[:, None]
    s = np.arange(rows)[None, :]
    mats = [(s // h == t // h) & (s <= t) for h in SCAN_MXU_SIZES]
    mats += [(s // h == t // h) & (s >= t) for h in SCAN_MXU_SIZES]
    return jnp.asarray(np.concatenate([np.tile(m, (1, 3)) for m in mats], axis=0), BF16)


def _segmented_scans(g, mats_ref):
    rows = g.shape[0]
    hi = g.astype(BF16)
    rest = g - hi.astype(F32)
    mid = rest.astype(BF16)
    lo = (rest - mid.astype(F32)).astype(BF16)
    small = jnp.dot(mats_ref[...], jnp.concatenate([hi, mid, lo], axis=0),
                    preferred_element_type=F32)
    n = len(SCAN_MXU_SIZES)
    out = {1: (g, g)}
    for j, h in enumerate(SCAN_MXU_SIZES):
        out[h] = (small[j * rows:(j + 1) * rows], small[(n + j) * rows:(n + j + 1) * rows])
    h = SCAN_MXU_SIZES[-1]
    pre, suf = out[h]
    while h < rows:
        pre_parts, suf_parts = [], []
        for b in range(rows // (2 * h)):
            lo_r, mid_r, hi_r = b * 2 * h, b * 2 * h + h, (b + 1) * 2 * h
            pre_parts += [pre[lo_r:mid_r], pre[mid_r:hi_r] + pre[mid_r - 1:mid_r, :]]
            suf_parts += [suf[lo_r:mid_r] + suf[mid_r:mid_r + 1, :], suf[mid_r:hi_r]]
        pre = jnp.concatenate(pre_parts, axis=0)
        suf = jnp.concatenate(suf_parts, axis=0)
        h *= 2
        out[h] = (pre, suf)
    return out


def _hgrn_kernel(hg_ref, lb_ref, gain_ref, mats_ref, o_ref, state_ref):
    c = pl.program_id(1)

    @pl.when(c == 0)
    def _():
        state_ref[...] = jnp.zeros_like(state_ref)

    rows = HG_CHUNK
    t_idx = lax.broadcasted_iota(jnp.int32, (rows, rows), 0)
    s_idx = lax.broadcasted_iota(jnp.int32, (rows, rows), 1)
    diff = jnp.where(t_idx > s_idx, t_idx ^ s_idx, 0)
    nt_dims = (((1,), (1,)), ((), ()))
    lb = lb_ref[...]
    gain = gain_ref[...]

    def heads(x):
        return [x[:, h * HG_DIM:(h + 1) * HG_DIM] for h in range(HG_HEADS)]

    def gates_and_scans(tok):
        w = HG_WIDTH
        q = hg_ref[tok, 0:w]
        f = lb + (1.0 - lb) * jax.nn.sigmoid(hg_ref[tok, w:2 * w])
        logf = jnp.log(f)
        return dict(tok=tok, qf=q * jax.nn.sigmoid(q), k=1.0 - f, logf=logf,
                    scans=_segmented_scans(logf, mats_ref))

    def intra_chunk(c):
        qf, k, logf, scans = c["qf"], c["k"], c["logf"], c["scans"]
        qb, kb = heads(qf.astype(BF16)), heads(k.astype(BF16))
        a = [jnp.where(t_idx == s_idx,
                       lax.dot_general(qb[h], kb[h], nt_dims, preferred_element_type=F32), 0.0)
             for h in range(HG_HEADS)]
        level, h_size = 0, 1
        while h_size < rows:
            pre, suf = scans[h_size]
            ql = heads((qf * jnp.exp(pre)).astype(BF16))
            kl = heads((k * jnp.exp(suf - logf)).astype(BF16))
            mask = (diff >> level) == 1
            for h in range(HG_HEADS):
                al = lax.dot_general(ql[h], kl[h], nt_dims, preferred_element_type=F32)
                a[h] = jnp.where(mask, al, a[h])
            level += 1
            h_size *= 2
        cum, rev = scans[rows]
        c.update(a=a, cum=cum,
                 q_in=heads((qf * jnp.exp(cum)).astype(BF16)),
                 k_out=heads((k * jnp.exp(rev - logf)).astype(BF16)))
        return c

    def outputs_and_state(c, states):
        tok, w = c["tok"], HG_WIDTH
        v = hg_ref[tok, 2 * w:3 * w]
        og = hg_ref[tok, 3 * w:4 * w]
        vh = heads(v.astype(BF16))
        new_states = []
        for h in range(HG_HEADS):
            lanes = slice(h * HG_DIM, (h + 1) * HG_DIM)
            state_t = states[h]
            o = jnp.dot(c["a"][h].astype(BF16), vh[h], preferred_element_type=F32)
            o = o + lax.dot_general(c["q_in"][h], state_t.astype(BF16), nt_dims,
                                    preferred_element_type=F32)
            decay = jnp.exp(c["cum"][rows - 1:rows, lanes])
            v_t = v[:, lanes].T.astype(BF16)
            new_states.append(
                state_t * decay + jnp.dot(v_t, c["k_out"][h], preferred_element_type=F32))
            ms = jnp.mean(o * o, axis=-1, keepdims=True)
            gate = og[:, lanes]
            o_ref[tok, lanes] = (o * lax.rsqrt(ms + EPS) * gain[:, lanes]
                                 * (gate * jax.nn.sigmoid(gate)))
        return new_states

    toks = [slice(j * rows, (j + 1) * rows) for j in range(hg_ref.shape[0] // rows)]
    chunks = [gates_and_scans(tok) for tok in toks]
    chunks = [intra_chunk(c) for c in chunks]
    states = [state_ref[h] for h in range(HG_HEADS)]
    for c in chunks:
        states = outputs_and_state(c, states)
    for h in range(HG_HEADS):
        state_ref[h] = states[h]


def _hgrn(hg_proj, lb, gain, batch, seq):
    n = hg_proj.shape[0]
    n_chunk = seq // HG_STEP
    mats = _scan_matrices(HG_CHUNK)
    row = lambda b, c: (b * n_chunk + c, 0)
    const = lambda b, c: (0, 0)
    return pl.pallas_call(
        _hgrn_kernel,
        grid=(batch, n_chunk),
        in_specs=[
            pl.BlockSpec((HG_STEP, 4 * HG_WIDTH), row),
            pl.BlockSpec((1, HG_WIDTH), const),
            pl.BlockSpec((1, HG_WIDTH), const),
            pl.BlockSpec(mats.shape, const),
        ],
        out_specs=pl.BlockSpec((HG_STEP, HG_WIDTH), row),
        out_shape=jax.ShapeDtypeStruct((n, HG_WIDTH), F32),
        scratch_shapes=[pltpu.VMEM((HG_HEADS, HG_DIM, HG_DIM), F32)],
        compiler_params=_cparams(2),
        name="hgrn",
    )(hg_proj, lb, gain, mats)


def _out_mlp_kernel(x_ref, yn_ref, yh_ref, gn_ref, gm_ref, wa_ref, wb_ref,
                    w1_ref, w2_ref, o_ref, *, ff_chunk):
    yn = yn_ref[...]
    ms = jnp.mean(yn * yn, axis=-1, keepdims=True)
    yn = (yn * lax.rsqrt(ms + EPS) * gn_ref[...]).astype(BF16)
    x1 = (x_ref[...]
          + jnp.dot(yn, wa_ref[...], preferred_element_type=F32)
          + jnp.dot(yh_ref[...].astype(BF16), wb_ref[...], preferred_element_type=F32))
    ms = jnp.mean(x1 * x1, axis=-1, keepdims=True)
    h = (x1 * lax.rsqrt(ms + EPS) * gm_ref[...]).astype(BF16)
    mlp = None
    for c in range(w1_ref.shape[1] // ff_chunk):
        cols = slice(c * ff_chunk, (c + 1) * ff_chunk)
        u = jnp.maximum(jnp.dot(h, w1_ref[:, cols], preferred_element_type=F32), 0.0)
        d = jnp.dot((u * u).astype(BF16), w2_ref[cols, :], preferred_element_type=F32)
        mlp = d if mlp is None else mlp + d
    o_ref[...] = x1 + mlp


def _out_mlp(x2, y_nsa, y_hg, gn, gm, wa, wb, w1, w2, tm=256, ff_chunk=1024):
    n, d = x2.shape
    row = lambda i: (i, 0)
    const = lambda i: (0, 0)

    def resident(shape):
        return pl.BlockSpec(shape, const)

    return pl.pallas_call(
        functools.partial(_out_mlp_kernel, ff_chunk=ff_chunk),
        grid=(n // tm,),
        in_specs=[
            pl.BlockSpec((tm, d), row),
            pl.BlockSpec((tm, y_nsa.shape[1]), row),
            pl.BlockSpec((tm, y_hg.shape[1]), row),
            pl.BlockSpec((1, y_nsa.shape[1]), const),
            pl.BlockSpec((1, d), const),
            resident(wa.shape),
            resident(wb.shape),
            resident(w1.shape),
            resident(w2.shape),
        ],
        out_specs=pl.BlockSpec((tm, d), row),
        out_shape=jax.ShapeDtypeStruct((n, d), F32),
        compiler_params=_cparams(1),
        name="out_mlp",
    )(x2, y_nsa, y_hg, gn, gm, wa, wb, w1, w2)


def _rope_tables(seq):
    pos = jnp.arange(seq, dtype=F32)
    inv_freq = ROPE_THETA ** (-jnp.arange(0, ROT_DIM, 2, dtype=F32) / ROT_DIM)
    ang = pos[:, None] * inv_freq[None, :]
    cos, sin = jnp.cos(ang), jnp.sin(ang)
    rest = HEAD_DIM - ROT_DIM
    cos_h = jnp.concatenate([cos, cos, jnp.ones((seq, rest), F32)], axis=-1)
    sin_h = jnp.concatenate([-sin, sin, jnp.zeros((seq, rest), F32)], axis=-1)
    return jnp.tile(cos_h, (1, 2)), jnp.tile(sin_h, (1, 2))


def _cmp_to_slc_t(n_cmp_rows, n_cmp, n_slc):
    tok = np.arange(n_cmp)[:, None] * CMP_STRIDE + np.arange(CMP_LEN)[None, :]
    frac = (tok[:, :, None] // SLC_BLOCK == np.arange(n_slc)[None, None, :]).mean(axis=1)
    out = np.zeros((SLC_LANES, n_cmp_rows), np.float32)
    out[:n_slc, :n_cmp] = frac.T
    return jnp.asarray(out, BF16)


def kernel(x, attn_norm, w_in, q_norm, k_norm, cmp_pe_k, cmp_w1_k, cmp_w2_k, cmp_pe_v,
           cmp_w1_v, cmp_w2_v, nsa_out_norm, hgrn_lb, hgrn_out_norm, w_out, mlp_norm,
           w_mlp_in, w_mlp_out):
    batch, seq, d_model = x.shape
    n = batch * seq
    n_slc = seq // SLC_BLOCK
    n_cmp_rows = seq // CMP_STRIDE
    n_cmp = (seq - CMP_LEN) // CMP_STRIDE + 1
    assert n_slc <= SLC_LANES and seq % SLC_KEYS == 0 and seq >= max(WINDOW + Q_TILE, 2 * SLC_KEYS)
    assert w_in.shape[0] == 1, "single-layer block"

    kvw = KV_WIDTH
    o_q, o_kc, o_vc, o_ks, o_vs, o_kw, o_vw, o_gate = np.cumsum(
        [0, NSA_WIDTH, kvw, kvw, kvw, kvw, kvw, kvw])
    o_hg = o_gate + NSA_HEADS * 3
    w = w_in[0]
    pad = jnp.zeros((d_model, NSA_COLS - (NSA_WIDTH + 6 * kvw + NSA_HEADS * 3)), w.dtype)
    w_nsa = jnp.concatenate(
        [w[:, o_q:o_kc], w[:, o_kc:o_vc], w[:, o_ks:o_vs], w[:, o_kw:o_vw],
         w[:, o_vc:o_ks], w[:, o_vs:o_kw], w[:, o_vw:o_gate], w[:, o_gate:o_hg], pad],
        axis=1).astype(BF16)
    w_hg = w[:, o_hg:].astype(BF16)
    x2 = x.reshape(n, d_model)
    cos_t, sin_t = _rope_tables(seq)
    qg = jnp.tile(q_norm[0], 2)[None, :]
    kg = jnp.tile(k_norm[0], 2)[None, :]
    hg_proj, qt, kc, ks_aug, kw, vc, vs_t, vw_t, gt = _inproj(
        x2, attn_norm[0][None, :], w_nsa, w_hg, cos_t, sin_t, qg, kg, batch, seq)


    blocks = lambda t: t.reshape(batch * NSA_GROUPS, n_cmp_rows, CMP_STRIDE * HEAD_DIM)
    pe8 = lambda pe: jnp.broadcast_to(pe.reshape(1, CMP_LEN * HEAD_DIM), (8, CMP_LEN * HEAD_DIM)).astype(BF16)
    k_cmp, v_cmp_t = _compress(
        blocks(kc), blocks(vc), pe8(cmp_pe_k[0]), pe8(cmp_pe_v[0]),
        cmp_w1_k[0].astype(BF16), cmp_w1_v[0].astype(BF16),
        cmp_w2_k[0].astype(BF16), cmp_w2_v[0].astype(BF16))
    k_cmp = k_cmp.reshape(batch, NSA_GROUPS, n_cmp_rows, HEAD_DIM)
    v_cmp_t = v_cmp_t.reshape(batch, NSA_GROUPS, V_ROWS, n_cmp_rows)
    y_nsa = _nsa_attn(qt, gt, k_cmp, v_cmp_t, _cmp_to_slc_t(n_cmp_rows, n_cmp, n_slc),
                      ks_aug, vs_t, kw, vw_t)

    lb_all = jnp.cumsum(jax.nn.softmax(hgrn_lb.astype(F32), axis=0), axis=0)
    y_hg = _hgrn(hg_proj, lb_all[0][None, :], jnp.tile(hgrn_out_norm[0], HG_HEADS)[None, :],
                 batch, seq)

    wo = w_out[0].astype(BF16)
    out = _out_mlp(x2, y_nsa, y_hg, nsa_out_norm[0][None, :], mlp_norm[0][None, :],
                   wo[:NSA_WIDTH], wo[NSA_WIDTH:], w_mlp_in[0].astype(BF16),
                   w_mlp_out[0].astype(BF16))
    return out.reshape(batch, seq, d_model)
```

```python
import functools

import jax
import jax.numpy as jnp
import numpy as np
from jax import lax
from jax.experimental import pallas as pl
from jax.experimental.pallas import tpu as pltpu

F32 = jnp.float32
BF16 = jnp.bfloat16

NSA_HEADS = 8
NSA_GROUPS = 2
HEADS_PER_GROUP = NSA_HEADS // NSA_GROUPS
HEAD_DIM = 64
NSA_WIDTH = NSA_HEADS * HEAD_DIM
KV_WIDTH = NSA_GROUPS * HEAD_DIM
CMP_LEN = 32
CMP_STRIDE = 16
CMP_HIDDEN = 2 * HEAD_DIM
SLC_BLOCK = 64
SLC_SHIFT = 6
SLC_TOPN = 16
WINDOW = 512
ROT_DIM = HEAD_DIM // 4
ROPE_THETA = 500000.0
HG_HEADS = 4
HG_DIM = 128
HG_WIDTH = HG_HEADS * HG_DIM
EPS = 1e-6

LANES = 128
Q_TILE = 256
Q_COLS = HEADS_PER_GROUP * Q_TILE
SLC_KEYS = 512
SLC_LANES = 128
MXU_COLS = 256
V_ROWS = 80
GATE_ROWS = 16
NSA_COLS = 1408
NEG = -1e30
Q_SCALE = HEAD_DIM ** -0.5 * 1.4426950408889634
BIG = 2.0 ** 100
HG_CHUNK = 128
HG_STEP = 512
VMEM_LIMIT = 56 * 1024 * 1024


def _cparams(n_grid):
    return pltpu.CompilerParams(
        dimension_semantics=("arbitrary",) * n_grid, vmem_limit_bytes=VMEM_LIMIT)


def _value_rows_tail(cols):
    r = lax.broadcasted_iota(jnp.int32, (V_ROWS - HEAD_DIM, cols), 0)
    return jnp.where(r == 0, 1.0, 0.0).astype(BF16)


def _inproj_kernel(x_ref, g_ref, wn_ref, wh_ref, cos_ref, sin_ref, qg_ref, kg_ref,
                   hg_ref, qt_ref, kc_ref, ks_ref, kw_ref, vc_ref, vst_ref, vwt_ref, gt_ref):
    tq = x_ref.shape[0]
    x = x_ref[...]
    ms = jnp.mean(x * x, axis=-1, keepdims=True)
    h = (x * lax.rsqrt(ms + EPS) * g_ref[...]).astype(BF16)
    p = jnp.dot(h, wn_ref[...], preferred_element_type=F32)
    hg_ref[...] = jnp.dot(h, wh_ref[...], preferred_element_type=F32)
    cos = cos_ref[...]
    sin = sin_ref[...]
    lane = lax.broadcasted_iota(jnp.int32, (tq, LANES), 1)
    low_head = lane < HEAD_DIM
    dim = lane & (HEAD_DIM - 1)
    first_half = dim < ROT_DIM // 2

    def norm_rope(xs, gain, scale):
        x2 = xs * xs
        s_lo = jnp.sum(jnp.where(low_head, x2, 0.0), axis=-1, keepdims=True)
        s_hi = jnp.sum(jnp.where(low_head, 0.0, x2), axis=-1, keepdims=True)
        ms = jnp.where(low_head, s_lo, s_hi) * (1.0 / HEAD_DIM)
        y = xs * lax.rsqrt(ms + EPS) * gain
        partner = jnp.where(first_half,
                            pltpu.roll(y, LANES - ROT_DIM // 2, 1),
                            pltpu.roll(y, ROT_DIM // 2, 1))
        return (y * cos + partner * sin) * scale

    def split_groups(ref, x, lanes=slice(None)):
        for g in range(NSA_GROUPS):
            ref[g, :, lanes] = x[:, g * HEAD_DIM:(g + 1) * HEAD_DIM].astype(BF16)

    def split_groups_t(ref, x):
        xt = x.T
        for g in range(NSA_GROUPS):
            ref[g, 0:HEAD_DIM, :] = xt[g * HEAD_DIM:(g + 1) * HEAD_DIM, :].astype(BF16)
            ref[g, HEAD_DIM:V_ROWS, :] = _value_rows_tail(tq)

    qg = qg_ref[...]
    kg = kg_ref[...]
    for s in range(NSA_WIDTH // LANES):
        yt = norm_rope(p[:, s * LANES:(s + 1) * LANES], qg, Q_SCALE).T
        g, h0 = divmod(2 * s, HEADS_PER_GROUP)
        for j in range(2):
            qt_ref[g, :, (h0 + j) * tq:(h0 + j + 1) * tq] = (
                yt[j * HEAD_DIM:(j + 1) * HEAD_DIM, :].astype(BF16))
    base = NSA_WIDTH
    split_groups(kc_ref, norm_rope(p[:, base:base + LANES], kg, 1.0))
    tok = pl.program_id(1) * tq + lax.broadcasted_iota(jnp.int32, (tq, SLC_LANES), 0)
    blk = lax.broadcasted_iota(jnp.int32, (tq, SLC_LANES), 1)
    one_hot = jnp.where((tok >> SLC_SHIFT) == blk, 1.0, 0.0).astype(BF16)
    for g in range(NSA_GROUPS):
        ks_ref[g, :, 0:SLC_LANES] = one_hot
    split_groups(ks_ref, norm_rope(p[:, base + LANES:base + 2 * LANES], kg, 1.0),
                 slice(SLC_LANES, SLC_LANES + HEAD_DIM))
    split_groups(kw_ref, norm_rope(p[:, base + 2 * LANES:base + 3 * LANES], kg, 1.0))
    base += 3 * LANES
    split_groups(vc_ref, p[:, base:base + LANES])
    split_groups_t(vst_ref, p[:, base + LANES:base + 2 * LANES])
    split_groups_t(vwt_ref, p[:, base + 2 * LANES:base + 3 * LANES])
    base += 3 * LANES
    gates_t = jax.nn.sigmoid(p[:, base:base + LANES]).T
    per_group = HEADS_PER_GROUP * 3
    for g in range(NSA_GROUPS):
        gt_ref[g, 0:per_group, :] = gates_t[g * per_group:(g + 1) * per_group, :]
        gt_ref[g, per_group:GATE_ROWS, :] = jnp.zeros((GATE_ROWS - per_group, tq), F32)


def _inproj(x2, gain, w_nsa, w_hg, cos_t, sin_t, qg, kg, batch, seq):
    tq = Q_TILE
    nt = seq // tq
    d = x2.shape[1]
    tok = lambda b, i: (b * nt + i, 0)
    const = lambda b, i: (0, 0)
    rows = lambda b, i: (b, 0, i, 0)
    cols = lambda b, i: (b, 0, 0, i)
    grp = NSA_GROUPS
    return pl.pallas_call(
        _inproj_kernel,
        grid=(batch, nt),
        in_specs=[
            pl.BlockSpec((tq, d), tok),
            pl.BlockSpec((1, d), const),
            pl.BlockSpec(w_nsa.shape, const),
            pl.BlockSpec(w_hg.shape, const),
            pl.BlockSpec((tq, LANES), lambda b, i: (i, 0)),
            pl.BlockSpec((tq, LANES), lambda b, i: (i, 0)),
            pl.BlockSpec((1, LANES), const),
            pl.BlockSpec((1, LANES), const),
        ],
        out_specs=[
            pl.BlockSpec((tq, w_hg.shape[1]), tok),
            pl.BlockSpec((None, grp, None, HEAD_DIM, Q_COLS), lambda b, i: (b, 0, i, 0, 0)),
            pl.BlockSpec((None, grp, tq, HEAD_DIM), rows),
            pl.BlockSpec((None, grp, tq, SLC_LANES + HEAD_DIM), rows),
            pl.BlockSpec((None, grp, tq, HEAD_DIM), rows),
            pl.BlockSpec((None, grp, tq, HEAD_DIM), rows),
            pl.BlockSpec((None, grp, V_ROWS, tq), cols),
            pl.BlockSpec((None, grp, V_ROWS, tq), cols),
            pl.BlockSpec((None, grp, GATE_ROWS, tq), cols),
        ],
        out_shape=[
            jax.ShapeDtypeStruct((batch * seq, w_hg.shape[1]), F32),
            jax.ShapeDtypeStruct((batch, grp, nt, HEAD_DIM, Q_COLS), BF16),
            jax.ShapeDtypeStruct((batch, grp, seq, HEAD_DIM), BF16),
            jax.ShapeDtypeStruct((batch, grp, seq, SLC_LANES + HEAD_DIM), BF16),
            jax.ShapeDtypeStruct((batch, grp, seq, HEAD_DIM), BF16),
            jax.ShapeDtypeStruct((batch, grp, seq, HEAD_DIM), BF16),
            jax.ShapeDtypeStruct((batch, grp, V_ROWS, seq), BF16),
            jax.ShapeDtypeStruct((batch, grp, V_ROWS, seq), BF16),
            jax.ShapeDtypeStruct((batch, grp, GATE_ROWS, seq), F32),
        ],
        compiler_params=_cparams(2),
        name="inproj",
    )(x2, gain, w_nsa, w_hg, cos_t, sin_t, qg, kg)


def _compress_kernel(tk_ref, tv_ref, pek_ref, pev_ref, w1k_ref, w1v_ref,
                     w2k_ref, w2v_ref, ok_ref, ov_ref):
    half = CMP_STRIDE * HEAD_DIM

    def run(t_ref, pe_ref, w1_ref, w2_ref, o_ref, transposed):
        t = t_ref[...]
        n_rows = t.shape[0]
        top = jnp.dot(t, w1_ref[0:half, :], preferred_element_type=F32)
        bot = jnp.dot(t, w1_ref[half:2 * half, :], preferred_element_type=F32)
        bot_next = pltpu.roll(bot, n_rows - 1, 0)
        pe = jnp.dot(pe_ref[...], w1_ref[...], preferred_element_type=F32)[0:1, :]
        h = top + bot_next + pe
        h = h * jax.nn.sigmoid(h)
        o = jnp.dot(h.astype(BF16), w2_ref[...], preferred_element_type=F32)
        r = lax.broadcasted_iota(jnp.int32, o.shape, 0)
        o = jnp.where(r < n_rows - 1, o, 0.0)
        if transposed:
            ot = jnp.concatenate([o, jnp.zeros_like(o)], axis=1).T
            o_ref[0:HEAD_DIM, :] = ot[0:HEAD_DIM, :].astype(BF16)
            o_ref[HEAD_DIM:V_ROWS, :] = _value_rows_tail(n_rows)
        else:
            o_ref[...] = o.astype(BF16)

    run(tk_ref, pek_ref, w1k_ref, w2k_ref, ok_ref, False)
    run(tv_ref, pev_ref, w1v_ref, w2v_ref, ov_ref, True)


def _compress(tk, tv, pek, pev, w1k, w1v, w2k, w2v):
    bg, n_rows, width = tk.shape
    blk = lambda i: (i, 0, 0)
    const = lambda i: (0, 0)
    return pl.pallas_call(
        _compress_kernel,
        grid=(bg,),
        in_specs=[
            pl.BlockSpec((None, n_rows, width), blk),
            pl.BlockSpec((None, n_rows, width), blk),
            pl.BlockSpec(pek.shape, const),
            pl.BlockSpec(pev.shape, const),
            pl.BlockSpec(w1k.shape, const),
            pl.BlockSpec(w1v.shape, const),
            pl.BlockSpec(w2k.shape, const),
            pl.BlockSpec(w2v.shape, const),
        ],
        out_specs=[pl.BlockSpec((None, n_rows, HEAD_DIM), blk),
                   pl.BlockSpec((None, V_ROWS, n_rows), blk)],
        out_shape=[jax.ShapeDtypeStruct((bg, n_rows, HEAD_DIM), BF16),
                   jax.ShapeDtypeStruct((bg, V_ROWS, n_rows), BF16)],
        compiler_params=_cparams(1),
        name="compress",
    )(tk, tv, pek, pev, w1k, w1v, w2k, w2v)


def _nsa_attn_kernel(qt_ref, gt_ref, kc_ref, vct_ref, mapt_ref, ks_ref, vst_ref,
                     kw_ref, vwt_ref, y_ref, qaug_ref, s0_ref, s1_ref, c0_ref, c1_ref,
                     m_ref, acc_ref, ocmp_ref, owin_ref):
    i = pl.program_id(2)
    q0 = i * Q_TILE
    qt = qt_ref[...]
    col = lax.broadcasted_iota(jnp.int32, (1, Q_COLS), 1)
    tpos = q0 + (col & (Q_TILE - 1))

    def compressed_branch(n_rows, n_blk):
        kc = kc_ref[0:n_rows, :]
        cmp_end = lax.broadcasted_iota(jnp.int32, (n_rows, 1), 0) * CMP_STRIDE + (CMP_LEN - 1)
        mapt = mapt_ref[0:n_blk, 0:n_rows]
        head_cols = [slice(h * Q_TILE, (h + 1) * Q_TILE) for h in range(HEADS_PER_GROUP)]
        scores = [jnp.dot(kc, qt[:, cols], preferred_element_type=F32) for cols in head_cols]
        imp = None
        for cols, s in zip(head_cols, scores):
            s = jnp.where(cmp_end <= tpos[:, cols], s, NEG)
            m = jnp.max(s, axis=0, keepdims=True)
            p = jnp.exp2(s - jnp.maximum(m, 0.1 * NEG)).astype(BF16)
            acc = jnp.dot(vct_ref[:, 0:n_rows], p, preferred_element_type=F32)
            l = acc[HEAD_DIM:HEAD_DIM + 1, :]
            inv = jnp.where(l > 0.0, 1.0 / l, 0.0)
            ocmp_ref[:, cols] = acc[0:HEAD_DIM, :] * inv
            part = jnp.dot(mapt, p, preferred_element_type=F32) * inv
            imp = part if imp is None else imp + part
        return imp

    def select_blocks(imp, n_blk):
        blk_i = lax.broadcasted_iota(jnp.int32, (n_blk, Q_TILE), 0)
        cur = (q0 + lax.broadcasted_iota(jnp.int32, (n_blk, Q_TILE), 1)) >> SLC_SHIFT
        blk = blk_i.astype(F32)
        sel = jnp.zeros((n_blk, Q_TILE), F32)
        for forced in (blk_i == 0, blk_i == cur, blk_i == cur - 1):
            sel = jnp.where(forced, 1.0, sel)
        val = jnp.where(blk_i <= cur, jnp.where(sel > 0.5, -1.0, imp), -1.0)
        for _ in range(min(SLC_TOPN, n_blk) - 3):
            mx = jnp.max(val, axis=0, keepdims=True)
            first = jnp.min(jnp.where(val == mx, blk, float(n_blk)), axis=0, keepdims=True)
            first = jnp.where(mx > -0.5, first, -1.0)
            pick = blk == first
            sel = jnp.where(pick, 1.0, sel)
            val = jnp.where(pick, -1.0, val)
        sel_bias = jnp.where(sel > 0.5, 0.0, -BIG).astype(BF16)
        for h in range(HEADS_PER_GROUP):
            qaug_ref[0:n_blk, h * Q_TILE:(h + 1) * Q_TILE] = sel_bias
        if n_blk < SLC_LANES:
            qaug_ref[n_blk:SLC_LANES, :] = jnp.full((SLC_LANES - n_blk, Q_COLS), -BIG, BF16)

    col_tiles = [slice(c0, c0 + MXU_COLS) for c0 in range(0, Q_COLS, MXU_COLS)]
    buffers = ((s0_ref, c0_ref), (s1_ref, c1_ref))

    def slc_scores(k0, s_ref, cmax_ref, cols):
        sc = jnp.dot(ks_ref[pl.ds(k0, SLC_KEYS), :], qaug_ref[:, cols],
                     preferred_element_type=F32)
        s_ref[:, cols] = sc
        cmax_ref[:, cols] = jnp.max(sc, axis=0, keepdims=True)

    span = WINDOW + Q_TILE
    t_loc = col & (Q_TILE - 1)

    def window_scores(interior):
        if not interior:
            sw = jnp.dot(kw_ref[0:span, :], qt, preferred_element_type=F32)
            kpos = lax.broadcasted_iota(jnp.int32, (span, 1), 0)
            sw = jnp.where(kpos <= tpos, jnp.where(kpos > tpos - WINDOW, sw, NEG), NEG)
            return [sw], 0
        start = pl.multiple_of(q0 - WINDOW, Q_TILE)
        i_loc = lax.broadcasted_iota(jnp.int32, (Q_TILE, 1), 0)
        n_parts = span // Q_TILE
        parts = []
        for r in range(n_parts):
            sw = jnp.dot(kw_ref[pl.ds(start + r * Q_TILE, Q_TILE), :], qt,
                         preferred_element_type=F32)
            if r == 0:
                sw = jnp.where(i_loc > t_loc, sw, NEG)
            elif r == n_parts - 1:
                sw = jnp.where(i_loc <= t_loc, sw, NEG)
            parts.append(sw)
        return parts, start

    def window_finish(parts, v_start):
        mw = None
        for sw in parts:
            cm = jnp.max(sw, axis=0, keepdims=True)
            mw = cm if mw is None else jnp.maximum(mw, cm)
        accw, off = None, 0
        for sw in parts:
            pw = jnp.exp2(sw - mw).astype(BF16)
            d = jnp.dot(vwt_ref[:, pl.ds(v_start + off, sw.shape[0])], pw,
                        preferred_element_type=F32)
            accw = d if accw is None else accw + d
            off += sw.shape[0]
        owin_ref[...] = accw[0:HEAD_DIM, :] / accw[HEAD_DIM:HEAD_DIM + 1, :]

    def front(n_rows, n_blk, interior):
        imp = compressed_branch(n_rows, n_blk)
        window_finish(*window_scores(interior))
        select_blocks(imp, n_blk)
        qaug_ref[SLC_LANES:SLC_LANES + HEAD_DIM, :] = qt
        for cols in col_tiles:
            slc_scores(0, *buffers[0], cols)
        for cols in col_tiles[:-1]:
            slc_scores(SLC_KEYS, *buffers[1], cols)

    n_cmp_rows = kc_ref.shape[0]
    row_steps = list(range(LANES, n_cmp_rows, LANES)) + [n_cmp_rows]
    q_end = q0 + Q_TILE
    lo = 0
    for n_rows in row_steps:
        hi = n_rows * CMP_STRIDE
        n_blk = min(SLC_LANES, -(-(hi // SLC_BLOCK) // 16) * 16)
        cuts = [lo, WINDOW, hi] if lo < WINDOW < hi else [lo, hi]
        for c_lo, c_hi in zip(cuts[:-1], cuts[1:]):
            pl.when(jnp.logical_and(q_end > c_lo, q_end <= c_hi))(
                functools.partial(front, n_rows, n_blk, c_lo >= WINDOW))
        lo = hi

    def slc_update(k0, s_ref, cmax_ref, cols, masked):
        sc = s_ref[:, cols]
        if masked:
            kpos = k0 + lax.broadcasted_iota(jnp.int32, (SLC_KEYS, 1), 0)
            sc = jnp.where(kpos <= tpos[:, cols], sc, -BIG)
            cmax = jnp.max(sc, axis=0, keepdims=True)
        else:
            cmax = cmax_ref[:, cols]
        m_prev = m_ref[:, cols]
        m_new = jnp.maximum(m_prev, cmax)
        pr = jnp.exp2(sc - m_new).astype(BF16)
        pv = jnp.dot(vst_ref[:, pl.ds(k0, SLC_KEYS)], pr, preferred_element_type=F32)
        acc_ref[:, cols] = acc_ref[:, cols] * jnp.exp2(m_prev - m_new) + pv
        m_ref[:, cols] = m_new

    n_full = q0 // SLC_KEYS
    last_tile = ks_ref.shape[0] // SLC_KEYS - 1

    def tile_start(t):
        return pl.multiple_of(jnp.minimum(t, last_tile) * SLC_KEYS, SLC_KEYS)

    even, odd = buffers

    def slc_body(j, carry):
        k_even, k_odd = tile_start(2 * j), tile_start(2 * j + 1)
        k_even_next, k_odd_next = tile_start(2 * j + 2), tile_start(2 * j + 3)
        slc_scores(k_odd, *odd, col_tiles[-1])
        for cols in col_tiles:
            slc_update(k_even, *even, cols, masked=False)
            slc_scores(k_even_next, *even, cols)
        for cols in col_tiles[:-1]:
            slc_update(k_odd, *odd, cols, masked=False)
            slc_scores(k_odd_next, *odd, cols)
        slc_update(k_odd, *odd, col_tiles[-1], masked=False)
        return carry

    m_ref[...] = jnp.full(m_ref.shape, NEG, F32)
    acc_ref[...] = jnp.zeros(acc_ref.shape, F32)
    lax.fori_loop(0, n_full // 2, slc_body, 0)
    k_last = tile_start(n_full)

    @pl.when((n_full & 1) == 0)
    def _():
        for cols in col_tiles:
            slc_update(k_last, *even, cols, masked=True)

    @pl.when((n_full & 1) == 1)
    def _():
        slc_scores(k_last, *odd, col_tiles[-1])
        for cols in col_tiles:
            slc_update(tile_start(n_full - 1), *even, cols, masked=False)
        for cols in col_tiles:
            slc_update(k_last, *odd, cols, masked=True)

    o_slc = acc_ref[0:HEAD_DIM, :] / acc_ref[HEAD_DIM:HEAD_DIM + 1, :]

    g = gt_ref[...]

    def gate_row(branch):
        return jnp.concatenate(
            [g[h * 3 + branch:h * 3 + branch + 1, :] for h in range(HEADS_PER_GROUP)], axis=1)

    y = (gate_row(0) * ocmp_ref[...] + gate_row(1) * o_slc
         + gate_row(2) * owin_ref[...])
    for pair in range(HEADS_PER_GROUP // 2):
        lo, mid, hi = (2 * pair) * Q_TILE, (2 * pair + 1) * Q_TILE, (2 * pair + 2) * Q_TILE
        stacked = jnp.concatenate([y[:, lo:mid], y[:, mid:hi]], axis=0)
        y_ref[:, pair * LANES:(pair + 1) * LANES] = stacked.T


def _nsa_attn(qt, gt, kc, vct, mapt, ks, vst, kw, vwt):
    b, g, nt = qt.shape[:3]
    seq = ks.shape[2]
    n_cmp_rows = kc.shape[2]
    per_bg = lambda bi, gi, i: (bi, gi, 0, 0)
    return pl.pallas_call(
        _nsa_attn_kernel,
        grid=(b, g, nt),
        in_specs=[
            pl.BlockSpec((None, None, None, HEAD_DIM, Q_COLS), lambda bi, gi, i: (bi, gi, i, 0, 0)),
            pl.BlockSpec((None, None, GATE_ROWS, Q_TILE), lambda bi, gi, i: (bi, gi, 0, i)),
            pl.BlockSpec((None, None, n_cmp_rows, HEAD_DIM), per_bg),
            pl.BlockSpec((None, None, V_ROWS, n_cmp_rows), per_bg),
            pl.BlockSpec(mapt.shape, lambda bi, gi, i: (0, 0)),
            pl.BlockSpec((None, None, seq, HEAD_DIM + SLC_LANES), per_bg),
            pl.BlockSpec((None, None, V_ROWS, seq), per_bg),
            pl.BlockSpec((None, None, seq, HEAD_DIM), per_bg),
            pl.BlockSpec((None, None, V_ROWS, seq), per_bg),
        ],
        out_specs=pl.BlockSpec((Q_TILE, HEADS_PER_GROUP * HEAD_DIM),
                               lambda bi, gi, i: (bi * nt + i, gi)),
        out_shape=jax.ShapeDtypeStruct((b * seq, NSA_WIDTH), F32),
        scratch_shapes=[
            pltpu.VMEM((SLC_LANES + HEAD_DIM, Q_COLS), BF16),
            pltpu.VMEM((SLC_KEYS, Q_COLS), F32),
            pltpu.VMEM((SLC_KEYS, Q_COLS), F32),
            pltpu.VMEM((1, Q_COLS), F32),
            pltpu.VMEM((1, Q_COLS), F32),
            pltpu.VMEM((1, Q_COLS), F32),
            pltpu.VMEM((V_ROWS, Q_COLS), F32),
            pltpu.VMEM((HEAD_DIM, Q_COLS), F32),
            pltpu.VMEM((HEAD_DIM, Q_COLS), F32),
        ],
        compiler_params=_cparams(3),
        name="nsa_attn",
    )(qt, gt, kc, vct, mapt, ks, vst, kw, vwt)


SCAN_MXU_SIZES = (2, 4, 8)


def _scan_matrices(rows):
    t = np.arange(rows)[:, None]
    s = np.arange(rows)[None, :]
    mats = [(s // h == t // h) & (s <= t) for h in SCAN_MXU_SIZES]
    mats += [(s // h == t // h) & (s >= t) for h in SCAN_MXU_SIZES]
    return jnp.asarray(np.concatenate([np.tile(m, (1, 3)) for m in mats], axis=0), BF16)


def _segmented_scans(g, mats_ref):
    rows = g.shape[0]
    hi = g.astype(BF16)
    rest = g - hi.astype(F32)
    mid = rest.astype(BF16)
    lo = (rest - mid.astype(F32)).astype(BF16)
    small = jnp.dot(mats_ref[...], jnp.concatenate([hi, mid, lo], axis=0),
                    preferred_element_type=F32)
    n = len(SCAN_MXU_SIZES)
    out = {1: (g, g)}
    for j, h in enumerate(SCAN_MXU_SIZES):
        out[h] = (small[j * rows:(j + 1) * rows], small[(n + j) * rows:(n + j + 1) * rows])
    h = SCAN_MXU_SIZES[-1]
    pre, suf = out[h]
    while h < rows:
        pre_parts, suf_parts = [], []
        for b in range(rows // (2 * h)):
            lo_r, mid_r, hi_r = b * 2 * h, b * 2 * h + h, (b + 1) * 2 * h
            pre_parts += [pre[lo_r:mid_r], pre[mid_r:hi_r] + pre[mid_r - 1:mid_r, :]]
            suf_parts += [suf[lo_r:mid_r] + suf[mid_r:mid_r + 1, :], suf[mid_r:hi_r]]
        pre = jnp.concatenate(pre_parts, axis=0)
        suf = jnp.concatenate(suf_parts, axis=0)
        h *= 2
        out[h] = (pre, suf)
    return out


def _hgrn_kernel(hg_ref, lb_ref, gain_ref, mats_ref, o_ref, state_ref):
    c = pl.program_id(1)

    @pl.when(c == 0)
    def _():
        state_ref[...] = jnp.zeros_like(state_ref)

    rows = HG_CHUNK
    t_idx = lax.broadcasted_iota(jnp.int32, (rows, rows), 0)
    s_idx = lax.broadcasted_iota(jnp.int32, (rows, rows), 1)
    diff = jnp.where(t_idx > s_idx, t_idx ^ s_idx, 0)
    nt_dims = (((1,), (1,)), ((), ()))
    lb = lb_ref[...]
    gain = gain_ref[...]

    def heads(x):
        return [x[:, h * HG_DIM:(h + 1) * HG_DIM] for h in range(HG_HEADS)]

    def gates_and_scans(tok):
        w = HG_WIDTH
        q = hg_ref[tok, 0:w]
        f = lb + (1.0 - lb) * jax.nn.sigmoid(hg_ref[tok, w:2 * w])
        logf = jnp.log(f)
        return dict(tok=tok, qf=q * jax.nn.sigmoid(q), k=1.0 - f, logf=logf,
                    scans=_segmented_scans(logf, mats_ref))

    def intra_chunk(c):
        qf, k, logf, scans = c["qf"], c["k"], c["logf"], c["scans"]
        qb, kb = heads(qf.astype(BF16)), heads(k.astype(BF16))
        a = [jnp.where(t_idx == s_idx,
                       lax.dot_general(qb[h], kb[h], nt_dims, preferred_element_type=F32), 0.0)
             for h in range(HG_HEADS)]
        level, h_size = 0, 1
        while h_size < rows:
            pre, suf = scans[h_size]
            ql = heads((qf * jnp.exp(pre)).astype(BF16))
            kl = heads((k * jnp.exp(suf - logf)).astype(BF16))
            mask = (diff >> level) == 1
            for h in range(HG_HEADS):
                al = lax.dot_general(ql[h], kl[h], nt_dims, preferred_element_type=F32)
                a[h] = jnp.where(mask, al, a[h])
            level += 1
            h_size *= 2
        cum, rev = scans[rows]
        c.update(a=a, cum=cum,
                 q_in=heads((qf * jnp.exp(cum)).astype(BF16)),
                 k_out=heads((k * jnp.exp(rev - logf)).astype(BF16)))
        return c

    def outputs_and_state(c, states):
        tok, w = c["tok"], HG_WIDTH
        v = hg_ref[tok, 2 * w:3 * w]
        og = hg_ref[tok, 3 * w:4 * w]
        vh = heads(v.astype(BF16))
        new_states = []
        for h in range(HG_HEADS):
            lanes = slice(h * HG_DIM, (h + 1) * HG_DIM)
            state_t = states[h]
            o = jnp.dot(c["a"][h].astype(BF16), vh[h], preferred_element_type=F32)
            o = o + lax.dot_general(c["q_in"][h], state_t.astype(BF16), nt_dims,
                                    preferred_element_type=F32)
            decay = jnp.exp(c["cum"][rows - 1:rows, lanes])
            v_t = v[:, lanes].T.astype(BF16)
            new_states.append(
                state_t * decay + jnp.dot(v_t, c["k_out"][h], preferred_element_type=F32))
            ms = jnp.mean(o * o, axis=-1, keepdims=True)
            gate = og[:, lanes]
            o_ref[tok, lanes] = (o * lax.rsqrt(ms + EPS) * gain[:, lanes]
                                 * (gate * jax.nn.sigmoid(gate)))
        return new_states

    toks = [slice(j * rows, (j + 1) * rows) for j in range(hg_ref.shape[0] // rows)]
    chunks = [gates_and_scans(tok) for tok in toks]
    chunks = [intra_chunk(c) for c in chunks]
    states = [state_ref[h] for h in range(HG_HEADS)]
    for c in chunks:
        states = outputs_and_state(c, states)
    for h in range(HG_HEADS):
        state_ref[h] = states[h]


def _hgrn(hg_proj, lb, gain, batch, seq):
    n = hg_proj.shape[0]
    n_chunk = seq // HG_STEP
    mats = _scan_matrices(HG_CHUNK)
    row = lambda b, c: (b * n_chunk + c, 0)
    const = lambda b, c: (0, 0)
    return pl.pallas_call(
        _hgrn_kernel,
        grid=(batch, n_chunk),
        in_specs=[
            pl.BlockSpec((HG_STEP, 4 * HG_WIDTH), row),
            pl.BlockSpec((1, HG_WIDTH), const),
            pl.BlockSpec((1, HG_WIDTH), const),
            pl.BlockSpec(mats.shape, const),
        ],
        out_specs=pl.BlockSpec((HG_STEP, HG_WIDTH), row),
        out_shape=jax.ShapeDtypeStruct((n, HG_WIDTH), F32),
        scratch_shapes=[pltpu.VMEM((HG_HEADS, HG_DIM, HG_DIM), F32)],
        compiler_params=_cparams(2),
        name="hgrn",
    )(hg_proj, lb, gain, mats)


def _out_mlp_kernel(x_ref, yn_ref, yh_ref, gn_ref, gm_ref, wa_ref, wb_ref,
                    w1_ref, w2_ref, o_ref, *, ff_chunk):
    yn = yn_ref[...]
    ms = jnp.mean(yn * yn, axis=-1, keepdims=True)
    yn = (yn * lax.rsqrt(ms + EPS) * gn_ref[...]).astype(BF16)
    x1 = (x_ref[...]
          + jnp.dot(yn, wa_ref[...], preferred_element_type=F32)
          + jnp.dot(yh_ref[...].astype(BF16), wb_ref[...], preferred_element_type=F32))
    ms = jnp.mean(x1 * x1, axis=-1, keepdims=True)
    h = (x1 * lax.rsqrt(ms + EPS) * gm_ref[...]).astype(BF16)
    mlp = None
    for c in range(w1_ref.shape[1] // ff_chunk):
        cols = slice(c * ff_chunk, (c + 1) * ff_chunk)
        u = jnp.maximum(jnp.dot(h, w1_ref[:, cols], preferred_element_type=F32), 0.0)
        d = jnp.dot((u * u).astype(BF16), w2_ref[cols, :], preferred_element_type=F32)
        mlp = d if mlp is None else mlp + d
    o_ref[...] = x1 + mlp


def _out_mlp(x2, y_nsa, y_hg, gn, gm, wa, wb, w1, w2, tm=256, ff_chunk=1024):
    n, d = x2.shape
    row = lambda i: (i, 0)
    const = lambda i: (0, 0)

    def resident(shape):
        return pl.BlockSpec(shape, const)

    return pl.pallas_call(
        functools.partial(_out_mlp_kernel, ff_chunk=ff_chunk),
        grid=(n // tm,),
        in_specs=[
            pl.BlockSpec((tm, d), row),
            pl.BlockSpec((tm, y_nsa.shape[1]), row),
            pl.BlockSpec((tm, y_hg.shape[1]), row),
            pl.BlockSpec((1, y_nsa.shape[1]), const),
            pl.BlockSpec((1, d), const),
            resident(wa.shape),
            resident(wb.shape),
            resident(w1.shape),
            resident(w2.shape),
        ],
        out_specs=pl.BlockSpec((tm, d), row),
        out_shape=jax.ShapeDtypeStruct((n, d), F32),
        compiler_params=_cparams(1),
        name="out_mlp",
    )(x2, y_nsa, y_hg, gn, gm, wa, wb, w1, w2)


def _rope_tables(seq):
    pos = jnp.arange(seq, dtype=F32)
    inv_freq = ROPE_THETA ** (-jnp.arange(0, ROT_DIM, 2, dtype=F32) / ROT_DIM)
    ang = pos[:, None] * inv_freq[None, :]
    cos, sin = jnp.cos(ang), jnp.sin(ang)
    rest = HEAD_DIM - ROT_DIM
    cos_h = jnp.concatenate([cos, cos, jnp.ones((seq, rest), F32)], axis=-1)
    sin_h = jnp.concatenate([-sin, sin, jnp.zeros((seq, rest), F32)], axis=-1)
    return jnp.tile(cos_h, (1, 2)), jnp.tile(sin_h, (1, 2))


def _cmp_to_slc_t(n_cmp_rows, n_cmp, n_slc):
    tok = np.arange(n_cmp)[:, None] * CMP_STRIDE + np.arange(CMP_LEN)[None, :]
    frac = (tok[:, :, None] // SLC_BLOCK == np.arange(n_slc)[None, None, :]).mean(axis=1)
    out = np.zeros((SLC_LANES, n_cmp_rows), np.float32)
    out[:n_slc, :n_cmp] = frac.T
    return jnp.asarray(out, BF16)


def kernel(x, attn_norm, w_in, q_norm, k_norm, cmp_pe_k, cmp_w1_k, cmp_w2_k, cmp_pe_v,
           cmp_w1_v, cmp_w2_v, nsa_out_norm, hgrn_lb, hgrn_out_norm, w_out, mlp_norm,
           w_mlp_in, w_mlp_out):
    batch, seq, d_model = x.shape
    n = batch * seq
    n_slc = seq // SLC_BLOCK
    n_cmp_rows = seq // CMP_STRIDE
    n_cmp = (seq - CMP_LEN) // CMP_STRIDE + 1
    assert n_slc <= SLC_LANES and seq % SLC_KEYS == 0 and seq >= max(WINDOW + Q_TILE, 2 * SLC_KEYS)
    assert w_in.shape[0] == 1, "single-layer block"

    kvw = KV_WIDTH
    o_q, o_kc, o_vc, o_ks, o_vs, o_kw, o_vw, o_gate = np.cumsum(
        [0, NSA_WIDTH, kvw, kvw, kvw, kvw, kvw, kvw])
    o_hg = o_gate + NSA_HEADS * 3
    w = w_in[0]
    pad = jnp.zeros((d_model, NSA_COLS - (NSA_WIDTH + 6 * kvw + NSA_HEADS * 3)), w.dtype)
    w_nsa = jnp.concatenate(
        [w[:, o_q:o_kc], w[:, o_kc:o_vc], w[:, o_ks:o_vs], w[:, o_kw:o_vw],
         w[:, o_vc:o_ks], w[:, o_vs:o_kw], w[:, o_vw:o_gate], w[:, o_gate:o_hg], pad],
        axis=1).astype(BF16)
    w_hg = w[:, o_hg:].astype(BF16)
    x2 = x.reshape(n, d_model)
    cos_t, sin_t = _rope_tables(seq)
    qg = jnp.tile(q_norm[0], 2)[None, :]
    kg = jnp.tile(k_norm[0], 2)[None, :]
    hg_proj, qt, kc, ks_aug, kw, vc, vs_t, vw_t, gt = _inproj(
        x2, attn_norm[0][None, :], w_nsa, w_hg, cos_t, sin_t, qg, kg, batch, seq)


    blocks = lambda t: t.reshape(batch * NSA_GROUPS, n_cmp_rows, CMP_STRIDE * HEAD_DIM)
    pe8 = lambda pe: jnp.broadcast_to(pe.reshape(1, CMP_LEN * HEAD_DIM), (8, CMP_LEN * HEAD_DIM)).astype(BF16)
    k_cmp, v_cmp_t = _compress(
        blocks(kc), blocks(vc), pe8(cmp_pe_k[0]), pe8(cmp_pe_v[0]),
        cmp_w1_k[0].astype(BF16), cmp_w1_v[0].astype(BF16),
        cmp_w2_k[0].astype(BF16), cmp_w2_v[0].astype(BF16))
    k_cmp = k_cmp.reshape(batch, NSA_GROUPS, n_cmp_rows, HEAD_DIM)
    v_cmp_t = v_cmp_t.reshape(batch, NSA_GROUPS, V_ROWS, n_cmp_rows)
    y_nsa = _nsa_attn(qt, gt, k_cmp, v_cmp_t, _cmp_to_slc_t(n_cmp_rows, n_cmp, n_slc),
                      ks_aug, vs_t, kw, vw_t)

    lb_all = jnp.cumsum(jax.nn.softmax(hgrn_lb.astype(F32), axis=0), axis=0)
    y_hg = _hgrn(hg_proj, lb_all[0][None, :], jnp.tile(hgrn_out_norm[0], HG_HEADS)[None, :],
                 batch, seq)

    wo = w_out[0].astype(BF16)
    out = _out_mlp(x2, y_nsa, y_hg, nsa_out_norm[0][None, :], mlp_norm[0][None, :],
                   wo[:NSA_WIDTH], wo[NSA_WIDTH:], w_mlp_in[0].astype(BF16),
                   w_mlp_out[0].astype(BF16))
    return out.reshape(batch, seq, d_model)
```

```python
import functools

import jax
import jax.numpy as jnp
import numpy as np
from jax import lax
from jax.experimental import pallas as pl
from jax.experimental.pallas import tpu as pltpu

F32 = jnp.float32
BF16 = jnp.bfloat16

NSA_HEADS = 8
NSA_GROUPS = 2
HEADS_PER_GROUP = NSA_HEADS // NSA_GROUPS
HEAD_DIM = 64
NSA_WIDTH = NSA_HEADS * HEAD_DIM
KV_WIDTH = NSA_GROUPS * HEAD_DIM
CMP_LEN = 32
CMP_STRIDE = 16
CMP_HIDDEN = 2 * HEAD_DIM
SLC_BLOCK = 64
SLC_SHIFT = 6
SLC_TOPN = 16
WINDOW = 512
ROT_DIM = HEAD_DIM // 4
ROPE_THETA = 500000.0
HG_HEADS = 4
HG_DIM = 128
HG_WIDTH = HG_HEADS * HG_DIM
EPS = 1e-6

LANES = 128
Q_TILE = 512
Q_COLS = HEADS_PER_GROUP * Q_TILE
SLC_KEYS = 512
SLC_LANES = 128
MXU_COLS = 256
V_ROWS = 80
GATE_ROWS = 16
NSA_COLS = 1408
NEG = -1e30
Q_SCALE = HEAD_DIM ** -0.5 * 1.4426950408889634
BIG = 2.0 ** 100
HG_CHUNK = 128
HG_STEP = 512
VMEM_LIMIT = 56 * 1024 * 1024


def _cparams(n_grid):
    return pltpu.CompilerParams(
        dimension_semantics=("arbitrary",) * n_grid, vmem_limit_bytes=VMEM_LIMIT)


def _value_rows_tail(cols):
    r = lax.broadcasted_iota(jnp.int32, (V_ROWS - HEAD_DIM, cols), 0)
    return jnp.where(r == 0, 1.0, 0.0).astype(BF16)


def _inproj_kernel(x_ref, g_ref, wn_ref, wh_ref, cos_ref, sin_ref, qg_ref, kg_ref,
                   hg_ref, qt_ref, kc_ref, ks_ref, kw_ref, vc_ref, vst_ref, vwt_ref, gt_ref):
    tq = x_ref.shape[0]
    x = x_ref[...]
    ms = jnp.mean(x * x, axis=-1, keepdims=True)
    h = (x * lax.rsqrt(ms + EPS) * g_ref[...]).astype(BF16)
    p = jnp.dot(h, wn_ref[...], preferred_element_type=F32)
    hg_ref[...] = jnp.dot(h, wh_ref[...], preferred_element_type=F32)
    cos = cos_ref[...]
    sin = sin_ref[...]
    lane = lax.broadcasted_iota(jnp.int32, (tq, LANES), 1)
    low_head = lane < HEAD_DIM
    dim = lane & (HEAD_DIM - 1)
    first_half = dim < ROT_DIM // 2

    def norm_rope(xs, gain, scale):
        x2 = xs * xs
        s_lo = jnp.sum(jnp.where(low_head, x2, 0.0), axis=-1, keepdims=True)
        s_hi = jnp.sum(jnp.where(low_head, 0.0, x2), axis=-1, keepdims=True)
        ms = jnp.where(low_head, s_lo, s_hi) * (1.0 / HEAD_DIM)
        y = xs * lax.rsqrt(ms + EPS) * gain
        partner = jnp.where(first_half,
                            pltpu.roll(y, LANES - ROT_DIM // 2, 1),
                            pltpu.roll(y, ROT_DIM // 2, 1))
        return (y * cos + partner * sin) * scale

    def split_groups(ref, x, lanes=slice(None)):
        for g in range(NSA_GROUPS):
            ref[g, :, lanes] = x[:, g * HEAD_DIM:(g + 1) * HEAD_DIM].astype(BF16)

    def split_groups_t(ref, x):
        xt = x.T
        for g in range(NSA_GROUPS):
            ref[g, 0:HEAD_DIM, :] = xt[g * HEAD_DIM:(g + 1) * HEAD_DIM, :].astype(BF16)
            ref[g, HEAD_DIM:V_ROWS, :] = _value_rows_tail(tq)

    qg = qg_ref[...]
    kg = kg_ref[...]
    for s in range(NSA_WIDTH // LANES):
        yt = norm_rope(p[:, s * LANES:(s + 1) * LANES], qg, Q_SCALE).T
        g, h0 = divmod(2 * s, HEADS_PER_GROUP)
        for j in range(2):
            qt_ref[g, :, (h0 + j) * tq:(h0 + j + 1) * tq] = (
                yt[j * HEAD_DIM:(j + 1) * HEAD_DIM, :].astype(BF16))
    base = NSA_WIDTH
    split_groups(kc_ref, norm_rope(p[:, base:base + LANES], kg, 1.0))
    tok = pl.program_id(1) * tq + lax.broadcasted_iota(jnp.int32, (tq, SLC_LANES), 0)
    blk = lax.broadcasted_iota(jnp.int32, (tq, SLC_LANES), 1)
    one_hot = jnp.where((tok >> SLC_SHIFT) == blk, 1.0, 0.0).astype(BF16)
    for g in range(NSA_GROUPS):
        ks_ref[g, :, 0:SLC_LANES] = one_hot
    split_groups(ks_ref, norm_rope(p[:, base + LANES:base + 2 * LANES], kg, 1.0),
                 slice(SLC_LANES, SLC_LANES + HEAD_DIM))
    split_groups(kw_ref, norm_rope(p[:, base + 2 * LANES:base + 3 * LANES], kg, 1.0))
    base += 3 * LANES
    split_groups(vc_ref, p[:, base:base + LANES])
    split_groups_t(vst_ref, p[:, base + LANES:base + 2 * LANES])
    split_groups_t(vwt_ref, p[:, base + 2 * LANES:base + 3 * LANES])
    base += 3 * LANES
    gates_t = jax.nn.sigmoid(p[:, base:base + LANES]).T
    per_group = HEADS_PER_GROUP * 3
    for g in range(NSA_GROUPS):
        gt_ref[g, 0:per_group, :] = gates_t[g * per_group:(g + 1) * per_group, :]
        gt_ref[g, per_group:GATE_ROWS, :] = jnp.zeros((GATE_ROWS - per_group, tq), F32)


def _inproj(x2, gain, w_nsa, w_hg, cos_t, sin_t, qg, kg, batch, seq):
    tq = Q_TILE
    nt = seq // tq
    d = x2.shape[1]
    tok = lambda b, i: (b * nt + i, 0)
    const = lambda b, i: (0, 0)
    rows = lambda b, i: (b, 0, i, 0)
    cols = lambda b, i: (b, 0, 0, i)
    grp = NSA_GROUPS
    return pl.pallas_call(
        _inproj_kernel,
        grid=(batch, nt),
        in_specs=[
            pl.BlockSpec((tq, d), tok),
            pl.BlockSpec((1, d), const),
            pl.BlockSpec(w_nsa.shape, const),
            pl.BlockSpec(w_hg.shape, const),
            pl.BlockSpec((tq, LANES), lambda b, i: (i, 0)),
            pl.BlockSpec((tq, LANES), lambda b, i: (i, 0)),
            pl.BlockSpec((1, LANES), const),
            pl.BlockSpec((1, LANES), const),
        ],
        out_specs=[
            pl.BlockSpec((tq, w_hg.shape[1]), tok),
            pl.BlockSpec((None, grp, None, HEAD_DIM, Q_COLS), lambda b, i: (b, 0, i, 0, 0)),
            pl.BlockSpec((None, grp, tq, HEAD_DIM), rows),
            pl.BlockSpec((None, grp, tq, SLC_LANES + HEAD_DIM), rows),
            pl.BlockSpec((None, grp, tq, HEAD_DIM), rows),
            pl.BlockSpec((None, grp, tq, HEAD_DIM), rows),
            pl.BlockSpec((None, grp, V_ROWS, tq), cols),
            pl.BlockSpec((None, grp, V_ROWS, tq), cols),
            pl.BlockSpec((None, grp, GATE_ROWS, tq), cols),
        ],
        out_shape=[
            jax.ShapeDtypeStruct((batch * seq, w_hg.shape[1]), F32),
            jax.ShapeDtypeStruct((batch, grp, nt, HEAD_DIM, Q_COLS), BF16),
            jax.ShapeDtypeStruct((batch, grp, seq, HEAD_DIM), BF16),
            jax.ShapeDtypeStruct((batch, grp, seq, SLC_LANES + HEAD_DIM), BF16),
            jax.ShapeDtypeStruct((batch, grp, seq, HEAD_DIM), BF16),
            jax.ShapeDtypeStruct((batch, grp, seq, HEAD_DIM), BF16),
            jax.ShapeDtypeStruct((batch, grp, V_ROWS, seq), BF16),
            jax.ShapeDtypeStruct((batch, grp, V_ROWS, seq), BF16),
            jax.ShapeDtypeStruct((batch, grp, GATE_ROWS, seq), F32),
        ],
        compiler_params=_cparams(2),
        name="inproj",
    )(x2, gain, w_nsa, w_hg, cos_t, sin_t, qg, kg)


def _compress_kernel(tk_ref, tv_ref, pek_ref, pev_ref, w1k_ref, w1v_ref,
                     w2k_ref, w2v_ref, ok_ref, ov_ref):
    half = CMP_STRIDE * HEAD_DIM

    def run(t_ref, pe_ref, w1_ref, w2_ref, o_ref, transposed):
        t = t_ref[...]
        n_rows = t.shape[0]
        top = jnp.dot(t, w1_ref[0:half, :], preferred_element_type=F32)
        bot = jnp.dot(t, w1_ref[half:2 * half, :], preferred_element_type=F32)
        bot_next = pltpu.roll(bot, n_rows - 1, 0)
        pe = jnp.dot(pe_ref[...], w1_ref[...], preferred_element_type=F32)[0:1, :]
        h = top + bot_next + pe
        h = h * jax.nn.sigmoid(h)
        o = jnp.dot(h.astype(BF16), w2_ref[...], preferred_element_type=F32)
        r = lax.broadcasted_iota(jnp.int32, o.shape, 0)
        o = jnp.where(r < n_rows - 1, o, 0.0)
        if transposed:
            ot = jnp.concatenate([o, jnp.zeros_like(o)], axis=1).T
            o_ref[0:HEAD_DIM, :] = ot[0:HEAD_DIM, :].astype(BF16)
            o_ref[HEAD_DIM:V_ROWS, :] = _value_rows_tail(n_rows)
        else:
            o_ref[...] = o.astype(BF16)

    run(tk_ref, pek_ref, w1k_ref, w2k_ref, ok_ref, False)
    run(tv_ref, pev_ref, w1v_ref, w2v_ref, ov_ref, True)


def _compress(tk, tv, pek, pev, w1k, w1v, w2k, w2v):
    bg, n_rows, width = tk.shape
    blk = lambda i: (i, 0, 0)
    const = lambda i: (0, 0)
    return pl.pallas_call(
        _compress_kernel,
        grid=(bg,),
        in_specs=[
            pl.BlockSpec((None, n_rows, width), blk),
            pl.BlockSpec((None, n_rows, width), blk),
            pl.BlockSpec(pek.shape, const),
            pl.BlockSpec(pev.shape, const),
            pl.BlockSpec(w1k.shape, const),
            pl.BlockSpec(w1v.shape, const),
            pl.BlockSpec(w2k.shape, const),
            pl.BlockSpec(w2v.shape, const),
        ],
        out_specs=[pl.BlockSpec((None, n_rows, HEAD_DIM), blk),
                   pl.BlockSpec((None, V_ROWS, n_rows), blk)],
        out_shape=[jax.ShapeDtypeStruct((bg, n_rows, HEAD_DIM), BF16),
                   jax.ShapeDtypeStruct((bg, V_ROWS, n_rows), BF16)],
        compiler_params=_cparams(1),
        name="compress",
    )(tk, tv, pek, pev, w1k, w1v, w2k, w2v)


def _nsa_attn_kernel(qt_ref, gt_ref, kc_ref, vct_ref, mapt_ref, ks_ref, vst_ref,
                     kw_ref, vwt_ref, y_ref, qaug_ref, s0_ref, s1_ref, c0_ref, c1_ref,
                     m_ref, acc_ref, ocmp_ref, owin_ref):
    i = pl.program_id(2)
    q0 = i * Q_TILE
    qt = qt_ref[...]
    col = lax.broadcasted_iota(jnp.int32, (1, Q_COLS), 1)
    tpos = q0 + (col & (Q_TILE - 1))

    def compressed_branch(n_rows, n_blk):
        kc = kc_ref[0:n_rows, :]
        cmp_end = lax.broadcasted_iota(jnp.int32, (n_rows, 1), 0) * CMP_STRIDE + (CMP_LEN - 1)
        mapt = mapt_ref[0:n_blk, 0:n_rows]
        head_cols = [slice(h * Q_TILE, (h + 1) * Q_TILE) for h in range(HEADS_PER_GROUP)]
        scores = [jnp.dot(kc, qt[:, cols], preferred_element_type=F32) for cols in head_cols]
        imp = None
        for cols, s in zip(head_cols, scores):
            s = jnp.where(cmp_end <= tpos[:, cols], s, NEG)
            m = jnp.max(s, axis=0, keepdims=True)
            p = jnp.exp2(s - jnp.maximum(m, 0.1 * NEG)).astype(BF16)
            acc = jnp.dot(vct_ref[:, 0:n_rows], p, preferred_element_type=F32)
            l = acc[HEAD_DIM:HEAD_DIM + 1, :]
            inv = jnp.where(l > 0.0, 1.0 / l, 0.0)
            ocmp_ref[:, cols] = acc[0:HEAD_DIM, :] * inv
            part = jnp.dot(mapt, p, preferred_element_type=F32) * inv
            imp = part if imp is None else imp + part
        return imp

    def select_blocks(imp, n_blk):
        blk_i = lax.broadcasted_iota(jnp.int32, (n_blk, Q_TILE), 0)
        cur = (q0 + lax.broadcasted_iota(jnp.int32, (n_blk, Q_TILE), 1)) >> SLC_SHIFT
        blk = blk_i.astype(F32)
        sel = jnp.zeros((n_blk, Q_TILE), F32)
        for forced in (blk_i == 0, blk_i == cur, blk_i == cur - 1):
            sel = jnp.where(forced, 1.0, sel)
        val = jnp.where(blk_i <= cur, jnp.where(sel > 0.5, -1.0, imp), -1.0)
        for _ in range(min(SLC_TOPN, n_blk) - 3):
            mx = jnp.max(val, axis=0, keepdims=True)
            first = jnp.min(jnp.where(val == mx, blk, float(n_blk)), axis=0, keepdims=True)
            first = jnp.where(mx > -0.5, first, -1.0)
            pick = blk == first
            sel = jnp.where(pick, 1.0, sel)
            val = jnp.where(pick, -1.0, val)
        sel_bias = jnp.where(sel > 0.5, 0.0, -BIG).astype(BF16)
        for h in range(HEADS_PER_GROUP):
            qaug_ref[0:n_blk, h * Q_TILE:(h + 1) * Q_TILE] = sel_bias
        if n_blk < SLC_LANES:
            qaug_ref[n_blk:SLC_LANES, :] = jnp.full((SLC_LANES - n_blk, Q_COLS), -BIG, BF16)

    col_tiles = [slice(c0, c0 + MXU_COLS) for c0 in range(0, Q_COLS, MXU_COLS)]
    buffers = ((s0_ref, c0_ref), (s1_ref, c1_ref))

    def slc_scores(k0, s_ref, cmax_ref, cols):
        sc = jnp.dot(ks_ref[pl.ds(k0, SLC_KEYS), :], qaug_ref[:, cols],
                     preferred_element_type=F32)
        s_ref[:, cols] = sc
        cmax_ref[:, cols] = jnp.max(sc, axis=0, keepdims=True)

    span = WINDOW + Q_TILE
    t_loc = col & (Q_TILE - 1)

    def window_scores(interior):
        if not interior:
            sw = jnp.dot(kw_ref[0:span, :], qt, preferred_element_type=F32)
            kpos = lax.broadcasted_iota(jnp.int32, (span, 1), 0)
            sw = jnp.where(kpos <= tpos, jnp.where(kpos > tpos - WINDOW, sw, NEG), NEG)
            return [sw], 0
        start = pl.multiple_of(q0 - WINDOW, Q_TILE)
        i_loc = lax.broadcasted_iota(jnp.int32, (Q_TILE, 1), 0)
        n_parts = span // Q_TILE
        parts = []
        for r in range(n_parts):
            sw = jnp.dot(kw_ref[pl.ds(start + r * Q_TILE, Q_TILE), :], qt,
                         preferred_element_type=F32)
            if r == 0:
                sw = jnp.where(i_loc > t_loc, sw, NEG)
            elif r == n_parts - 1:
                sw = jnp.where(i_loc <= t_loc, sw, NEG)
            parts.append(sw)
        return parts, start

    def window_finish(parts, v_start):
        mw = None
        for sw in parts:
            cm = jnp.max(sw, axis=0, keepdims=True)
            mw = cm if mw is None else jnp.maximum(mw, cm)
        accw, off = None, 0
        for sw in parts:
            pw = jnp.exp2(sw - mw).astype(BF16)
            d = jnp.dot(vwt_ref[:, pl.ds(v_start + off, sw.shape[0])], pw,
                        preferred_element_type=F32)
            accw = d if accw is None else accw + d
            off += sw.shape[0]
        owin_ref[...] = accw[0:HEAD_DIM, :] / accw[HEAD_DIM:HEAD_DIM + 1, :]

    def front(n_rows, n_blk, interior):
        imp = compressed_branch(n_rows, n_blk)
        window_finish(*window_scores(interior))
        select_blocks(imp, n_blk)
        qaug_ref[SLC_LANES:SLC_LANES + HEAD_DIM, :] = qt
        for cols in col_tiles:
            slc_scores(0, *buffers[0], cols)
        for cols in col_tiles[:-1]:
            slc_scores(SLC_KEYS, *buffers[1], cols)

    n_cmp_rows = kc_ref.shape[0]
    row_steps = list(range(LANES, n_cmp_rows, LANES)) + [n_cmp_rows]
    q_end = q0 + Q_TILE
    lo = 0
    for n_rows in row_steps:
        hi = n_rows * CMP_STRIDE
        n_blk = min(SLC_LANES, -(-(hi // SLC_BLOCK) // 16) * 16)
        cuts = [lo, WINDOW, hi] if lo < WINDOW < hi else [lo, hi]
        for c_lo, c_hi in zip(cuts[:-1], cuts[1:]):
            pl.when(jnp.logical_and(q_end > c_lo, q_end <= c_hi))(
                functools.partial(front, n_rows, n_blk, c_lo >= WINDOW))
        lo = hi

    def slc_update(k0, s_ref, cmax_ref, cols, masked):
        sc = s_ref[:, cols]
        if masked:
            kpos = k0 + lax.broadcasted_iota(jnp.int32, (SLC_KEYS, 1), 0)
            sc = jnp.where(kpos <= tpos[:, cols], sc, -BIG)
            cmax = jnp.max(sc, axis=0, keepdims=True)
        else:
            cmax = cmax_ref[:, cols]
        m_prev = m_ref[:, cols]
        m_new = jnp.maximum(m_prev, cmax)
        pr = jnp.exp2(sc - m_new).astype(BF16)
        pv = jnp.dot(vst_ref[:, pl.ds(k0, SLC_KEYS)], pr, preferred_element_type=F32)
        acc_ref[:, cols] = acc_ref[:, cols] * jnp.exp2(m_prev - m_new) + pv
        m_ref[:, cols] = m_new

    n_full = q0 // SLC_KEYS
    last_tile = ks_ref.shape[0] // SLC_KEYS - 1

    def tile_start(t):
        return pl.multiple_of(jnp.minimum(t, last_tile) * SLC_KEYS, SLC_KEYS)

    even, odd = buffers

    def slc_body(j, carry):
        k_even, k_odd = tile_start(2 * j), tile_start(2 * j + 1)
        k_even_next, k_odd_next = tile_start(2 * j + 2), tile_start(2 * j + 3)
        slc_scores(k_odd, *odd, col_tiles[-1])
        for cols in col_tiles:
            slc_update(k_even, *even, cols, masked=False)
            slc_scores(k_even_next, *even, cols)
        for cols in col_tiles[:-1]:
            slc_update(k_odd, *odd, cols, masked=False)
            slc_scores(k_odd_next, *odd, cols)
        slc_update(k_odd, *odd, col_tiles[-1], masked=False)
        return carry

    m_ref[...] = jnp.full(m_ref.shape, NEG, F32)
    acc_ref[...] = jnp.zeros(acc_ref.shape, F32)
    lax.fori_loop(0, n_full // 2, slc_body, 0)
    k_last = tile_start(n_full)

    @pl.when((n_full & 1) == 0)
    def _():
        for cols in col_tiles:
            slc_update(k_last, *even, cols, masked=True)

    @pl.when((n_full & 1) == 1)
    def _():
        slc_scores(k_last, *odd, col_tiles[-1])
        for cols in col_tiles:
            slc_update(tile_start(n_full - 1), *even, cols, masked=False)
        for cols in col_tiles:
            slc_update(k_last, *odd, cols, masked=True)

    o_slc = acc_ref[0:HEAD_DIM, :] / acc_ref[HEAD_DIM:HEAD_DIM + 1, :]

    g = gt_ref[...]

    def gate_row(branch):
        return jnp.concatenate(
            [g[h * 3 + branch:h * 3 + branch + 1, :] for h in range(HEADS_PER_GROUP)], axis=1)

    y = (gate_row(0) * ocmp_ref[...] + gate_row(1) * o_slc
         + gate_row(2) * owin_ref[...])
    for pair in range(HEADS_PER_GROUP // 2):
        lo, mid, hi = (2 * pair) * Q_TILE, (2 * pair + 1) * Q_TILE, (2 * pair + 2) * Q_TILE
        stacked = jnp.concatenate([y[:, lo:mid], y[:, mid:hi]], axis=0)
        y_ref[:, pair * LANES:(pair + 1) * LANES] = stacked.T


def _nsa_attn(qt, gt, kc, vct, mapt, ks, vst, kw, vwt):
    b, g, nt = qt.shape[:3]
    seq = ks.shape[2]
    n_cmp_rows = kc.shape[2]
    per_bg = lambda bi, gi, i: (bi, gi, 0, 0)
    return pl.pallas_call(
        _nsa_attn_kernel,
        grid=(b, g, nt),
        in_specs=[
            pl.BlockSpec((None, None, None, HEAD_DIM, Q_COLS), lambda bi, gi, i: (bi, gi, i, 0, 0)),
            pl.BlockSpec((None, None, GATE_ROWS, Q_TILE), lambda bi, gi, i: (bi, gi, 0, i)),
            pl.BlockSpec((None, None, n_cmp_rows, HEAD_DIM), per_bg),
            pl.BlockSpec((None, None, V_ROWS, n_cmp_rows), per_bg),
            pl.BlockSpec(mapt.shape, lambda bi, gi, i: (0, 0)),
            pl.BlockSpec((None, None, seq, HEAD_DIM + SLC_LANES), per_bg),
            pl.BlockSpec((None, None, V_ROWS, seq), per_bg),
            pl.BlockSpec((None, None, seq, HEAD_DIM), per_bg),
            pl.BlockSpec((None, None, V_ROWS, seq), per_bg),
        ],
        out_specs=pl.BlockSpec((Q_TILE, HEADS_PER_GROUP * HEAD_DIM),
                               lambda bi, gi, i: (bi * nt + i, gi)),
        out_shape=jax.ShapeDtypeStruct((b * seq, NSA_WIDTH), F32),
        scratch_shapes=[
            pltpu.VMEM((SLC_LANES + HEAD_DIM, Q_COLS), BF16),
            pltpu.VMEM((SLC_KEYS, Q_COLS), F32),
            pltpu.VMEM((SLC_KEYS, Q_COLS), F32),
            pltpu.VMEM((1, Q_COLS), F32),
            pltpu.VMEM((1, Q_COLS), F32),
            pltpu.VMEM((1, Q_COLS), F32),
            pltpu.VMEM((V_ROWS, Q_COLS), F32),
            pltpu.VMEM((HEAD_DIM, Q_COLS), F32),
            pltpu.VMEM((HEAD_DIM, Q_COLS), F32),
        ],
        compiler_params=_cparams(3),
        name="nsa_attn",
    )(qt, gt, kc, vct, mapt, ks, vst, kw, vwt)


SCAN_MXU_SIZES = (2, 4, 8)


def _scan_matrices(rows):
    t = np.arange(rows)[:, None]
    s = np.arange(rows)[None, :]
    mats = [(s // h == t // h) & (s <= t) for h in SCAN_MXU_SIZES]
    mats += [(s // h == t // h) & (s >= t) for h in SCAN_MXU_SIZES]
    return jnp.asarray(np.concatenate([np.tile(m, (1, 3)) for m in mats], axis=0), BF16)


def _segmented_scans(g, mats_ref):
    rows = g.shape[0]
    hi = g.astype(BF16)
    rest = g - hi.astype(F32)
    mid = rest.astype(BF16)
    lo = (rest - mid.astype(F32)).astype(BF16)
    small = jnp.dot(mats_ref[...], jnp.concatenate([hi, mid, lo], axis=0),
                    preferred_element_type=F32)
    n = len(SCAN_MXU_SIZES)
    out = {1: (g, g)}
    for j, h in enumerate(SCAN_MXU_SIZES):
        out[h] = (small[j * rows:(j + 1) * rows], small[(n + j) * rows:(n + j + 1) * rows])
    h = SCAN_MXU_SIZES[-1]
    pre, suf = out[h]
    while h < rows:
        pre_parts, suf_parts = [], []
        for b in range(rows // (2 * h)):
            lo_r, mid_r, hi_r = b * 2 * h, b * 2 * h + h, (b + 1) * 2 * h
            pre_parts += [pre[lo_r:mid_r], pre[mid_r:hi_r] + pre[mid_r - 1:mid_r, :]]
            suf_parts += [suf[lo_r:mid_r] + suf[mid_r:mid_r + 1, :], suf[mid_r:hi_r]]
        pre = jnp.concatenate(pre_parts, axis=0)
        suf = jnp.concatenate(suf_parts, axis=0)
        h *= 2
        out[h] = (pre, suf)
    return out


def _hgrn_kernel(hg_ref, lb_ref, gain_ref, mats_ref, o_ref, state_ref):
    c = pl.program_id(1)

    @pl.when(c == 0)
    def _():
        state_ref[...] = jnp.zeros_like(state_ref)

    rows = HG_CHUNK
    t_idx = lax.broadcasted_iota(jnp.int32, (rows, rows), 0)
    s_idx = lax.broadcasted_iota(jnp.int32, (rows, rows), 1)
    diff = jnp.where(t_idx > s_idx, t_idx ^ s_idx, 0)
    nt_dims = (((1,), (1,)), ((), ()))
    lb = lb_ref[...]
    gain = gain_ref[...]

    def heads(x):
        return [x[:, h * HG_DIM:(h + 1) * HG_DIM] for h in range(HG_HEADS)]

    def gates_and_scans(tok):
        w = HG_WIDTH
        q = hg_ref[tok, 0:w]
        f = lb + (1.0 - lb) * jax.nn.sigmoid(hg_ref[tok, w:2 * w])
        logf = jnp.log(f)
        return dict(tok=tok, qf=q * jax.nn.sigmoid(q), k=1.0 - f, logf=logf,
                    scans=_segmented_scans(logf, mats_ref))

    def intra_chunk(c):
        qf, k, logf, scans = c["qf"], c["k"], c["logf"], c["scans"]
        qb, kb = heads(qf.astype(BF16)), heads(k.astype(BF16))
        a = [jnp.where(t_idx == s_idx,
                       lax.dot_general(qb[h], kb[h], nt_dims, preferred_element_type=F32), 0.0)
             for h in range(HG_HEADS)]
        level, h_size = 0, 1
        while h_size < rows:
            pre, suf = scans[h_size]
            ql = heads((qf * jnp.exp(pre)).astype(BF16))
            kl = heads((k * jnp.exp(suf - logf)).astype(BF16))
            mask = (diff >> level) == 1
            for h in range(HG_HEADS):
                al = lax.dot_general(ql[h], kl[h], nt_dims, preferred_element_type=F32)
                a[h] = jnp.where(mask, al, a[h])
            level += 1
            h_size *= 2
        cum, rev = scans[rows]
        c.update(a=a, cum=cum,
                 q_in=heads((qf * jnp.exp(cum)).astype(BF16)),
                 k_out=heads((k * jnp.exp(rev - logf)).astype(BF16)))
        return c

    def outputs_and_state(c, states):
        tok, w = c["tok"], HG_WIDTH
        v = hg_ref[tok, 2 * w:3 * w]
        og = hg_ref[tok, 3 * w:4 * w]
        vh = heads(v.astype(BF16))
        new_states = []
        for h in range(HG_HEADS):
            lanes = slice(h * HG_DIM, (h + 1) * HG_DIM)
            state_t = states[h]
            o = jnp.dot(c["a"][h].astype(BF16), vh[h], preferred_element_type=F32)
            o = o + lax.dot_general(c["q_in"][h], state_t.astype(BF16), nt_dims,
                                    preferred_element_type=F32)
            decay = jnp.exp(c["cum"][rows - 1:rows, lanes])
            v_t = v[:, lanes].T.astype(BF16)
            new_states.append(
                state_t * decay + jnp.dot(v_t, c["k_out"][h], preferred_element_type=F32))
            ms = jnp.mean(o * o, axis=-1, keepdims=True)
            gate = og[:, lanes]
            o_ref[tok, lanes] = (o * lax.rsqrt(ms + EPS) * gain[:, lanes]
                                 * (gate * jax.nn.sigmoid(gate)))
        return new_states

    toks = [slice(j * rows, (j + 1) * rows) for j in range(hg_ref.shape[0] // rows)]
    chunks = [gates_and_scans(tok) for tok in toks]
    chunks = [intra_chunk(c) for c in chunks]
    states = [state_ref[h] for h in range(HG_HEADS)]
    for c in chunks:
        states = outputs_and_state(c, states)
    for h in range(HG_HEADS):
        state_ref[h] = states[h]


def _hgrn(hg_proj, lb, gain, batch, seq):
    n = hg_proj.shape[0]
    n_chunk = seq // HG_STEP
    mats = _scan_matrices(HG_CHUNK)
    row = lambda b, c: (b * n_chunk + c, 0)
    const = lambda b, c: (0, 0)
    return pl.pallas_call(
        _hgrn_kernel,
        grid=(batch, n_chunk),
        in_specs=[
            pl.BlockSpec((HG_STEP, 4 * HG_WIDTH), row),
            pl.BlockSpec((1, HG_WIDTH), const),
            pl.BlockSpec((1, HG_WIDTH), const),
            pl.BlockSpec(mats.shape, const),
        ],
        out_specs=pl.BlockSpec((HG_STEP, HG_WIDTH), row),
        out_shape=jax.ShapeDtypeStruct((n, HG_WIDTH), F32),
        scratch_shapes=[pltpu.VMEM((HG_HEADS, HG_DIM, HG_DIM), F32)],
        compiler_params=_cparams(2),
        name="hgrn",
    )(hg_proj, lb, gain, mats)


def _out_mlp_kernel(x_ref, yn_ref, yh_ref, gn_ref, gm_ref, wa_ref, wb_ref,
                    w1_ref, w2_ref, o_ref, *, ff_chunk):
    yn = yn_ref[...]
    ms = jnp.mean(yn * yn, axis=-1, keepdims=True)
    yn = (yn * lax.rsqrt(ms + EPS) * gn_ref[...]).astype(BF16)
    x1 = (x_ref[...]
          + jnp.dot(yn, wa_ref[...], preferred_element_type=F32)
          + jnp.dot(yh_ref[...].astype(BF16), wb_ref[...], preferred_element_type=F32))
    ms = jnp.mean(x1 * x1, axis=-1, keepdims=True)
    h = (x1 * lax.rsqrt(ms + EPS) * gm_ref[...]).astype(BF16)
    mlp = None
    for c in range(w1_ref.shape[1] // ff_chunk):
        cols = slice(c * ff_chunk, (c + 1) * ff_chunk)
        u = jnp.maximum(jnp.dot(h, w1_ref[:, cols], preferred_element_type=F32), 0.0)
        d = jnp.dot((u * u).astype(BF16), w2_ref[cols, :], preferred_element_type=F32)
        mlp = d if mlp is None else mlp + d
    o_ref[...] = x1 + mlp


def _out_mlp(x2, y_nsa, y_hg, gn, gm, wa, wb, w1, w2, tm=256, ff_chunk=1024):
    n, d = x2.shape
    row = lambda i: (i, 0)
    const = lambda i: (0, 0)

    def resident(shape):
        return pl.BlockSpec(shape, const)

    return pl.pallas_call(
        functools.partial(_out_mlp_kernel, ff_chunk=ff_chunk),
        grid=(n // tm,),
        in_specs=[
            pl.BlockSpec((tm, d), row),
            pl.BlockSpec((tm, y_nsa.shape[1]), row),
            pl.BlockSpec((tm, y_hg.shape[1]), row),
            pl.BlockSpec((1, y_nsa.shape[1]), const),
            pl.BlockSpec((1, d), const),
            resident(wa.shape),
            resident(wb.shape),
            resident(w1.shape),
            resident(w2.shape),
        ],
        out_specs=pl.BlockSpec((tm, d), row),
        out_shape=jax.ShapeDtypeStruct((n, d), F32),
        compiler_params=_cparams(1),
        name="out_mlp",
    )(x2, y_nsa, y_hg, gn, gm, wa, wb, w1, w2)


def _rope_tables(seq):
    pos = jnp.arange(seq, dtype=F32)
    inv_freq = ROPE_THETA ** (-jnp.arange(0, ROT_DIM, 2, dtype=F32) / ROT_DIM)
    ang = pos[:, None] * inv_freq[None, :]
    cos, sin = jnp.cos(ang), jnp.sin(ang)
    rest = HEAD_DIM - ROT_DIM
    cos_h = jnp.concatenate([cos, cos, jnp.ones((seq, rest), F32)], axis=-1)
    sin_h = jnp.concatenate([-sin, sin, jnp.zeros((seq, rest), F32)], axis=-1)
    return jnp.tile(cos_h, (1, 2)), jnp.tile(sin_h, (1, 2))


def _cmp_to_slc_t(n_cmp_rows, n_cmp, n_slc):
    tok = np.arange(n_cmp)[:, None] * CMP_STRIDE + np.arange(CMP_LEN)[None, :]
    frac = (tok[:, :, None] // SLC_BLOCK == np.arange(n_slc)[None, None, :]).mean(axis=1)
    out = np.zeros((SLC_LANES, n_cmp_rows), np.float32)
    out[:n_slc, :n_cmp] = frac.T
    return jnp.asarray(out, BF16)


def kernel(x, attn_norm, w_in, q_norm, k_norm, cmp_pe_k, cmp_w1_k, cmp_w2_k, cmp_pe_v,
           cmp_w1_v, cmp_w2_v, nsa_out_norm, hgrn_lb, hgrn_out_norm, w_out, mlp_norm,
           w_mlp_in, w_mlp_out):
    batch, seq, d_model = x.shape
    n = batch * seq
    n_slc = seq // SLC_BLOCK
    n_cmp_rows = seq // CMP_STRIDE
    n_cmp = (seq - CMP_LEN) // CMP_STRIDE + 1
    assert n_slc <= SLC_LANES and seq % SLC_KEYS == 0 and seq >= max(WINDOW + Q_TILE, 2 * SLC_KEYS)
    assert w_in.shape[0] == 1, "single-layer block"

    kvw = KV_WIDTH
    o_q, o_kc, o_vc, o_ks, o_vs, o_kw, o_vw, o_gate = np.cumsum(
        [0, NSA_WIDTH, kvw, kvw, kvw, kvw, kvw, kvw])
    o_hg = o_gate + NSA_HEADS * 3
    w = w_in[0]
    pad = jnp.zeros((d_model, NSA_COLS - (NSA_WIDTH + 6 * kvw + NSA_HEADS * 3)), w.dtype)
    w_nsa = jnp.concatenate(
        [w[:, o_q:o_kc], w[:, o_kc:o_vc], w[:, o_ks:o_vs], w[:, o_kw:o_vw],
         w[:, o_vc:o_ks], w[:, o_vs:o_kw], w[:, o_vw:o_gate], w[:, o_gate:o_hg], pad],
        axis=1).astype(BF16)
    w_hg = w[:, o_hg:].astype(BF16)
    x2 = x.reshape(n, d_model)
    cos_t, sin_t = _rope_tables(seq)
    qg = jnp.tile(q_norm[0], 2)[None, :]
    kg = jnp.tile(k_norm[0], 2)[None, :]
    hg_proj, qt, kc, ks_aug, kw, vc, vs_t, vw_t, gt = _inproj(
        x2, attn_norm[0][None, :], w_nsa, w_hg, cos_t, sin_t, qg, kg, batch, seq)


    blocks = lambda t: t.reshape(batch * NSA_GROUPS, n_cmp_rows, CMP_STRIDE * HEAD_DIM)
    pe8 = lambda pe: jnp.broadcast_to(pe.reshape(1, CMP_LEN * HEAD_DIM), (8, CMP_LEN * HEAD_DIM)).astype(BF16)
    k_cmp, v_cmp_t = _compress(
        blocks(kc), blocks(vc), pe8(cmp_pe_k[0]), pe8(cmp_pe_v[0]),
        cmp_w1_k[0].astype(BF16), cmp_w1_v[0].astype(BF16),
        cmp_w2_k[0].astype(BF16), cmp_w2_v[0].astype(BF16))
    k_cmp = k_cmp.reshape(batch, NSA_GROUPS, n_cmp_rows, HEAD_DIM)
    v_cmp_t = v_cmp_t.reshape(batch, NSA_GROUPS, V_ROWS, n_cmp_rows)
    y_nsa = _nsa_attn(qt, gt, k_cmp, v_cmp_t, _cmp_to_slc_t(n_cmp_rows, n_cmp, n_slc),
                      ks_aug, vs_t, kw, vw_t)

    lb_all = jnp.cumsum(jax.nn.softmax(hgrn_lb.astype(F32), axis=0), axis=0)
    y_hg = _hgrn(hg_proj, lb_all[0][None, :], jnp.tile(hgrn_out_norm[0], HG_HEADS)[None, :],
                 batch, seq)

    wo = w_out[0].astype(BF16)
    out = _out_mlp(x2, y_nsa, y_hg, nsa_out_norm[0][None, :], mlp_norm[0][None, :],
                   wo[:NSA_WIDTH], wo[NSA_WIDTH:], w_mlp_in[0].astype(BF16),
                   w_mlp_out[0].astype(BF16))
    return out.reshape(batch, seq, d_model)
```

```python
import functools

import jax
import jax.numpy as jnp
import numpy as np
from jax import lax
from jax.experimental import pallas as pl
from jax.experimental.pallas import tpu as pltpu

F32 = jnp.float32
BF16 = jnp.bfloat16

NSA_HEADS = 8
NSA_GROUPS = 2
HEADS_PER_GROUP = NSA_HEADS // NSA_GROUPS
HEAD_DIM = 64
NSA_WIDTH = NSA_HEADS * HEAD_DIM
KV_WIDTH = NSA_GROUPS * HEAD_DIM
CMP_LEN = 32
CMP_STRIDE = 16
CMP_HIDDEN = 2 * HEAD_DIM
SLC_BLOCK = 64
SLC_SHIFT = 6
SLC_TOPN = 16
WINDOW = 512
ROT_DIM = HEAD_DIM // 4
ROPE_THETA = 500000.0
HG_HEADS = 4
HG_DIM = 128
HG_WIDTH = HG_HEADS * HG_DIM
EPS = 1e-6

LANES = 128
Q_TILE = 512
Q_COLS = HEADS_PER_GROUP * Q_TILE
SLC_KEYS = 512
SLC_LANES = 128
MXU_COLS = 256
V_ROWS = 80
GATE_ROWS = 16
NSA_COLS = 1408
NEG = -1e30
Q_SCALE = HEAD_DIM ** -0.5 * 1.4426950408889634
BIG = 2.0 ** 100
HG_CHUNK = 128
HG_STEP = 512
VMEM_LIMIT = 56 * 1024 * 1024


def _cparams(n_grid):
    return pltpu.CompilerParams(
        dimension_semantics=("arbitrary",) * n_grid, vmem_limit_bytes=VMEM_LIMIT)


def _value_rows_tail(cols):
    r = lax.broadcasted_iota(jnp.int32, (V_ROWS - HEAD_DIM, cols), 0)
    return jnp.where(r == 0, 1.0, 0.0).astype(BF16)


def _inproj_kernel(x_ref, g_ref, wn_ref, wh_ref, cos_ref, sin_ref, qg_ref, kg_ref,
                   hg_ref, qt_ref, kc_ref, ks_ref, kw_ref, vc_ref, vst_ref, vwt_ref, gt_ref):
    tq = x_ref.shape[0]
    x = x_ref[...]
    ms = jnp.mean(x * x, axis=-1, keepdims=True)
    h = (x * lax.rsqrt(ms + EPS) * g_ref[...]).astype(BF16)
    p = jnp.dot(h, wn_ref[...], preferred_element_type=F32)
    hg_ref[...] = jnp.dot(h, wh_ref[...], preferred_element_type=F32)
    cos = cos_ref[...]
    sin = sin_ref[...]
    lane = lax.broadcasted_iota(jnp.int32, (tq, LANES), 1)
    low_head = lane < HEAD_DIM
    dim = lane & (HEAD_DIM - 1)
    first_half = dim < ROT_DIM // 2

    def norm_rope(xs, gain, scale):
        x2 = xs * xs
        s_lo = jnp.sum(jnp.where(low_head, x2, 0.0), axis=-1, keepdims=True)
        s_hi = jnp.sum(jnp.where(low_head, 0.0, x2), axis=-1, keepdims=True)
        ms = jnp.where(low_head, s_lo, s_hi) * (1.0 / HEAD_DIM)
        y = xs * lax.rsqrt(ms + EPS) * gain
        partner = jnp.where(first_half,
                            pltpu.roll(y, LANES - ROT_DIM // 2, 1),
                            pltpu.roll(y, ROT_DIM // 2, 1))
        return (y * cos + partner * sin) * scale

    def split_groups(ref, x, lanes=slice(None)):
        for g in range(NSA_GROUPS):
            ref[g, :, lanes] = x[:, g * HEAD_DIM:(g + 1) * HEAD_DIM].astype(BF16)

    def split_groups_t(ref, x):
        xt = x.T
        for g in range(NSA_GROUPS):
            ref[g, 0:HEAD_DIM, :] = xt[g * HEAD_DIM:(g + 1) * HEAD_DIM, :].astype(BF16)
            ref[g, HEAD_DIM:V_ROWS, :] = _value_rows_tail(tq)

    qg = qg_ref[...]
    kg = kg_ref[...]
    for s in range(NSA_WIDTH // LANES):
        yt = norm_rope(p[:, s * LANES:(s + 1) * LANES], qg, Q_SCALE).T
        g, h0 = divmod(2 * s, HEADS_PER_GROUP)
        for j in range(2):
            qt_ref[g, :, (h0 + j) * tq:(h0 + j + 1) * tq] = (
                yt[j * HEAD_DIM:(j + 1) * HEAD_DIM, :].astype(BF16))
    base = NSA_WIDTH
    split_groups(kc_ref, norm_rope(p[:, base:base + LANES], kg, 1.0))
    tok = pl.program_id(1) * tq + lax.broadcasted_iota(jnp.int32, (tq, SLC_LANES), 0)
    blk = lax.broadcasted_iota(jnp.int32, (tq, SLC_LANES), 1)
    one_hot = jnp.where((tok >> SLC_SHIFT) == blk, 1.0, 0.0).astype(BF16)
    for g in range(NSA_GROUPS):
        ks_ref[g, :, 0:SLC_LANES] = one_hot
    split_groups(ks_ref, norm_rope(p[:, base + LANES:base + 2 * LANES], kg, 1.0),
                 slice(SLC_LANES, SLC_LANES + HEAD_DIM))
    split_groups(kw_ref, norm_rope(p[:, base + 2 * LANES:base + 3 * LANES], kg, 1.0))
    base += 3 * LANES
    split_groups(vc_ref, p[:, base:base + LANES])
    split_groups_t(vst_ref, p[:, base + LANES:base + 2 * LANES])
    split_groups_t(vwt_ref, p[:, base + 2 * LANES:base + 3 * LANES])
    base += 3 * LANES
    gates_t = jax.nn.sigmoid(p[:, base:base + LANES]).T
    per_group = HEADS_PER_GROUP * 3
    for g in range(NSA_GROUPS):
        gt_ref[g, 0:per_group, :] = gates_t[g * per_group:(g + 1) * per_group, :]
        gt_ref[g, per_group:GATE_ROWS, :] = jnp.zeros((GATE_ROWS - per_group, tq), F32)


def _inproj(x2, gain, w_nsa, w_hg, cos_t, sin_t, qg, kg, batch, seq):
    tq = Q_TILE
    nt = seq // tq
    d = x2.shape[1]
    tok = lambda b, i: (b * nt + i, 0)
    const = lambda b, i: (0, 0)
    rows = lambda b, i: (b, 0, i, 0)
    cols = lambda b, i: (b, 0, 0, i)
    grp = NSA_GROUPS
    return pl.pallas_call(
        _inproj_kernel,
        grid=(batch, nt),
        in_specs=[
            pl.BlockSpec((tq, d), tok),
            pl.BlockSpec((1, d), const),
            pl.BlockSpec(w_nsa.shape, const),
            pl.BlockSpec(w_hg.shape, const),
            pl.BlockSpec((tq, LANES), lambda b, i: (i, 0)),
            pl.BlockSpec((tq, LANES), lambda b, i: (i, 0)),
            pl.BlockSpec((1, LANES), const),
            pl.BlockSpec((1, LANES), const),
        ],
        out_specs=[
            pl.BlockSpec((tq, w_hg.shape[1]), tok),
            pl.BlockSpec((None, grp, None, HEAD_DIM, Q_COLS), lambda b, i: (b, 0, i, 0, 0)),
            pl.BlockSpec((None, grp, tq, HEAD_DIM), rows),
            pl.BlockSpec((None, grp, tq, SLC_LANES + HEAD_DIM), rows),
            pl.BlockSpec((None, grp, tq, HEAD_DIM), rows),
            pl.BlockSpec((None, grp, tq, HEAD_DIM), rows),
            pl.BlockSpec((None, grp, V_ROWS, tq), cols),
            pl.BlockSpec((None, grp, V_ROWS, tq), cols),
            pl.BlockSpec((None, grp, GATE_ROWS, tq), cols),
        ],
        out_shape=[
            jax.ShapeDtypeStruct((batch * seq, w_hg.shape[1]), F32),
            jax.ShapeDtypeStruct((batch, grp, nt, HEAD_DIM, Q_COLS), BF16),
            jax.ShapeDtypeStruct((batch, grp, seq, HEAD_DIM), BF16),
            jax.ShapeDtypeStruct((batch, grp, seq, SLC_LANES + HEAD_DIM), BF16),
            jax.ShapeDtypeStruct((batch, grp, seq, HEAD_DIM), BF16),
            jax.ShapeDtypeStruct((batch, grp, seq, HEAD_DIM), BF16),
            jax.ShapeDtypeStruct((batch, grp, V_ROWS, seq), BF16),
            jax.ShapeDtypeStruct((batch, grp, V_ROWS, seq), BF16),
            jax.ShapeDtypeStruct((batch, grp, GATE_ROWS, seq), F32),
        ],
        compiler_params=_cparams(2),
        name="inproj",
    )(x2, gain, w_nsa, w_hg, cos_t, sin_t, qg, kg)


def _compress_kernel(tk_ref, tv_ref, pek_ref, pev_ref, w1k_ref, w1v_ref,
                     w2k_ref, w2v_ref, ok_ref, ov_ref):
    half = CMP_STRIDE * HEAD_DIM

    def run(t_ref, pe_ref, w1_ref, w2_ref, o_ref, transposed):
        t = t_ref[...]
        n_rows = t.shape[0]
        top = jnp.dot(t, w1_ref[0:half, :], preferred_element_type=F32)
        bot = jnp.dot(t, w1_ref[half:2 * half, :], preferred_element_type=F32)
        bot_next = pltpu.roll(bot, n_rows - 1, 0)
        pe = jnp.dot(pe_ref[...], w1_ref[...], preferred_element_type=F32)[0:1, :]
        h = top + bot_next + pe
        h = h * jax.nn.sigmoid(h)
        o = jnp.dot(h.astype(BF16), w2_ref[...], preferred_element_type=F32)
        r = lax.broadcasted_iota(jnp.int32, o.shape, 0)
        o = jnp.where(r < n_rows - 1, o, 0.0)
        if transposed:
            ot = jnp.concatenate([o, jnp.zeros_like(o)], axis=1).T
            o_ref[0:HEAD_DIM, :] = ot[0:HEAD_DIM, :].astype(BF16)
            o_ref[HEAD_DIM:V_ROWS, :] = _value_rows_tail(n_rows)
        else:
            o_ref[...] = o.astype(BF16)

    run(tk_ref, pek_ref, w1k_ref, w2k_ref, ok_ref, False)
    run(tv_ref, pev_ref, w1v_ref, w2v_ref, ov_ref, True)


def _compress(tk, tv, pek, pev, w1k, w1v, w2k, w2v):
    bg, n_rows, width = tk.shape
    blk = lambda i: (i, 0, 0)
    const = lambda i: (0, 0)
    return pl.pallas_call(
        _compress_kernel,
        grid=(bg,),
        in_specs=[
            pl.BlockSpec((None, n_rows, width), blk),
            pl.BlockSpec((None, n_rows, width), blk),
            pl.BlockSpec(pek.shape, const),
            pl.BlockSpec(pev.shape, const),
            pl.BlockSpec(w1k.shape, const),
            pl.BlockSpec(w1v.shape, const),
            pl.BlockSpec(w2k.shape, const),
            pl.BlockSpec(w2v.shape, const),
        ],
        out_specs=[pl.BlockSpec((None, n_rows, HEAD_DIM), blk),
                   pl.BlockSpec((None, V_ROWS, n_rows), blk)],
        out_shape=[jax.ShapeDtypeStruct((bg, n_rows, HEAD_DIM), BF16),
                   jax.ShapeDtypeStruct((bg, V_ROWS, n_rows), BF16)],
        compiler_params=_cparams(1),
        name="compress",
    )(tk, tv, pek, pev, w1k, w1v, w2k, w2v)


def _nsa_attn_kernel(qt_ref, gt_ref, kc_ref, vct_ref, mapt_ref, ks_ref, vst_ref,
                     kw_ref, vwt_ref, y_ref, qaug_ref, s0_ref, s1_ref, c0_ref, c1_ref,
                     m_ref, acc_ref, ocmp_ref, owin_ref):
    i = pl.program_id(2)
    q0 = i * Q_TILE
    qt = qt_ref[...]
    col = lax.broadcasted_iota(jnp.int32, (1, Q_COLS), 1)
    tpos = q0 + (col & (Q_TILE - 1))

    def compressed_branch(n_rows, n_blk):
        kc = kc_ref[0:n_rows, :]
        cmp_end = lax.broadcasted_iota(jnp.int32, (n_rows, 1), 0) * CMP_STRIDE + (CMP_LEN - 1)
        mapt = mapt_ref[0:n_blk, 0:n_rows]
        head_cols = [slice(h * Q_TILE, (h + 1) * Q_TILE) for h in range(HEADS_PER_GROUP)]
        scores = [jnp.dot(kc, qt[:, cols], preferred_element_type=F32) for cols in head_cols]
        imp = None
        for cols, s in zip(head_cols, scores):
            s = jnp.where(cmp_end <= tpos[:, cols], s, NEG)
            m = jnp.max(s, axis=0, keepdims=True)
            p = jnp.exp2(s - jnp.maximum(m, 0.1 * NEG)).astype(BF16)
            acc = jnp.dot(vct_ref[:, 0:n_rows], p, preferred_element_type=F32)
            l = acc[HEAD_DIM:HEAD_DIM + 1, :]
            inv = jnp.where(l > 0.0, 1.0 / l, 0.0)
            ocmp_ref[:, cols] = acc[0:HEAD_DIM, :] * inv
            part = jnp.dot(mapt, p, preferred_element_type=F32) * inv
            imp = part if imp is None else imp + part
        return imp

    def select_blocks(imp, n_blk):
        blk_i = lax.broadcasted_iota(jnp.int32, (n_blk, Q_TILE), 0)
        cur = (q0 + lax.broadcasted_iota(jnp.int32, (n_blk, Q_TILE), 1)) >> SLC_SHIFT
        blk = blk_i.astype(F32)
        sel = jnp.zeros((n_blk, Q_TILE), F32)
        for forced in (blk_i == 0, blk_i == cur, blk_i == cur - 1):
            sel = jnp.where(forced, 1.0, sel)
        val = jnp.where(blk_i <= cur, jnp.where(sel > 0.5, -1.0, imp), -1.0)
        for _ in range(min(SLC_TOPN, n_blk) - 3):
            mx = jnp.max(val, axis=0, keepdims=True)
            first = jnp.min(jnp.where(val == mx, blk, float(n_blk)), axis=0, keepdims=True)
            first = jnp.where(mx > -0.5, first, -1.0)
            pick = blk == first
            sel = jnp.where(pick, 1.0, sel)
            val = jnp.where(pick, -1.0, val)
        sel_bias = jnp.where(sel > 0.5, 0.0, -BIG).astype(BF16)
        for h in range(HEADS_PER_GROUP):
            qaug_ref[0:n_blk, h * Q_TILE:(h + 1) * Q_TILE] = sel_bias
        if n_blk < SLC_LANES:
            qaug_ref[n_blk:SLC_LANES, :] = jnp.full((SLC_LANES - n_blk, Q_COLS), -BIG, BF16)

    col_tiles = [slice(c0, c0 + MXU_COLS) for c0 in range(0, Q_COLS, MXU_COLS)]
    buffers = ((s0_ref, c0_ref), (s1_ref, c1_ref))

    def slc_scores(k0, s_ref, cmax_ref, cols):
        sc = jnp.dot(ks_ref[pl.ds(k0, SLC_KEYS), :], qaug_ref[:, cols],
                     preferred_element_type=F32)
        s_ref[:, cols] = sc
        cmax_ref[:, cols] = jnp.max(sc, axis=0, keepdims=True)

    span = WINDOW + Q_TILE
    t_loc = col & (Q_TILE - 1)

    def window_scores(interior):
        if not interior:
            sw = jnp.dot(kw_ref[0:span, :], qt, preferred_element_type=F32)
            kpos = lax.broadcasted_iota(jnp.int32, (span, 1), 0)
            sw = jnp.where(kpos <= tpos, jnp.where(kpos > tpos - WINDOW, sw, NEG), NEG)
            return [sw], 0
        start = pl.multiple_of(q0 - WINDOW, Q_TILE)
        i_loc = lax.broadcasted_iota(jnp.int32, (Q_TILE, 1), 0)
        n_parts = span // Q_TILE
        parts = []
        for r in range(n_parts):
            sw = jnp.dot(kw_ref[pl.ds(start + r * Q_TILE, Q_TILE), :], qt,
                         preferred_element_type=F32)
            if r == 0:
                sw = jnp.where(i_loc > t_loc, sw, NEG)
            elif r == n_parts - 1:
                sw = jnp.where(i_loc <= t_loc, sw, NEG)
            parts.append(sw)
        return parts, start

    def window_finish(parts, v_start):
        mw = None
        for sw in parts:
            cm = jnp.max(sw, axis=0, keepdims=True)
            mw = cm if mw is None else jnp.maximum(mw, cm)
        accw, off = None, 0
        for sw in parts:
            pw = jnp.exp2(sw - mw).astype(BF16)
            d = jnp.dot(vwt_ref[:, pl.ds(v_start + off, sw.shape[0])], pw,
                        preferred_element_type=F32)
            accw = d if accw is None else accw + d
            off += sw.shape[0]
        owin_ref[...] = accw[0:HEAD_DIM, :] / accw[HEAD_DIM:HEAD_DIM + 1, :]

    def front(n_rows, n_blk, interior):
        imp = compressed_branch(n_rows, n_blk)
        window_finish(*window_scores(interior))
        select_blocks(imp, n_blk)
        qaug_ref[SLC_LANES:SLC_LANES + HEAD_DIM, :] = qt
        for cols in col_tiles:
            slc_scores(0, *buffers[0], cols)
        for cols in col_tiles[:-1]:
            slc_scores(SLC_KEYS, *buffers[1], cols)

    n_cmp_rows = kc_ref.shape[0]
    row_steps = list(range(LANES, n_cmp_rows, LANES)) + [n_cmp_rows]
    q_end = q0 + Q_TILE
    lo = 0
    for n_rows in row_steps:
        hi = n_rows * CMP_STRIDE
        n_blk = min(SLC_LANES, -(-(hi // SLC_BLOCK) // 16) * 16)
        cuts = [lo, WINDOW, hi] if lo < WINDOW < hi else [lo, hi]
        for c_lo, c_hi in zip(cuts[:-1], cuts[1:]):
            pl.when(jnp.logical_and(q_end > c_lo, q_end <= c_hi))(
                functools.partial(front, n_rows, n_blk, c_lo >= WINDOW))
        lo = hi

    def slc_update(k0, s_ref, cmax_ref, cols, masked):
        sc = s_ref[:, cols]
        if masked:
            kpos = k0 + lax.broadcasted_iota(jnp.int32, (SLC_KEYS, 1), 0)
            sc = jnp.where(kpos <= tpos[:, cols], sc, -BIG)
            cmax = jnp.max(sc, axis=0, keepdims=True)
        else:
            cmax = cmax_ref[:, cols]
        m_prev = m_ref[:, cols]
        m_new = jnp.maximum(m_prev, cmax)
        pr = jnp.exp2(sc - m_new).astype(BF16)
        pv = jnp.dot(vst_ref[:, pl.ds(k0, SLC_KEYS)], pr, preferred_element_type=F32)
        acc_ref[:, cols] = acc_ref[:, cols] * jnp.exp2(m_prev - m_new) + pv
        m_ref[:, cols] = m_new

    n_full = q0 // SLC_KEYS
    last_tile = ks_ref.shape[0] // SLC_KEYS - 1

    def tile_start(t):
        return pl.multiple_of(jnp.minimum(t, last_tile) * SLC_KEYS, SLC_KEYS)

    even, odd = buffers

    def slc_body(j, carry):
        k_even, k_odd = tile_start(2 * j), tile_start(2 * j + 1)
        k_even_next, k_odd_next = tile_start(2 * j + 2), tile_start(2 * j + 3)
        slc_scores(k_odd, *odd, col_tiles[-1])
        for cols in col_tiles:
            slc_update(k_even, *even, cols, masked=False)
            slc_scores(k_even_next, *even, cols)
        for cols in col_tiles[:-1]:
            slc_update(k_odd, *odd, cols, masked=False)
            slc_scores(k_odd_next, *odd, cols)
        slc_update(k_odd, *odd, col_tiles[-1], masked=False)
        return carry

    m_ref[...] = jnp.full(m_ref.shape, NEG, F32)
    acc_ref[...] = jnp.zeros(acc_ref.shape, F32)
    lax.fori_loop(0, n_full // 2, slc_body, 0)
    k_last = tile_start(n_full)

    @pl.when((n_full & 1) == 0)
    def _():
        for cols in col_tiles:
            slc_update(k_last, *even, cols, masked=True)

    @pl.when((n_full & 1) == 1)
    def _():
        slc_scores(k_last, *odd, col_tiles[-1])
        for cols in col_tiles:
            slc_update(tile_start(n_full - 1), *even, cols, masked=False)
        for cols in col_tiles:
            slc_update(k_last, *odd, cols, masked=True)

    o_slc = acc_ref[0:HEAD_DIM, :] / acc_ref[HEAD_DIM:HEAD_DIM + 1, :]

    g = gt_ref[...]

    def gate_row(branch):
        return jnp.concatenate(
            [g[h * 3 + branch:h * 3 + branch + 1, :] for h in range(HEADS_PER_GROUP)], axis=1)

    y = (gate_row(0) * ocmp_ref[...] + gate_row(1) * o_slc
         + gate_row(2) * owin_ref[...])
    for pair in range(HEADS_PER_GROUP // 2):
        lo, mid, hi = (2 * pair) * Q_TILE, (2 * pair + 1) * Q_TILE, (2 * pair + 2) * Q_TILE
        stacked = jnp.concatenate([y[:, lo:mid], y[:, mid:hi]], axis=0)
        y_ref[:, pair * LANES:(pair + 1) * LANES] = stacked.T


def _nsa_attn(qt, gt, kc, vct, mapt, ks, vst, kw, vwt):
    b, g, nt = qt.shape[:3]
    seq = ks.shape[2]
    n_cmp_rows = kc.shape[2]
    per_bg = lambda bi, gi, i: (bi, gi, 0, 0)
    return pl.pallas_call(
        _nsa_attn_kernel,
        grid=(b, g, nt),
        in_specs=[
            pl.BlockSpec((None, None, None, HEAD_DIM, Q_COLS), lambda bi, gi, i: (bi, gi, i, 0, 0)),
            pl.BlockSpec((None, None, GATE_ROWS, Q_TILE), lambda bi, gi, i: (bi, gi, 0, i)),
            pl.BlockSpec((None, None, n_cmp_rows, HEAD_DIM), per_bg),
            pl.BlockSpec((None, None, V_ROWS, n_cmp_rows), per_bg),
            pl.BlockSpec(mapt.shape, lambda bi, gi, i: (0, 0)),
            pl.BlockSpec((None, None, seq, HEAD_DIM + SLC_LANES), per_bg),
            pl.BlockSpec((None, None, V_ROWS, seq), per_bg),
            pl.BlockSpec((None, None, seq, HEAD_DIM), per_bg),
            pl.BlockSpec((None, None, V_ROWS, seq), per_bg),
        ],
        out_specs=pl.BlockSpec((Q_TILE, HEADS_PER_GROUP * HEAD_DIM),
                               lambda bi, gi, i: (bi * nt + i, gi)),
        out_shape=jax.ShapeDtypeStruct((b * seq, NSA_WIDTH), F32),
        scratch_shapes=[
            pltpu.VMEM((SLC_LANES + HEAD_DIM, Q_COLS), BF16),
            pltpu.VMEM((SLC_KEYS, Q_COLS), F32),
            pltpu.VMEM((SLC_KEYS, Q_COLS), F32),
            pltpu.VMEM((1, Q_COLS), F32),
            pltpu.VMEM((1, Q_COLS), F32),
            pltpu.VMEM((1, Q_COLS), F32),
            pltpu.VMEM((V_ROWS, Q_COLS), F32),
            pltpu.VMEM((HEAD_DIM, Q_COLS), F32),
            pltpu.VMEM((HEAD_DIM, Q_COLS), F32),
        ],
        compiler_params=_cparams(3),
        name="nsa_attn",
    )(qt, gt, kc, vct, mapt, ks, vst, kw, vwt)


SCAN_MXU_SIZES = (2, 4, 8)


def _scan_matrices(rows):
    t = np.arange(rows)[:, None]
    s = np.arange(rows)[None, :]
    mats = [(s // h == t // h) & (s <= t) for h in SCAN_MXU_SIZES]
    mats += [(s // h == t // h) & (s >= t) for h in SCAN_MXU_SIZES]
    return jnp.asarray(np.concatenate([np.tile(m, (1, 3)) for m in mats], axis=0), BF16)


def _segmented_scans(g, mats_ref):
    rows = g.shape[0]
    hi = g.astype(BF16)
    rest = g - hi.astype(F32)
    mid = rest.astype(BF16)
    lo = (rest - mid.astype(F32)).astype(BF16)
    small = jnp.dot(mats_ref[...], jnp.concatenate([hi, mid, lo], axis=0),
                    preferred_element_type=F32)
    n = len(SCAN_MXU_SIZES)
    out = {1: (g, g)}
    for j, h in enumerate(SCAN_MXU_SIZES):
        out[h] = (small[j * rows:(j + 1) * rows], small[(n + j) * rows:(n + j + 1) * rows])
    h = SCAN_MXU_SIZES[-1]
    pre, suf = out[h]
    while h < rows:
        pre_parts, suf_parts = [], []
        for b in range(rows // (2 * h)):
            lo_r, mid_r, hi_r = b * 2 * h, b * 2 * h + h, (b + 1) * 2 * h
            pre_parts += [pre[lo_r:mid_r], pre[mid_r:hi_r] + pre[mid_r - 1:mid_r, :]]
            suf_parts += [suf[lo_r:mid_r] + suf[mid_r:mid_r + 1, :], suf[mid_r:hi_r]]
        pre = jnp.concatenate(pre_parts, axis=0)
        suf = jnp.concatenate(suf_parts, axis=0)
        h *= 2
        out[h] = (pre, suf)
    return out


def _hgrn_kernel(hg_ref, lb_ref, gain_ref, mats_ref, o_ref, state_ref):
    c = pl.program_id(1)

    @pl.when(c == 0)
    def _():
        state_ref[...] = jnp.zeros_like(state_ref)

    rows = HG_CHUNK
    t_idx = lax.broadcasted_iota(jnp.int32, (rows, rows), 0)
    s_idx = lax.broadcasted_iota(jnp.int32, (rows, rows), 1)
    diff = jnp.where(t_idx > s_idx, t_idx ^ s_idx, 0)
    nt_dims = (((1,), (1,)), ((), ()))
    lb = lb_ref[...]
    gain = gain_ref[...]

    def heads(x):
        return [x[:, h * HG_DIM:(h + 1) * HG_DIM] for h in range(HG_HEADS)]

    def gates_and_scans(tok):
        w = HG_WIDTH
        q = hg_ref[tok, 0:w]
        f = lb + (1.0 - lb) * jax.nn.sigmoid(hg_ref[tok, w:2 * w])
        logf = jnp.log(f)
        return dict(tok=tok, qf=q * jax.nn.sigmoid(q), k=1.0 - f, logf=logf,
                    scans=_segmented_scans(logf, mats_ref))

    def intra_chunk(c):
        qf, k, logf, scans = c["qf"], c["k"], c["logf"], c["scans"]
        qb, kb = heads(qf.astype(BF16)), heads(k.astype(BF16))
        a = [jnp.where(t_idx == s_idx,
                       lax.dot_general(qb[h], kb[h], nt_dims, preferred_element_type=F32), 0.0)
             for h in range(HG_HEADS)]
        level, h_size = 0, 1
        while h_size < rows:
            pre, suf = scans[h_size]
            ql = heads((qf * jnp.exp(pre)).astype(BF16))
            kl = heads((k * jnp.exp(suf - logf)).astype(BF16))
            mask = (diff >> level) == 1
            for h in range(HG_HEADS):
                al = lax.dot_general(ql[h], kl[h], nt_dims, preferred_element_type=F32)
                a[h] = jnp.where(mask, al, a[h])
            level += 1
            h_size *= 2
        cum, rev = scans[rows]
        c.update(a=a, cum=cum,
                 q_in=heads((qf * jnp.exp(cum)).astype(BF16)),
                 k_out=heads((k * jnp.exp(rev - logf)).astype(BF16)))
        return c

    def outputs_and_state(c, states):
        tok, w = c["tok"], HG_WIDTH
        v = hg_ref[tok, 2 * w:3 * w]
        og = hg_ref[tok, 3 * w:4 * w]
        vh = heads(v.astype(BF16))
        new_states = []
        for h in range(HG_HEADS):
            lanes = slice(h * HG_DIM, (h + 1) * HG_DIM)
            state_t = states[h]
            o = jnp.dot(c["a"][h].astype(BF16), vh[h], preferred_element_type=F32)
            o = o + lax.dot_general(c["q_in"][h], state_t.astype(BF16), nt_dims,
                                    preferred_element_type=F32)
            decay = jnp.exp(c["cum"][rows - 1:rows, lanes])
            v_t = v[:, lanes].T.astype(BF16)
            new_states.append(
                state_t * decay + jnp.dot(v_t, c["k_out"][h], preferred_element_type=F32))
            ms = jnp.mean(o * o, axis=-1, keepdims=True)
            gate = og[:, lanes]
            o_ref[tok, lanes] = (o * lax.rsqrt(ms + EPS) * gain[:, lanes]
                                 * (gate * jax.nn.sigmoid(gate)))
        return new_states

    toks = [slice(j * rows, (j + 1) * rows) for j in range(hg_ref.shape[0] // rows)]
    chunks = [gates_and_scans(tok) for tok in toks]
    chunks = [intra_chunk(c) for c in chunks]
    states = [state_ref[h] for h in range(HG_HEADS)]
    for c in chunks:
        states = outputs_and_state(c, states)
    for h in range(HG_HEADS):
        state_ref[h] = states[h]


def _hgrn(hg_proj, lb, gain, batch, seq):
    n = hg_proj.shape[0]
    n_chunk = seq // HG_STEP
    mats = _scan_matrices(HG_CHUNK)
    row = lambda b, c: (b * n_chunk + c, 0)
    const = lambda b, c: (0, 0)
    return pl.pallas_call(
        _hgrn_kernel,
        grid=(batch, n_chunk),
        in_specs=[
            pl.BlockSpec((HG_STEP, 4 * HG_WIDTH), row),
            pl.BlockSpec((1, HG_WIDTH), const),
            pl.BlockSpec((1, HG_WIDTH), const),
            pl.BlockSpec(mats.shape, const),
        ],
        out_specs=pl.BlockSpec((HG_STEP, HG_WIDTH), row),
        out_shape=jax.ShapeDtypeStruct((n, HG_WIDTH), F32),
        scratch_shapes=[pltpu.VMEM((HG_HEADS, HG_DIM, HG_DIM), F32)],
        compiler_params=_cparams(2),
        name="hgrn",
    )(hg_proj, lb, gain, mats)


def _out_mlp_kernel(x_ref, yn_ref, yh_ref, gn_ref, gm_ref, wa_ref, wb_ref,
                    w1_ref, w2_ref, o_ref, *, ff_chunk):
    yn = yn_ref[...]
    ms = jnp.mean(yn * yn, axis=-1, keepdims=True)
    yn = (yn * lax.rsqrt(ms + EPS) * gn_ref[...]).astype(BF16)
    x1 = (x_ref[...]
          + jnp.dot(yn, wa_ref[...], preferred_element_type=F32)
          + jnp.dot(yh_ref[...].astype(BF16), wb_ref[...], preferred_element_type=F32))
    ms = jnp.mean(x1 * x1, axis=-1, keepdims=True)
    h = (x1 * lax.rsqrt(ms + EPS) * gm_ref[...]).astype(BF16)
    mlp = None
    for c in range(w1_ref.shape[1] // ff_chunk):
        cols = slice(c * ff_chunk, (c + 1) * ff_chunk)
        u = jnp.maximum(jnp.dot(h, w1_ref[:, cols], preferred_element_type=F32), 0.0)
        d = jnp.dot((u * u).astype(BF16), w2_ref[cols, :], preferred_element_type=F32)
        mlp = d if mlp is None else mlp + d
    o_ref[...] = x1 + mlp


def _out_mlp(x2, y_nsa, y_hg, gn, gm, wa, wb, w1, w2, tm=512, ff_chunk=1024):
    n, d = x2.shape
    row = lambda i: (i, 0)
    const = lambda i: (0, 0)

    def resident(shape):
        return pl.BlockSpec(shape, const, pipeline_mode=pl.Buffered(1))

    return pl.pallas_call(
        functools.partial(_out_mlp_kernel, ff_chunk=ff_chunk),
        grid=(n // tm,),
        in_specs=[
            pl.BlockSpec((tm, d), row),
            pl.BlockSpec((tm, y_nsa.shape[1]), row),
            pl.BlockSpec((tm, y_hg.shape[1]), row),
            pl.BlockSpec((1, y_nsa.shape[1]), const),
            pl.BlockSpec((1, d), const),
            resident(wa.shape),
            resident(wb.shape),
            resident(w1.shape),
            resident(w2.shape),
        ],
        out_specs=pl.BlockSpec((tm, d), row),
        out_shape=jax.ShapeDtypeStruct((n, d), F32),
        compiler_params=_cparams(1),
        name="out_mlp",
    )(x2, y_nsa, y_hg, gn, gm, wa, wb, w1, w2)


def _rope_tables(seq):
    pos = jnp.arange(seq, dtype=F32)
    inv_freq = ROPE_THETA ** (-jnp.arange(0, ROT_DIM, 2, dtype=F32) / ROT_DIM)
    ang = pos[:, None] * inv_freq[None, :]
    cos, sin = jnp.cos(ang), jnp.sin(ang)
    rest = HEAD_DIM - ROT_DIM
    cos_h = jnp.concatenate([cos, cos, jnp.ones((seq, rest), F32)], axis=-1)
    sin_h = jnp.concatenate([-sin, sin, jnp.zeros((seq, rest), F32)], axis=-1)
    return jnp.tile(cos_h, (1, 2)), jnp.tile(sin_h, (1, 2))


def _cmp_to_slc_t(n_cmp_rows, n_cmp, n_slc):
    tok = np.arange(n_cmp)[:, None] * CMP_STRIDE + np.arange(CMP_LEN)[None, :]
    frac = (tok[:, :, None] // SLC_BLOCK == np.arange(n_slc)[None, None, :]).mean(axis=1)
    out = np.zeros((SLC_LANES, n_cmp_rows), np.float32)
    out[:n_slc, :n_cmp] = frac.T
    return jnp.asarray(out, BF16)


def kernel(x, attn_norm, w_in, q_norm, k_norm, cmp_pe_k, cmp_w1_k, cmp_w2_k, cmp_pe_v,
           cmp_w1_v, cmp_w2_v, nsa_out_norm, hgrn_lb, hgrn_out_norm, w_out, mlp_norm,
           w_mlp_in, w_mlp_out):
    batch, seq, d_model = x.shape
    n = batch * seq
    n_slc = seq // SLC_BLOCK
    n_cmp_rows = seq // CMP_STRIDE
    n_cmp = (seq - CMP_LEN) // CMP_STRIDE + 1
    assert n_slc <= SLC_LANES and seq % SLC_KEYS == 0 and seq >= max(WINDOW + Q_TILE, 2 * SLC_KEYS)
    assert w_in.shape[0] == 1, "single-layer block"

    kvw = KV_WIDTH
    o_q, o_kc, o_vc, o_ks, o_vs, o_kw, o_vw, o_gate = np.cumsum(
        [0, NSA_WIDTH, kvw, kvw, kvw, kvw, kvw, kvw])
    o_hg = o_gate + NSA_HEADS * 3
    w = w_in[0]
    pad = jnp.zeros((d_model, NSA_COLS - (NSA_WIDTH + 6 * kvw + NSA_HEADS * 3)), w.dtype)
    w_nsa = jnp.concatenate(
        [w[:, o_q:o_kc], w[:, o_kc:o_vc], w[:, o_ks:o_vs], w[:, o_kw:o_vw],
         w[:, o_vc:o_ks], w[:, o_vs:o_kw], w[:, o_vw:o_gate], w[:, o_gate:o_hg], pad],
        axis=1).astype(BF16)
    w_hg = w[:, o_hg:].astype(BF16)
    x2 = x.reshape(n, d_model)
    cos_t, sin_t = _rope_tables(seq)
    qg = jnp.tile(q_norm[0], 2)[None, :]
    kg = jnp.tile(k_norm[0], 2)[None, :]
    hg_proj, qt, kc, ks_aug, kw, vc, vs_t, vw_t, gt = _inproj(
        x2, attn_norm[0][None, :], w_nsa, w_hg, cos_t, sin_t, qg, kg, batch, seq)


    blocks = lambda t: t.reshape(batch * NSA_GROUPS, n_cmp_rows, CMP_STRIDE * HEAD_DIM)
    pe8 = lambda pe: jnp.broadcast_to(pe.reshape(1, CMP_LEN * HEAD_DIM), (8, CMP_LEN * HEAD_DIM)).astype(BF16)
    k_cmp, v_cmp_t = _compress(
        blocks(kc), blocks(vc), pe8(cmp_pe_k[0]), pe8(cmp_pe_v[0]),
        cmp_w1_k[0].astype(BF16), cmp_w1_v[0].astype(BF16),
        cmp_w2_k[0].astype(BF16), cmp_w2_v[0].astype(BF16))
    k_cmp = k_cmp.reshape(batch, NSA_GROUPS, n_cmp_rows, HEAD_DIM)
    v_cmp_t = v_cmp_t.reshape(batch, NSA_GROUPS, V_ROWS, n_cmp_rows)
    y_nsa = _nsa_attn(qt, gt, k_cmp, v_cmp_t, _cmp_to_slc_t(n_cmp_rows, n_cmp, n_slc),
                      ks_aug, vs_t, kw, vw_t)

    lb_all = jnp.cumsum(jax.nn.softmax(hgrn_lb.astype(F32), axis=0), axis=0)
    y_hg = _hgrn(hg_proj, lb_all[0][None, :], jnp.tile(hgrn_out_norm[0], HG_HEADS)[None, :],
                 batch, seq)

    wo = w_out[0].astype(BF16)
    out = _out_mlp(x2, y_nsa, y_hg, nsa_out_norm[0][None, :], mlp_norm[0][None, :],
                   wo[:NSA_WIDTH], wo[NSA_WIDTH:], w_mlp_in[0].astype(BF16),
                   w_mlp_out[0].astype(BF16))
    return out.reshape(batch, seq, d_model)
```

```python
import functools

import jax
import jax.numpy as jnp
import numpy as np
from jax import lax
from jax.experimental import pallas as pl
from jax.experimental.pallas import tpu as pltpu

F32 = jnp.float32
BF16 = jnp.bfloat16

NSA_HEADS = 8
NSA_GROUPS = 2
HEADS_PER_GROUP = NSA_HEADS // NSA_GROUPS
HEAD_DIM = 64
NSA_WIDTH = NSA_HEADS * HEAD_DIM
KV_WIDTH = NSA_GROUPS * HEAD_DIM
CMP_LEN = 32
CMP_STRIDE = 16
CMP_HIDDEN = 2 * HEAD_DIM
SLC_BLOCK = 64
SLC_SHIFT = 6
SLC_TOPN = 16
WINDOW = 512
ROT_DIM = HEAD_DIM // 4
ROPE_THETA = 500000.0
HG_HEADS = 4
HG_DIM = 128
HG_WIDTH = HG_HEADS * HG_DIM
EPS = 1e-6

LANES = 128
Q_TILE = 512
Q_COLS = HEADS_PER_GROUP * Q_TILE
SLC_KEYS = 512
SLC_LANES = 128
MXU_COLS = 256
V_ROWS = 80
GATE_ROWS = 16
NSA_COLS = 1408
NEG = -1e30
Q_SCALE = HEAD_DIM ** -0.5 * 1.4426950408889634
BIG = 2.0 ** 100
HG_CHUNK = 128
HG_STEP = 512
VMEM_LIMIT = 56 * 1024 * 1024


def _cparams(n_grid):
    return pltpu.CompilerParams(
        dimension_semantics=("arbitrary",) * n_grid, vmem_limit_bytes=VMEM_LIMIT)


def _interleave(*pieces):
    live = list(pieces)
    while live:
        for gen in list(live):
            try:
                next(gen)
            except StopIteration:
                live.remove(gen)


def _value_rows_tail(cols):
    r = lax.broadcasted_iota(jnp.int32, (V_ROWS - HEAD_DIM, cols), 0)
    return jnp.where(r == 0, 1.0, 0.0).astype(BF16)


def _inproj_kernel(x_ref, g_ref, wn_ref, wh_ref, cos_ref, sin_ref, qg_ref, kg_ref,
                   lb_ref, hgain_ref, mats_ref,
                   yhg_ref, qt_ref, kc_ref, ks_ref, kw_ref, vc_ref, vst_ref, vwt_ref, gt_ref,
                   hgcur_ref, hgnext_ref, state_ref, *, n_tiles):
    tq = x_ref.shape[0]
    step = pl.program_id(1)

    @pl.when(step == 0)
    def _():
        state_ref[...] = jnp.zeros_like(state_ref)
        hgnext_ref[...] = jnp.zeros_like(hgnext_ref)

    hgcur_ref[...] = hgnext_ref[...]

    def project():
        x = x_ref[...]
        ms = jnp.mean(x * x, axis=-1, keepdims=True)
        h = (x * lax.rsqrt(ms + EPS) * g_ref[...]).astype(BF16)
        p_parts = []
        for c0 in range(0, NSA_COLS, MXU_COLS):
            c1 = min(c0 + MXU_COLS, NSA_COLS)
            p_parts.append(jnp.dot(h, wn_ref[:, c0:c1], preferred_element_type=F32))
            yield
        p = jnp.concatenate(p_parts, axis=1)
        for c0 in range(0, wh_ref.shape[1], MXU_COLS):
            hgnext_ref[:, c0:c0 + MXU_COLS] = jnp.dot(
                h, wh_ref[:, c0:c0 + MXU_COLS], preferred_element_type=F32)
            yield
        cos = cos_ref[...]
        sin = sin_ref[...]
        lane = lax.broadcasted_iota(jnp.int32, (tq, LANES), 1)
        low_head = lane < HEAD_DIM
        dim = lane & (HEAD_DIM - 1)
        first_half = dim < ROT_DIM // 2

        def norm_rope(xs, gain, scale):
            x2 = xs * xs
            s_lo = jnp.sum(jnp.where(low_head, x2, 0.0), axis=-1, keepdims=True)
            s_hi = jnp.sum(jnp.where(low_head, 0.0, x2), axis=-1, keepdims=True)
            ms = jnp.where(low_head, s_lo, s_hi) * (1.0 / HEAD_DIM)
            y = xs * lax.rsqrt(ms + EPS) * gain
            partner = jnp.where(first_half,
                                pltpu.roll(y, LANES - ROT_DIM // 2, 1),
                                pltpu.roll(y, ROT_DIM // 2, 1))
            return (y * cos + partner * sin) * scale

        def split_groups(ref, x, lanes=slice(None)):
            for g in range(NSA_GROUPS):
                ref[g, :, lanes] = x[:, g * HEAD_DIM:(g + 1) * HEAD_DIM].astype(BF16)

        def split_groups_t(ref, x):
            xt = x.T
            for g in range(NSA_GROUPS):
                ref[g, 0:HEAD_DIM, :] = xt[g * HEAD_DIM:(g + 1) * HEAD_DIM, :].astype(BF16)
                ref[g, HEAD_DIM:V_ROWS, :] = _value_rows_tail(tq)

        qg = qg_ref[...]
        kg = kg_ref[...]
        for s in range(NSA_WIDTH // LANES):
            yt = norm_rope(p[:, s * LANES:(s + 1) * LANES], qg, Q_SCALE).T
            g, h0 = divmod(2 * s, HEADS_PER_GROUP)
            for j in range(2):
                qt_ref[g, :, (h0 + j) * tq:(h0 + j + 1) * tq] = (
                    yt[j * HEAD_DIM:(j + 1) * HEAD_DIM, :].astype(BF16))
            yield
        base = NSA_WIDTH
        split_groups(kc_ref, norm_rope(p[:, base:base + LANES], kg, 1.0))
        yield
        tok = (jnp.minimum(step, n_tiles - 1) * tq
               + lax.broadcasted_iota(jnp.int32, (tq, SLC_LANES), 0))
        blk = lax.broadcasted_iota(jnp.int32, (tq, SLC_LANES), 1)
        one_hot = jnp.where((tok >> SLC_SHIFT) == blk, 1.0, 0.0).astype(BF16)
        for g in range(NSA_GROUPS):
            ks_ref[g, :, 0:SLC_LANES] = one_hot
        split_groups(ks_ref, norm_rope(p[:, base + LANES:base + 2 * LANES], kg, 1.0),
                     slice(SLC_LANES, SLC_LANES + HEAD_DIM))
        yield
        split_groups(kw_ref, norm_rope(p[:, base + 2 * LANES:base + 3 * LANES], kg, 1.0))
        yield
        base += 3 * LANES
        split_groups(vc_ref, p[:, base:base + LANES])
        split_groups_t(vst_ref, p[:, base + LANES:base + 2 * LANES])
        split_groups_t(vwt_ref, p[:, base + 2 * LANES:base + 3 * LANES])
        yield
        base += 3 * LANES
        gates_t = jax.nn.sigmoid(p[:, base:base + LANES]).T
        per_group = HEADS_PER_GROUP * 3
        for g in range(NSA_GROUPS):
            gt_ref[g, 0:per_group, :] = gates_t[g * per_group:(g + 1) * per_group, :]
            gt_ref[g, per_group:GATE_ROWS, :] = jnp.zeros((GATE_ROWS - per_group, tq), F32)

    _interleave(project(),
                _hgrn_tile(hgcur_ref, lb_ref, hgain_ref, mats_ref, yhg_ref, state_ref))


def _inproj_hgrn(x2, gain, w_nsa, w_hg, cos_t, sin_t, qg, kg, lb, hgain, batch, seq):
    assert HG_STEP == Q_TILE
    tq = Q_TILE
    nt = seq // tq
    d = x2.shape[1]
    mats = _scan_matrices(HG_CHUNK)
    cur = lambda i: jnp.minimum(i, nt - 1)
    tok = lambda b, i: (b * nt + cur(i), 0)
    const = lambda b, i: (0, 0)
    rows = lambda b, i: (b, 0, cur(i), 0)
    cols = lambda b, i: (b, 0, 0, cur(i))
    grp = NSA_GROUPS
    return pl.pallas_call(
        functools.partial(_inproj_kernel, n_tiles=nt),
        grid=(batch, nt + 1),
        in_specs=[
            pl.BlockSpec((tq, d), tok),
            pl.BlockSpec((1, d), const),
            pl.BlockSpec(w_nsa.shape, const),
            pl.BlockSpec(w_hg.shape, const),
            pl.BlockSpec((tq, LANES), lambda b, i: (cur(i), 0)),
            pl.BlockSpec((tq, LANES), lambda b, i: (cur(i), 0)),
            pl.BlockSpec((1, LANES), const),
            pl.BlockSpec((1, LANES), const),
            pl.BlockSpec((1, HG_WIDTH), const),
            pl.BlockSpec((1, HG_WIDTH), const),
            pl.BlockSpec(mats.shape, const),
        ],
        out_specs=[
            pl.BlockSpec((tq, HG_WIDTH), lambda b, i: (b * nt + jnp.maximum(i - 1, 0), 0)),
            pl.BlockSpec((None, grp, None, HEAD_DIM, Q_COLS),
                         lambda b, i: (b, 0, cur(i), 0, 0)),
            pl.BlockSpec((None, grp, tq, HEAD_DIM), rows),
            pl.BlockSpec((None, grp, tq, SLC_LANES + HEAD_DIM), rows),
            pl.BlockSpec((None, grp, tq, HEAD_DIM), rows),
            pl.BlockSpec((None, grp, tq, HEAD_DIM), rows),
            pl.BlockSpec((None, grp, V_ROWS, tq), cols),
            pl.BlockSpec((None, grp, V_ROWS, tq), cols),
            pl.BlockSpec((None, grp, GATE_ROWS, tq), cols),
        ],
        out_shape=[
            jax.ShapeDtypeStruct((batch * seq, HG_WIDTH), F32),
            jax.ShapeDtypeStruct((batch, grp, nt, HEAD_DIM, Q_COLS), BF16),
            jax.ShapeDtypeStruct((batch, grp, seq, HEAD_DIM), BF16),
            jax.ShapeDtypeStruct((batch, grp, seq, SLC_LANES + HEAD_DIM), BF16),
            jax.ShapeDtypeStruct((batch, grp, seq, HEAD_DIM), BF16),
            jax.ShapeDtypeStruct((batch, grp, seq, HEAD_DIM), BF16),
            jax.ShapeDtypeStruct((batch, grp, V_ROWS, seq), BF16),
            jax.ShapeDtypeStruct((batch, grp, V_ROWS, seq), BF16),
            jax.ShapeDtypeStruct((batch, grp, GATE_ROWS, seq), F32),
        ],
        scratch_shapes=[
            pltpu.VMEM((tq, w_hg.shape[1]), F32),
            pltpu.VMEM((tq, w_hg.shape[1]), F32),
            pltpu.VMEM((HG_HEADS, HG_DIM, HG_DIM), F32),
        ],
        compiler_params=_cparams(2),
        name="inproj_hgrn",
    )(x2, gain, w_nsa, w_hg, cos_t, sin_t, qg, kg, lb, hgain, mats)


def _compress_kernel(tk_ref, tv_ref, pek_ref, pev_ref, w1k_ref, w1v_ref,
                     w2k_ref, w2v_ref, ok_ref, ov_ref):
    half = CMP_STRIDE * HEAD_DIM

    def run(t_ref, pe_ref, w1_ref, w2_ref, o_ref, transposed):
        t = t_ref[...]
        n_rows = t.shape[0]
        top = jnp.dot(t, w1_ref[0:half, :], preferred_element_type=F32)
        bot = jnp.dot(t, w1_ref[half:2 * half, :], preferred_element_type=F32)
        bot_next = pltpu.roll(bot, n_rows - 1, 0)
        pe = jnp.dot(pe_ref[...], w1_ref[...], preferred_element_type=F32)[0:1, :]
        h = top + bot_next + pe
        h = h * jax.nn.sigmoid(h)
        o = jnp.dot(h.astype(BF16), w2_ref[...], preferred_element_type=F32)
        r = lax.broadcasted_iota(jnp.int32, o.shape, 0)
        o = jnp.where(r < n_rows - 1, o, 0.0)
        if transposed:
            ot = jnp.concatenate([o, jnp.zeros_like(o)], axis=1).T
            o_ref[0:HEAD_DIM, :] = ot[0:HEAD_DIM, :].astype(BF16)
            o_ref[HEAD_DIM:V_ROWS, :] = _value_rows_tail(n_rows)
        else:
            o_ref[...] = o.astype(BF16)

    run(tk_ref, pek_ref, w1k_ref, w2k_ref, ok_ref, False)
    run(tv_ref, pev_ref, w1v_ref, w2v_ref, ov_ref, True)


def _compress(tk, tv, pek, pev, w1k, w1v, w2k, w2v):
    bg, n_rows, width = tk.shape
    blk = lambda i: (i, 0, 0)
    const = lambda i: (0, 0)
    return pl.pallas_call(
        _compress_kernel,
        grid=(bg,),
        in_specs=[
            pl.BlockSpec((None, n_rows, width), blk),
            pl.BlockSpec((None, n_rows, width), blk),
            pl.BlockSpec(pek.shape, const),
            pl.BlockSpec(pev.shape, const),
            pl.BlockSpec(w1k.shape, const),
            pl.BlockSpec(w1v.shape, const),
            pl.BlockSpec(w2k.shape, const),
            pl.BlockSpec(w2v.shape, const),
        ],
        out_specs=[pl.BlockSpec((None, n_rows, HEAD_DIM), blk),
                   pl.BlockSpec((None, V_ROWS, n_rows), blk)],
        out_shape=[jax.ShapeDtypeStruct((bg, n_rows, HEAD_DIM), BF16),
                   jax.ShapeDtypeStruct((bg, V_ROWS, n_rows), BF16)],
        compiler_params=_cparams(1),
        name="compress",
    )(tk, tv, pek, pev, w1k, w1v, w2k, w2v)


def _nsa_attn_kernel(qt_ref, gt_ref, kc_ref, vct_ref, mapt_ref, ks_ref, vst_ref,
                     kw_ref, vwt_ref, y_ref, qaug_ref, s0_ref, s1_ref, c0_ref, c1_ref,
                     m_ref, acc_ref, ocmp_ref, owin_ref):
    i = pl.program_id(2)
    q0 = i * Q_TILE
    qt = qt_ref[...]
    col = lax.broadcasted_iota(jnp.int32, (1, Q_COLS), 1)
    tpos = q0 + (col & (Q_TILE - 1))

    def compressed_branch(n_rows, n_blk):
        kc = kc_ref[0:n_rows, :]
        cmp_end = lax.broadcasted_iota(jnp.int32, (n_rows, 1), 0) * CMP_STRIDE + (CMP_LEN - 1)
        mapt = mapt_ref[0:n_blk, 0:n_rows]
        head_cols = [slice(h * Q_TILE, (h + 1) * Q_TILE) for h in range(HEADS_PER_GROUP)]
        scores = [jnp.dot(kc, qt[:, cols], preferred_element_type=F32) for cols in head_cols]
        imp = None
        for cols, s in zip(head_cols, scores):
            s = jnp.where(cmp_end <= tpos[:, cols], s, NEG)
            m = jnp.max(s, axis=0, keepdims=True)
            p = jnp.exp2(s - jnp.maximum(m, 0.1 * NEG)).astype(BF16)
            acc = jnp.dot(vct_ref[:, 0:n_rows], p, preferred_element_type=F32)
            l = acc[HEAD_DIM:HEAD_DIM + 1, :]
            inv = jnp.where(l > 0.0, 1.0 / l, 0.0)
            ocmp_ref[:, cols] = acc[0:HEAD_DIM, :] * inv
            part = jnp.dot(mapt, p, preferred_element_type=F32) * inv
            imp = part if imp is None else imp + part
        return imp

    def select_blocks(imp, n_blk):
        blk_i = lax.broadcasted_iota(jnp.int32, (n_blk, Q_TILE), 0)
        cur = (q0 + lax.broadcasted_iota(jnp.int32, (n_blk, Q_TILE), 1)) >> SLC_SHIFT
        blk = blk_i.astype(F32)
        sel = jnp.zeros((n_blk, Q_TILE), F32)
        for forced in (blk_i == 0, blk_i == cur, blk_i == cur - 1):
            sel = jnp.where(forced, 1.0, sel)
        val = jnp.where(blk_i <= cur, jnp.where(sel > 0.5, -1.0, imp), -1.0)
        for _ in range(min(SLC_TOPN, n_blk) - 3):
            mx = jnp.max(val, axis=0, keepdims=True)
            first = jnp.min(jnp.where(val == mx, blk, float(n_blk)), axis=0, keepdims=True)
            first = jnp.where(mx > -0.5, first, -1.0)
            pick = blk == first
            sel = jnp.where(pick, 1.0, sel)
            val = jnp.where(pick, -1.0, val)
        sel_bias = jnp.where(sel > 0.5, 0.0, -BIG).astype(BF16)
        for h in range(HEADS_PER_GROUP):
            qaug_ref[0:n_blk, h * Q_TILE:(h + 1) * Q_TILE] = sel_bias
        if n_blk < SLC_LANES:
            qaug_ref[n_blk:SLC_LANES, :] = jnp.full((SLC_LANES - n_blk, Q_COLS), -BIG, BF16)

    col_tiles = [slice(c0, c0 + MXU_COLS) for c0 in range(0, Q_COLS, MXU_COLS)]
    buffers = ((s0_ref, c0_ref), (s1_ref, c1_ref))

    def slc_scores(k0, s_ref, cmax_ref, cols):
        sc = jnp.dot(ks_ref[pl.ds(k0, SLC_KEYS), :], qaug_ref[:, cols],
                     preferred_element_type=F32)
        s_ref[:, cols] = sc
        cmax_ref[:, cols] = jnp.max(sc, axis=0, keepdims=True)

    span = WINDOW + Q_TILE
    t_loc = col & (Q_TILE - 1)

    def window_scores(interior):
        if not interior:
            sw = jnp.dot(kw_ref[0:span, :], qt, preferred_element_type=F32)
            kpos = lax.broadcasted_iota(jnp.int32, (span, 1), 0)
            sw = jnp.where(kpos <= tpos, jnp.where(kpos > tpos - WINDOW, sw, NEG), NEG)
            return [sw], 0
        start = pl.multiple_of(q0 - WINDOW, Q_TILE)
        i_loc = lax.broadcasted_iota(jnp.int32, (Q_TILE, 1), 0)
        n_parts = span // Q_TILE
        parts = []
        for r in range(n_parts):
            sw = jnp.dot(kw_ref[pl.ds(start + r * Q_TILE, Q_TILE), :], qt,
                         preferred_element_type=F32)
            if r == 0:
                sw = jnp.where(i_loc > t_loc, sw, NEG)
            elif r == n_parts - 1:
                sw = jnp.where(i_loc <= t_loc, sw, NEG)
            parts.append(sw)
        return parts, start

    def window_finish(parts, v_start):
        mw = None
        for sw in parts:
            cm = jnp.max(sw, axis=0, keepdims=True)
            mw = cm if mw is None else jnp.maximum(mw, cm)
        accw, off = None, 0
        for sw in parts:
            pw = jnp.exp2(sw - mw).astype(BF16)
            d = jnp.dot(vwt_ref[:, pl.ds(v_start + off, sw.shape[0])], pw,
                        preferred_element_type=F32)
            accw = d if accw is None else accw + d
            off += sw.shape[0]
        owin_ref[...] = accw[0:HEAD_DIM, :] / accw[HEAD_DIM:HEAD_DIM + 1, :]

    def front(n_rows, n_blk, interior):
        imp = compressed_branch(n_rows, n_blk)
        window_finish(*window_scores(interior))
        select_blocks(imp, n_blk)
        qaug_ref[SLC_LANES:SLC_LANES + HEAD_DIM, :] = qt
        for cols in col_tiles:
            slc_scores(0, *buffers[0], cols)
        for cols in col_tiles[:-1]:
            slc_scores(SLC_KEYS, *buffers[1], cols)

    n_cmp_rows = kc_ref.shape[0]
    row_steps = list(range(LANES, n_cmp_rows, LANES)) + [n_cmp_rows]
    q_end = q0 + Q_TILE
    lo = 0
    for n_rows in row_steps:
        hi = n_rows * CMP_STRIDE
        n_blk = min(SLC_LANES, -(-(hi // SLC_BLOCK) // 16) * 16)
        cuts = [lo, WINDOW, hi] if lo < WINDOW < hi else [lo, hi]
        for c_lo, c_hi in zip(cuts[:-1], cuts[1:]):
            pl.when(jnp.logical_and(q_end > c_lo, q_end <= c_hi))(
                functools.partial(front, n_rows, n_blk, c_lo >= WINDOW))
        lo = hi

    def slc_update(k0, s_ref, cmax_ref, cols, masked):
        sc = s_ref[:, cols]
        if masked:
            kpos = k0 + lax.broadcasted_iota(jnp.int32, (SLC_KEYS, 1), 0)
            sc = jnp.where(kpos <= tpos[:, cols], sc, -BIG)
            cmax = jnp.max(sc, axis=0, keepdims=True)
        else:
            cmax = cmax_ref[:, cols]
        m_prev = m_ref[:, cols]
        m_new = jnp.maximum(m_prev, cmax)
        pr = jnp.exp2(sc - m_new).astype(BF16)
        pv = jnp.dot(vst_ref[:, pl.ds(k0, SLC_KEYS)], pr, preferred_element_type=F32)
        acc_ref[:, cols] = acc_ref[:, cols] * jnp.exp2(m_prev - m_new) + pv
        m_ref[:, cols] = m_new

    n_full = q0 // SLC_KEYS
    last_tile = ks_ref.shape[0] // SLC_KEYS - 1

    def tile_start(t):
        return pl.multiple_of(jnp.minimum(t, last_tile) * SLC_KEYS, SLC_KEYS)

    even, odd = buffers

    def slc_body(j, carry):
        k_even, k_odd = tile_start(2 * j), tile_start(2 * j + 1)
        k_even_next, k_odd_next = tile_start(2 * j + 2), tile_start(2 * j + 3)
        slc_scores(k_odd, *odd, col_tiles[-1])
        for cols in col_tiles:
            slc_update(k_even, *even, cols, masked=False)
            slc_scores(k_even_next, *even, cols)
        for cols in col_tiles[:-1]:
            slc_update(k_odd, *odd, cols, masked=False)
            slc_scores(k_odd_next, *odd, cols)
        slc_update(k_odd, *odd, col_tiles[-1], masked=False)
        return carry

    m_ref[...] = jnp.full(m_ref.shape, NEG, F32)
    acc_ref[...] = jnp.zeros(acc_ref.shape, F32)
    lax.fori_loop(0, n_full // 2, slc_body, 0)
    k_last = tile_start(n_full)

    @pl.when((n_full & 1) == 0)
    def _():
        for cols in col_tiles:
            slc_update(k_last, *even, cols, masked=True)

    @pl.when((n_full & 1) == 1)
    def _():
        slc_scores(k_last, *odd, col_tiles[-1])
        for cols in col_tiles:
            slc_update(tile_start(n_full - 1), *even, cols, masked=False)
        for cols in col_tiles:
            slc_update(k_last, *odd, cols, masked=True)

    o_slc = acc_ref[0:HEAD_DIM, :] / acc_ref[HEAD_DIM:HEAD_DIM + 1, :]

    g = gt_ref[...]

    def gate_row(branch):
        return jnp.concatenate(
            [g[h * 3 + branch:h * 3 + branch + 1, :] for h in range(HEADS_PER_GROUP)], axis=1)

    y = (gate_row(0) * ocmp_ref[...] + gate_row(1) * o_slc
         + gate_row(2) * owin_ref[...])
    for pair in range(HEADS_PER_GROUP // 2):
        lo, mid, hi = (2 * pair) * Q_TILE, (2 * pair + 1) * Q_TILE, (2 * pair + 2) * Q_TILE
        stacked = jnp.concatenate([y[:, lo:mid], y[:, mid:hi]], axis=0)
        y_ref[:, pair * LANES:(pair + 1) * LANES] = stacked.T


def _nsa_attn(qt, gt, kc, vct, mapt, ks, vst, kw, vwt):
    b, g, nt = qt.shape[:3]
    seq = ks.shape[2]
    n_cmp_rows = kc.shape[2]
    per_bg = lambda bi, gi, i: (bi, gi, 0, 0)
    return pl.pallas_call(
        _nsa_attn_kernel,
        grid=(b, g, nt),
        in_specs=[
            pl.BlockSpec((None, None, None, HEAD_DIM, Q_COLS), lambda bi, gi, i: (bi, gi, i, 0, 0)),
            pl.BlockSpec((None, None, GATE_ROWS, Q_TILE), lambda bi, gi, i: (bi, gi, 0, i)),
            pl.BlockSpec((None, None, n_cmp_rows, HEAD_DIM), per_bg),
            pl.BlockSpec((None, None, V_ROWS, n_cmp_rows), per_bg),
            pl.BlockSpec(mapt.shape, lambda bi, gi, i: (0, 0)),
            pl.BlockSpec((None, None, seq, HEAD_DIM + SLC_LANES), per_bg),
            pl.BlockSpec((None, None, V_ROWS, seq), per_bg),
            pl.BlockSpec((None, None, seq, HEAD_DIM), per_bg),
            pl.BlockSpec((None, None, V_ROWS, seq), per_bg),
        ],
        out_specs=pl.BlockSpec((Q_TILE, HEADS_PER_GROUP * HEAD_DIM),
                               lambda bi, gi, i: (bi * nt + i, gi)),
        out_shape=jax.ShapeDtypeStruct((b * seq, NSA_WIDTH), F32),
        scratch_shapes=[
            pltpu.VMEM((SLC_LANES + HEAD_DIM, Q_COLS), BF16),
            pltpu.VMEM((SLC_KEYS, Q_COLS), F32),
            pltpu.VMEM((SLC_KEYS, Q_COLS), F32),
            pltpu.VMEM((1, Q_COLS), F32),
            pltpu.VMEM((1, Q_COLS), F32),
            pltpu.VMEM((1, Q_COLS), F32),
            pltpu.VMEM((V_ROWS, Q_COLS), F32),
            pltpu.VMEM((HEAD_DIM, Q_COLS), F32),
            pltpu.VMEM((HEAD_DIM, Q_COLS), F32),
        ],
        compiler_params=_cparams(3),
        name="nsa_attn",
    )(qt, gt, kc, vct, mapt, ks, vst, kw, vwt)


SCAN_MXU_SIZES = (2, 4, 8)


def _scan_matrices(rows):
    t = np.arange(rows)[:, None]
    s = np.arange(rows)[None, :]
    mats = [(s // h == t // h) & (s <= t) for h in SCAN_MXU_SIZES]
    mats += [(s // h == t // h) & (s >= t) for h in SCAN_MXU_SIZES]
    return jnp.asarray(np.concatenate([np.tile(m, (1, 3)) for m in mats], axis=0), BF16)


def _segmented_scans(g, mats_ref):
    rows = g.shape[0]
    hi = g.astype(BF16)
    rest = g - hi.astype(F32)
    mid = rest.astype(BF16)
    lo = (rest - mid.astype(F32)).astype(BF16)
    small = jnp.dot(mats_ref[...], jnp.concatenate([hi, mid, lo], axis=0),
                    preferred_element_type=F32)
    n = len(SCAN_MXU_SIZES)
    out = {1: (g, g)}
    for j, h in enumerate(SCAN_MXU_SIZES):
        out[h] = (small[j * rows:(j + 1) * rows], small[(n + j) * rows:(n + j + 1) * rows])
    h = SCAN_MXU_SIZES[-1]
    pre, suf = out[h]
    while h < rows:
        pre_parts, suf_parts = [], []
        for b in range(rows // (2 * h)):
            lo_r, mid_r, hi_r = b * 2 * h, b * 2 * h + h, (b + 1) * 2 * h
            pre_parts += [pre[lo_r:mid_r], pre[mid_r:hi_r] + pre[mid_r - 1:mid_r, :]]
            suf_parts += [suf[lo_r:mid_r] + suf[mid_r:mid_r + 1, :], suf[mid_r:hi_r]]
        pre = jnp.concatenate(pre_parts, axis=0)
        suf = jnp.concatenate(suf_parts, axis=0)
        h *= 2
        out[h] = (pre, suf)
    return out


def _hgrn_tile(hg_ref, lb_ref, gain_ref, mats_ref, o_ref, state_ref):
    rows = HG_CHUNK
    t_idx = lax.broadcasted_iota(jnp.int32, (rows, rows), 0)
    s_idx = lax.broadcasted_iota(jnp.int32, (rows, rows), 1)
    diff = jnp.where(t_idx > s_idx, t_idx ^ s_idx, 0)
    nt_dims = (((1,), (1,)), ((), ()))
    lb = lb_ref[...]
    gain = gain_ref[...]

    def heads(x):
        return [x[:, h * HG_DIM:(h + 1) * HG_DIM] for h in range(HG_HEADS)]

    def gates_and_scans(tok):
        w = HG_WIDTH
        q = hg_ref[tok, 0:w]
        f = lb + (1.0 - lb) * jax.nn.sigmoid(hg_ref[tok, w:2 * w])
        logf = jnp.log(f)
        return dict(tok=tok, qf=q * jax.nn.sigmoid(q), k=1.0 - f, logf=logf,
                    scans=_segmented_scans(logf, mats_ref))

    def intra_chunk(c):
        qf, k, logf, scans = c["qf"], c["k"], c["logf"], c["scans"]
        qb, kb = heads(qf.astype(BF16)), heads(k.astype(BF16))
        a = [jnp.where(t_idx == s_idx,
                       lax.dot_general(qb[h], kb[h], nt_dims, preferred_element_type=F32), 0.0)
             for h in range(HG_HEADS)]
        level, h_size = 0, 1
        while h_size < rows:
            pre, suf = scans[h_size]
            ql = heads((qf * jnp.exp(pre)).astype(BF16))
            kl = heads((k * jnp.exp(suf - logf)).astype(BF16))
            mask = (diff >> level) == 1
            for h in range(HG_HEADS):
                al = lax.dot_general(ql[h], kl[h], nt_dims, preferred_element_type=F32)
                a[h] = jnp.where(mask, al, a[h])
            level += 1
            h_size *= 2
            yield
        cum, rev = scans[rows]
        c.update(a=a, cum=cum,
                 q_in=heads((qf * jnp.exp(cum)).astype(BF16)),
                 k_out=heads((k * jnp.exp(rev - logf)).astype(BF16)))
        yield

    def outputs_and_state(c, states):
        tok, w = c["tok"], HG_WIDTH
        v = hg_ref[tok, 2 * w:3 * w]
        og = hg_ref[tok, 3 * w:4 * w]
        vh = heads(v.astype(BF16))
        new_states = []
        for h in range(HG_HEADS):
            lanes = slice(h * HG_DIM, (h + 1) * HG_DIM)
            state_t = states[h]
            o = jnp.dot(c["a"][h].astype(BF16), vh[h], preferred_element_type=F32)
            o = o + lax.dot_general(c["q_in"][h], state_t.astype(BF16), nt_dims,
                                    preferred_element_type=F32)
            decay = jnp.exp(c["cum"][rows - 1:rows, lanes])
            v_t = v[:, lanes].T.astype(BF16)
            new_states.append(
                state_t * decay + jnp.dot(v_t, c["k_out"][h], preferred_element_type=F32))
            ms = jnp.mean(o * o, axis=-1, keepdims=True)
            gate = og[:, lanes]
            o_ref[tok, lanes] = (o * lax.rsqrt(ms + EPS) * gain[:, lanes]
                                 * (gate * jax.nn.sigmoid(gate)))
        return new_states

    toks = [slice(j * rows, (j + 1) * rows) for j in range(hg_ref.shape[0] // rows)]
    chunks = []
    for tok in toks:
        chunks.append(gates_and_scans(tok))
        yield
    for c in chunks:
        yield from intra_chunk(c)
    states = [state_ref[h] for h in range(HG_HEADS)]
    for c in chunks:
        states = outputs_and_state(c, states)
        yield
    for h in range(HG_HEADS):
        state_ref[h] = states[h]


def _out_mlp_kernel(x_ref, yn_ref, yh_ref, gn_ref, gm_ref, wa_ref, wb_ref,
                    w1_ref, w2_ref, o_ref, *, ff_chunk):
    yn = yn_ref[...]
    ms = jnp.mean(yn * yn, axis=-1, keepdims=True)
    yn = (yn * lax.rsqrt(ms + EPS) * gn_ref[...]).astype(BF16)
    x1 = (x_ref[...]
          + jnp.dot(yn, wa_ref[...], preferred_element_type=F32)
          + jnp.dot(yh_ref[...].astype(BF16), wb_ref[...], preferred_element_type=F32))
    ms = jnp.mean(x1 * x1, axis=-1, keepdims=True)
    h = (x1 * lax.rsqrt(ms + EPS) * gm_ref[...]).astype(BF16)
    mlp = None
    for c in range(w1_ref.shape[1] // ff_chunk):
        cols = slice(c * ff_chunk, (c + 1) * ff_chunk)
        u = jnp.maximum(jnp.dot(h, w1_ref[:, cols], preferred_element_type=F32), 0.0)
        d = jnp.dot((u * u).astype(BF16), w2_ref[cols, :], preferred_element_type=F32)
        mlp = d if mlp is None else mlp + d
    o_ref[...] = x1 + mlp


def _out_mlp(x2, y_nsa, y_hg, gn, gm, wa, wb, w1, w2, tm=512, ff_chunk=1024):
    n, d = x2.shape
    row = lambda i: (i, 0)
    const = lambda i: (0, 0)

    def resident(shape):
        return pl.BlockSpec(shape, const, pipeline_mode=pl.Buffered(1))

    return pl.pallas_call(
        functools.partial(_out_mlp_kernel, ff_chunk=ff_chunk),
        grid=(n // tm,),
        in_specs=[
            pl.BlockSpec((tm, d), row),
            pl.BlockSpec((tm, y_nsa.shape[1]), row),
            pl.BlockSpec((tm, y_hg.shape[1]), row),
            pl.BlockSpec((1, y_nsa.shape[1]), const),
            pl.BlockSpec((1, d), const),
            resident(wa.shape),
            resident(wb.shape),
            resident(w1.shape),
            resident(w2.shape),
        ],
        out_specs=pl.BlockSpec((tm, d), row),
        out_shape=jax.ShapeDtypeStruct((n, d), F32),
        compiler_params=_cparams(1),
        name="out_mlp",
    )(x2, y_nsa, y_hg, gn, gm, wa, wb, w1, w2)


def _rope_tables(seq):
    pos = jnp.arange(seq, dtype=F32)
    inv_freq = ROPE_THETA ** (-jnp.arange(0, ROT_DIM, 2, dtype=F32) / ROT_DIM)
    ang = pos[:, None] * inv_freq[None, :]
    cos, sin = jnp.cos(ang), jnp.sin(ang)
    rest = HEAD_DIM - ROT_DIM
    cos_h = jnp.concatenate([cos, cos, jnp.ones((seq, rest), F32)], axis=-1)
    sin_h = jnp.concatenate([-sin, sin, jnp.zeros((seq, rest), F32)], axis=-1)
    return jnp.tile(cos_h, (1, 2)), jnp.tile(sin_h, (1, 2))


def _cmp_to_slc_t(n_cmp_rows, n_cmp, n_slc):
    tok = np.arange(n_cmp)[:, None] * CMP_STRIDE + np.arange(CMP_LEN)[None, :]
    frac = (tok[:, :, None] // SLC_BLOCK == np.arange(n_slc)[None, None, :]).mean(axis=1)
    out = np.zeros((SLC_LANES, n_cmp_rows), np.float32)
    out[:n_slc, :n_cmp] = frac.T
    return jnp.asarray(out, BF16)


def kernel(x, attn_norm, w_in, q_norm, k_norm, cmp_pe_k, cmp_w1_k, cmp_w2_k, cmp_pe_v,
           cmp_w1_v, cmp_w2_v, nsa_out_norm, hgrn_lb, hgrn_out_norm, w_out, mlp_norm,
           w_mlp_in, w_mlp_out):
    batch, seq, d_model = x.shape
    n = batch * seq
    n_slc = seq // SLC_BLOCK
    n_cmp_rows = seq // CMP_STRIDE
    n_cmp = (seq - CMP_LEN) // CMP_STRIDE + 1
    assert n_slc <= SLC_LANES and seq % SLC_KEYS == 0 and seq >= max(WINDOW + Q_TILE, 2 * SLC_KEYS)
    assert w_in.shape[0] == 1, "single-layer block"

    kvw = KV_WIDTH
    o_q, o_kc, o_vc, o_ks, o_vs, o_kw, o_vw, o_gate = np.cumsum(
        [0, NSA_WIDTH, kvw, kvw, kvw, kvw, kvw, kvw])
    o_hg = o_gate + NSA_HEADS * 3
    w = w_in[0]
    pad = jnp.zeros((d_model, NSA_COLS - (NSA_WIDTH + 6 * kvw + NSA_HEADS * 3)), w.dtype)
    w_nsa = jnp.concatenate(
        [w[:, o_q:o_kc], w[:, o_kc:o_vc], w[:, o_ks:o_vs], w[:, o_kw:o_vw],
         w[:, o_vc:o_ks], w[:, o_vs:o_kw], w[:, o_vw:o_gate], w[:, o_gate:o_hg], pad],
        axis=1).astype(BF16)
    w_hg = w[:, o_hg:].astype(BF16)
    x2 = x.reshape(n, d_model)
    cos_t, sin_t = _rope_tables(seq)
    qg = jnp.tile(q_norm[0], 2)[None, :]
    kg = jnp.tile(k_norm[0], 2)[None, :]
    lb_all = jnp.cumsum(jax.nn.softmax(hgrn_lb.astype(F32), axis=0), axis=0)
    y_hg, qt, kc, ks_aug, kw, vc, vs_t, vw_t, gt = _inproj_hgrn(
        x2, attn_norm[0][None, :], w_nsa, w_hg, cos_t, sin_t, qg, kg,
        lb_all[0][None, :], jnp.tile(hgrn_out_norm[0], HG_HEADS)[None, :], batch, seq)


    blocks = lambda t: t.reshape(batch * NSA_GROUPS, n_cmp_rows, CMP_STRIDE * HEAD_DIM)
    pe8 = lambda pe: jnp.broadcast_to(pe.reshape(1, CMP_LEN * HEAD_DIM), (8, CMP_LEN * HEAD_DIM)).astype(BF16)
    k_cmp, v_cmp_t = _compress(
        blocks(kc), blocks(vc), pe8(cmp_pe_k[0]), pe8(cmp_pe_v[0]),
        cmp_w1_k[0].astype(BF16), cmp_w1_v[0].astype(BF16),
        cmp_w2_k[0].astype(BF16), cmp_w2_v[0].astype(BF16))
    k_cmp = k_cmp.reshape(batch, NSA_GROUPS, n_cmp_rows, HEAD_DIM)
    v_cmp_t = v_cmp_t.reshape(batch, NSA_GROUPS, V_ROWS, n_cmp_rows)
    y_nsa = _nsa_attn(qt, gt, k_cmp, v_cmp_t, _cmp_to_slc_t(n_cmp_rows, n_cmp, n_slc),
                      ks_aug, vs_t, kw, vw_t)

    wo = w_out[0].astype(BF16)
    out = _out_mlp(x2, y_nsa, y_hg, nsa_out_norm[0][None, :], mlp_norm[0][None, :],
                   wo[:NSA_WIDTH], wo[NSA_WIDTH:], w_mlp_in[0].astype(BF16),
                   w_mlp_out[0].astype(BF16))
    return out.reshape(batch, seq, d_model)
```

```python
import functools

import jax
import jax.numpy as jnp
import numpy as np
from jax import lax
from jax.experimental import pallas as pl
from jax.experimental.pallas import tpu as pltpu

F32 = jnp.float32
BF16 = jnp.bfloat16

NSA_HEADS = 8
NSA_GROUPS = 2
HEADS_PER_GROUP = NSA_HEADS // NSA_GROUPS
HEAD_DIM = 64
NSA_WIDTH = NSA_HEADS * HEAD_DIM
KV_WIDTH = NSA_GROUPS * HEAD_DIM
CMP_LEN = 32
CMP_STRIDE = 16
CMP_HIDDEN = 2 * HEAD_DIM
SLC_BLOCK = 64
SLC_SHIFT = 6
SLC_TOPN = 16
WINDOW = 512
ROT_DIM = HEAD_DIM // 4
ROPE_THETA = 500000.0
HG_HEADS = 4
HG_DIM = 128
HG_WIDTH = HG_HEADS * HG_DIM
EPS = 1e-6

LANES = 128
Q_TILE = 512
Q_COLS = HEADS_PER_GROUP * Q_TILE
SLC_KEYS = 512
SLC_LANES = 128
MXU_COLS = 256
V_ROWS = 80
GATE_ROWS = 16
NSA_COLS = 1408
NEG = -1e30
Q_SCALE = HEAD_DIM ** -0.5 * 1.4426950408889634
BIG = 2.0 ** 100
HG_CHUNK = 128
HG_STEP = 512
VMEM_LIMIT = 56 * 1024 * 1024


def _cparams(n_grid):
    return pltpu.CompilerParams(
        dimension_semantics=("arbitrary",) * n_grid, vmem_limit_bytes=VMEM_LIMIT)


def _value_rows_tail(cols):
    r = lax.broadcasted_iota(jnp.int32, (V_ROWS - HEAD_DIM, cols), 0)
    return jnp.where(r == 0, 1.0, 0.0).astype(BF16)


def _inproj_kernel(x_ref, g_ref, wn_ref, wh_ref, cos_ref, sin_ref, qg_ref, kg_ref,
                   hg_ref, qt_ref, kc_ref, ks_ref, kw_ref, vc_ref, vst_ref, vwt_ref, gt_ref,
                   rows_ref):
    tq = x_ref.shape[0]
    x = x_ref[...]
    ms = jnp.mean(x * x, axis=-1, keepdims=True)
    h = (x * lax.rsqrt(ms + EPS) * g_ref[...]).astype(BF16)
    p = jnp.dot(h, wn_ref[...], preferred_element_type=F32)
    hg_ref[...] = jnp.dot(h, wh_ref[...], preferred_element_type=F32)
    cos = cos_ref[...]
    sin = sin_ref[...]
    lane = lax.broadcasted_iota(jnp.int32, (tq, LANES), 1)
    low_head = lane < HEAD_DIM
    dim = lane & (HEAD_DIM - 1)
    first_half = dim < ROT_DIM // 2

    def norm_rope(xs, gain, scale):
        x2 = xs * xs
        s_lo = jnp.sum(jnp.where(low_head, x2, 0.0), axis=-1, keepdims=True)
        s_hi = jnp.sum(jnp.where(low_head, 0.0, x2), axis=-1, keepdims=True)
        ms = jnp.where(low_head, s_lo, s_hi) * (1.0 / HEAD_DIM)
        y = xs * lax.rsqrt(ms + EPS) * gain
        partner = jnp.where(first_half,
                            pltpu.roll(y, LANES - ROT_DIM // 2, 1),
                            pltpu.roll(y, ROT_DIM // 2, 1))
        return (y * cos + partner * sin) * scale

    def split_groups(ref, x, lanes=slice(None)):
        for g in range(NSA_GROUPS):
            ref[g, :, lanes] = x[:, g * HEAD_DIM:(g + 1) * HEAD_DIM].astype(BF16)

    def split_groups_blocked(ref, x):
        rows_ref[...] = x
        n_blocks = tq // CMP_STRIDE
        for l in range(CMP_STRIDE):
            part = rows_ref[pl.ds(l, n_blocks, stride=CMP_STRIDE), :]
            for g in range(NSA_GROUPS):
                ref[g, :, l * HEAD_DIM:(l + 1) * HEAD_DIM] = (
                    part[:, g * HEAD_DIM:(g + 1) * HEAD_DIM].astype(BF16))

    def split_groups_t(ref, x):
        xt = x.T
        for g in range(NSA_GROUPS):
            ref[g, 0:HEAD_DIM, :] = xt[g * HEAD_DIM:(g + 1) * HEAD_DIM, :].astype(BF16)
            ref[g, HEAD_DIM:V_ROWS, :] = _value_rows_tail(tq)

    qg = qg_ref[...]
    kg = kg_ref[...]
    for s in range(NSA_WIDTH // LANES):
        yt = norm_rope(p[:, s * LANES:(s + 1) * LANES], qg, Q_SCALE).T
        g, h0 = divmod(2 * s, HEADS_PER_GROUP)
        for j in range(2):
            qt_ref[g, :, (h0 + j) * tq:(h0 + j + 1) * tq] = (
                yt[j * HEAD_DIM:(j + 1) * HEAD_DIM, :].astype(BF16))
    base = NSA_WIDTH
    split_groups_blocked(kc_ref, norm_rope(p[:, base:base + LANES], kg, 1.0))
    tok = pl.program_id(1) * tq + lax.broadcasted_iota(jnp.int32, (tq, SLC_LANES), 0)
    blk = lax.broadcasted_iota(jnp.int32, (tq, SLC_LANES), 1)
    one_hot = jnp.where((tok >> SLC_SHIFT) == blk, 1.0, 0.0).astype(BF16)
    for g in range(NSA_GROUPS):
        ks_ref[g, :, 0:SLC_LANES] = one_hot
    split_groups(ks_ref, norm_rope(p[:, base + LANES:base + 2 * LANES], kg, 1.0),
                 slice(SLC_LANES, SLC_LANES + HEAD_DIM))
    split_groups(kw_ref, norm_rope(p[:, base + 2 * LANES:base + 3 * LANES], kg, 1.0))
    base += 3 * LANES
    split_groups_blocked(vc_ref, p[:, base:base + LANES])
    split_groups_t(vst_ref, p[:, base + LANES:base + 2 * LANES])
    split_groups_t(vwt_ref, p[:, base + 2 * LANES:base + 3 * LANES])
    base += 3 * LANES
    gates_t = jax.nn.sigmoid(p[:, base:base + LANES]).T
    per_group = HEADS_PER_GROUP * 3
    for g in range(NSA_GROUPS):
        gt_ref[g, 0:per_group, :] = gates_t[g * per_group:(g + 1) * per_group, :]
        gt_ref[g, per_group:GATE_ROWS, :] = jnp.zeros((GATE_ROWS - per_group, tq), F32)


def _inproj(x2, gain, w_nsa, w_hg, cos_t, sin_t, qg, kg, batch, seq):
    tq = Q_TILE
    nt = seq // tq
    d = x2.shape[1]
    tok = lambda b, i: (b * nt + i, 0)
    const = lambda b, i: (0, 0)
    rows = lambda b, i: (b, 0, i, 0)
    cols = lambda b, i: (b, 0, 0, i)
    grp = NSA_GROUPS
    return pl.pallas_call(
        _inproj_kernel,
        grid=(batch, nt),
        in_specs=[
            pl.BlockSpec((tq, d), tok),
            pl.BlockSpec((1, d), const),
            pl.BlockSpec(w_nsa.shape, const),
            pl.BlockSpec(w_hg.shape, const),
            pl.BlockSpec((tq, LANES), lambda b, i: (i, 0)),
            pl.BlockSpec((tq, LANES), lambda b, i: (i, 0)),
            pl.BlockSpec((1, LANES), const),
            pl.BlockSpec((1, LANES), const),
        ],
        out_specs=[
            pl.BlockSpec((tq, w_hg.shape[1]), tok),
            pl.BlockSpec((None, grp, None, HEAD_DIM, Q_COLS), lambda b, i: (b, 0, i, 0, 0)),
            pl.BlockSpec((None, grp, tq // CMP_STRIDE, CMP_STRIDE * HEAD_DIM), rows),
            pl.BlockSpec((None, grp, tq, SLC_LANES + HEAD_DIM), rows),
            pl.BlockSpec((None, grp, tq, HEAD_DIM), rows),
            pl.BlockSpec((None, grp, tq // CMP_STRIDE, CMP_STRIDE * HEAD_DIM), rows),
            pl.BlockSpec((None, grp, V_ROWS, tq), cols),
            pl.BlockSpec((None, grp, V_ROWS, tq), cols),
            pl.BlockSpec((None, grp, GATE_ROWS, tq), cols),
        ],
        out_shape=[
            jax.ShapeDtypeStruct((batch * seq, w_hg.shape[1]), F32),
            jax.ShapeDtypeStruct((batch, grp, nt, HEAD_DIM, Q_COLS), BF16),
            jax.ShapeDtypeStruct((batch, grp, seq // CMP_STRIDE, CMP_STRIDE * HEAD_DIM), BF16),
            jax.ShapeDtypeStruct((batch, grp, seq, SLC_LANES + HEAD_DIM), BF16),
            jax.ShapeDtypeStruct((batch, grp, seq, HEAD_DIM), BF16),
            jax.ShapeDtypeStruct((batch, grp, seq // CMP_STRIDE, CMP_STRIDE * HEAD_DIM), BF16),
            jax.ShapeDtypeStruct((batch, grp, V_ROWS, seq), BF16),
            jax.ShapeDtypeStruct((batch, grp, V_ROWS, seq), BF16),
            jax.ShapeDtypeStruct((batch, grp, GATE_ROWS, seq), F32),
        ],
        scratch_shapes=[pltpu.VMEM((tq, LANES), F32)],
        compiler_params=_cparams(2),
        name="inproj",
    )(x2, gain, w_nsa, w_hg, cos_t, sin_t, qg, kg)


def _compress_kernel(tk_ref, tv_ref, pek_ref, pev_ref, w1k_ref, w1v_ref,
                     w2k_ref, w2v_ref, ok_ref, ov_ref):
    half = CMP_STRIDE * HEAD_DIM

    def run(t_ref, pe_ref, w1_ref, w2_ref, o_ref, transposed):
        t = t_ref[...]
        n_rows = t.shape[0]
        top = jnp.dot(t, w1_ref[0:half, :], preferred_element_type=F32)
        bot = jnp.dot(t, w1_ref[half:2 * half, :], preferred_element_type=F32)
        bot_next = pltpu.roll(bot, n_rows - 1, 0)
        pe = jnp.dot(pe_ref[...], w1_ref[...], preferred_element_type=F32)[0:1, :]
        h = top + bot_next + pe
        h = h * jax.nn.sigmoid(h)
        o = jnp.dot(h.astype(BF16), w2_ref[...], preferred_element_type=F32)
        r = lax.broadcasted_iota(jnp.int32, o.shape, 0)
        o = jnp.where(r < n_rows - 1, o, 0.0)
        if transposed:
            ot = jnp.concatenate([o, jnp.zeros_like(o)], axis=1).T
            o_ref[0:HEAD_DIM, :] = ot[0:HEAD_DIM, :].astype(BF16)
            o_ref[HEAD_DIM:V_ROWS, :] = _value_rows_tail(n_rows)
        else:
            o_ref[...] = o.astype(BF16)

    run(tk_ref, pek_ref, w1k_ref, w2k_ref, ok_ref, False)
    run(tv_ref, pev_ref, w1v_ref, w2v_ref, ov_ref, True)


def _compress(tk, tv, pek, pev, w1k, w1v, w2k, w2v):
    bg, n_rows, width = tk.shape
    blk = lambda i: (i, 0, 0)
    const = lambda i: (0, 0)
    return pl.pallas_call(
        _compress_kernel,
        grid=(bg,),
        in_specs=[
            pl.BlockSpec((None, n_rows, width), blk),
            pl.BlockSpec((None, n_rows, width), blk),
            pl.BlockSpec(pek.shape, const),
            pl.BlockSpec(pev.shape, const),
            pl.BlockSpec(w1k.shape, const),
            pl.BlockSpec(w1v.shape, const),
            pl.BlockSpec(w2k.shape, const),
            pl.BlockSpec(w2v.shape, const),
        ],
        out_specs=[pl.BlockSpec((None, n_rows, HEAD_DIM), blk),
                   pl.BlockSpec((None, V_ROWS, n_rows), blk)],
        out_shape=[jax.ShapeDtypeStruct((bg, n_rows, HEAD_DIM), BF16),
                   jax.ShapeDtypeStruct((bg, V_ROWS, n_rows), BF16)],
        compiler_params=_cparams(1),
        name="compress",
    )(tk, tv, pek, pev, w1k, w1v, w2k, w2v)


def _nsa_attn_kernel(qt_ref, gt_ref, kc_ref, vct_ref, mapt_ref, ks_ref, vst_ref,
                     kw_ref, vwt_ref, y_ref, qaug_ref, s0_ref, s1_ref, c0_ref, c1_ref,
                     m_ref, acc_ref, ocmp_ref, owin_ref):
    i = pl.program_id(2)
    q0 = i * Q_TILE
    qt = qt_ref[...]
    col = lax.broadcasted_iota(jnp.int32, (1, Q_COLS), 1)
    tpos = q0 + (col & (Q_TILE - 1))

    def compressed_branch(n_rows, n_blk):
        kc = kc_ref[0:n_rows, :]
        cmp_end = lax.broadcasted_iota(jnp.int32, (n_rows, 1), 0) * CMP_STRIDE + (CMP_LEN - 1)
        mapt = mapt_ref[0:n_blk, 0:n_rows]
        head_cols = [slice(h * Q_TILE, (h + 1) * Q_TILE) for h in range(HEADS_PER_GROUP)]
        scores = [jnp.dot(kc, qt[:, cols], preferred_element_type=F32) for cols in head_cols]
        imp = None
        for cols, s in zip(head_cols, scores):
            s = jnp.where(cmp_end <= tpos[:, cols], s, NEG)
            m = jnp.max(s, axis=0, keepdims=True)
            p = jnp.exp2(s - jnp.maximum(m, 0.1 * NEG)).astype(BF16)
            acc = jnp.dot(vct_ref[:, 0:n_rows], p, preferred_element_type=F32)
            l = acc[HEAD_DIM:HEAD_DIM + 1, :]
            inv = jnp.where(l > 0.0, 1.0 / l, 0.0)
            ocmp_ref[:, cols] = acc[0:HEAD_DIM, :] * inv
            part = jnp.dot(mapt, p, preferred_element_type=F32) * inv
            imp = part if imp is None else imp + part
        return imp

    def select_blocks(imp, n_blk):
        blk_i = lax.broadcasted_iota(jnp.int32, (n_blk, Q_TILE), 0)
        cur = (q0 + lax.broadcasted_iota(jnp.int32, (n_blk, Q_TILE), 1)) >> SLC_SHIFT
        blk = blk_i.astype(F32)
        sel = jnp.zeros((n_blk, Q_TILE), F32)
        for forced in (blk_i == 0, blk_i == cur, blk_i == cur - 1):
            sel = jnp.where(forced, 1.0, sel)
        val = jnp.where(blk_i <= cur, jnp.where(sel > 0.5, -1.0, imp), -1.0)
        for _ in range(min(SLC_TOPN, n_blk) - 3):
            mx = jnp.max(val, axis=0, keepdims=True)
            first = jnp.min(jnp.where(val == mx, blk, float(n_blk)), axis=0, keepdims=True)
            first = jnp.where(mx > -0.5, first, -1.0)
            pick = blk == first
            sel = jnp.where(pick, 1.0, sel)
            val = jnp.where(pick, -1.0, val)
        sel_bias = jnp.where(sel > 0.5, 0.0, -BIG).astype(BF16)
        for h in range(HEADS_PER_GROUP):
            qaug_ref[0:n_blk, h * Q_TILE:(h + 1) * Q_TILE] = sel_bias
        if n_blk < SLC_LANES:
            qaug_ref[n_blk:SLC_LANES, :] = jnp.full((SLC_LANES - n_blk, Q_COLS), -BIG, BF16)

    col_tiles = [slice(c0, c0 + MXU_COLS) for c0 in range(0, Q_COLS, MXU_COLS)]
    buffers = ((s0_ref, c0_ref), (s1_ref, c1_ref))

    def slc_scores(k0, s_ref, cmax_ref, cols):
        sc = jnp.dot(ks_ref[pl.ds(k0, SLC_KEYS), :], qaug_ref[:, cols],
                     preferred_element_type=F32)
        s_ref[:, cols] = sc
        cmax_ref[:, cols] = jnp.max(sc, axis=0, keepdims=True)

    span = WINDOW + Q_TILE
    t_loc = col & (Q_TILE - 1)

    def window_scores(interior):
        if not interior:
            sw = jnp.dot(kw_ref[0:span, :], qt, preferred_element_type=F32)
            kpos = lax.broadcasted_iota(jnp.int32, (span, 1), 0)
            sw = jnp.where(kpos <= tpos, jnp.where(kpos > tpos - WINDOW, sw, NEG), NEG)
            return [sw], 0
        start = pl.multiple_of(q0 - WINDOW, Q_TILE)
        i_loc = lax.broadcasted_iota(jnp.int32, (Q_TILE, 1), 0)
        n_parts = span // Q_TILE
        parts = []
        for r in range(n_parts):
            sw = jnp.dot(kw_ref[pl.ds(start + r * Q_TILE, Q_TILE), :], qt,
                         preferred_element_type=F32)
            if r == 0:
                sw = jnp.where(i_loc > t_loc, sw, NEG)
            elif r == n_parts - 1:
                sw = jnp.where(i_loc <= t_loc, sw, NEG)
            parts.append(sw)
        return parts, start

    def window_finish(parts, v_start):
        mw = None
        for sw in parts:
            cm = jnp.max(sw, axis=0, keepdims=True)
            mw = cm if mw is None else jnp.maximum(mw, cm)
        accw, off = None, 0
        for sw in parts:
            pw = jnp.exp2(sw - mw).astype(BF16)
            d = jnp.dot(vwt_ref[:, pl.ds(v_start + off, sw.shape[0])], pw,
                        preferred_element_type=F32)
            accw = d if accw is None else accw + d
            off += sw.shape[0]
        owin_ref[...] = accw[0:HEAD_DIM, :] / accw[HEAD_DIM:HEAD_DIM + 1, :]

    def front(n_rows, n_blk, interior):
        imp = compressed_branch(n_rows, n_blk)
        window_finish(*window_scores(interior))
        select_blocks(imp, n_blk)
        qaug_ref[SLC_LANES:SLC_LANES + HEAD_DIM, :] = qt
        for cols in col_tiles:
            slc_scores(0, *buffers[0], cols)
        for cols in col_tiles[:-1]:
            slc_scores(SLC_KEYS, *buffers[1], cols)

    n_cmp_rows = kc_ref.shape[0]
    row_steps = list(range(LANES, n_cmp_rows, LANES)) + [n_cmp_rows]
    q_end = q0 + Q_TILE
    lo = 0
    for n_rows in row_steps:
        hi = n_rows * CMP_STRIDE
        n_blk = min(SLC_LANES, -(-(hi // SLC_BLOCK) // 16) * 16)
        cuts = [lo, WINDOW, hi] if lo < WINDOW < hi else [lo, hi]
        for c_lo, c_hi in zip(cuts[:-1], cuts[1:]):
            pl.when(jnp.logical_and(q_end > c_lo, q_end <= c_hi))(
                functools.partial(front, n_rows, n_blk, c_lo >= WINDOW))
        lo = hi

    def slc_update(k0, s_ref, cmax_ref, cols, masked):
        sc = s_ref[:, cols]
        if masked:
            kpos = k0 + lax.broadcasted_iota(jnp.int32, (SLC_KEYS, 1), 0)
            sc = jnp.where(kpos <= tpos[:, cols], sc, -BIG)
            cmax = jnp.max(sc, axis=0, keepdims=True)
        else:
            cmax = cmax_ref[:, cols]
        m_prev = m_ref[:, cols]
        m_new = jnp.maximum(m_prev, cmax)
        pr = jnp.exp2(sc - m_new).astype(BF16)
        pv = jnp.dot(vst_ref[:, pl.ds(k0, SLC_KEYS)], pr, preferred_element_type=F32)
        acc_ref[:, cols] = acc_ref[:, cols] * jnp.exp2(m_prev - m_new) + pv
        m_ref[:, cols] = m_new

    n_full = q0 // SLC_KEYS
    last_tile = ks_ref.shape[0] // SLC_KEYS - 1

    def tile_start(t):
        return pl.multiple_of(jnp.minimum(t, last_tile) * SLC_KEYS, SLC_KEYS)

    even, odd = buffers

    def slc_body(j, carry):
        k_even, k_odd = tile_start(2 * j), tile_start(2 * j + 1)
        k_even_next, k_odd_next = tile_start(2 * j + 2), tile_start(2 * j + 3)
        slc_scores(k_odd, *odd, col_tiles[-1])
        for cols in col_tiles:
            slc_update(k_even, *even, cols, masked=False)
            slc_scores(k_even_next, *even, cols)
        for cols in col_tiles[:-1]:
            slc_update(k_odd, *odd, cols, masked=False)
            slc_scores(k_odd_next, *odd, cols)
        slc_update(k_odd, *odd, col_tiles[-1], masked=False)
        return carry

    m_ref[...] = jnp.full(m_ref.shape, NEG, F32)
    acc_ref[...] = jnp.zeros(acc_ref.shape, F32)
    lax.fori_loop(0, n_full // 2, slc_body, 0)
    k_last = tile_start(n_full)

    @pl.when((n_full & 1) == 0)
    def _():
        for cols in col_tiles:
            slc_update(k_last, *even, cols, masked=True)

    @pl.when((n_full & 1) == 1)
    def _():
        slc_scores(k_last, *odd, col_tiles[-1])
        for cols in col_tiles:
            slc_update(tile_start(n_full - 1), *even, cols, masked=False)
        for cols in col_tiles:
            slc_update(k_last, *odd, cols, masked=True)

    o_slc = acc_ref[0:HEAD_DIM, :] / acc_ref[HEAD_DIM:HEAD_DIM + 1, :]

    g = gt_ref[...]

    def gate_row(branch):
        return jnp.concatenate(
            [g[h * 3 + branch:h * 3 + branch + 1, :] for h in range(HEADS_PER_GROUP)], axis=1)

    y = (gate_row(0) * ocmp_ref[...] + gate_row(1) * o_slc
         + gate_row(2) * owin_ref[...])
    for pair in range(HEADS_PER_GROUP // 2):
        lo, mid, hi = (2 * pair) * Q_TILE, (2 * pair + 1) * Q_TILE, (2 * pair + 2) * Q_TILE
        stacked = jnp.concatenate([y[:, lo:mid], y[:, mid:hi]], axis=0)
        y_ref[:, pair * LANES:(pair + 1) * LANES] = stacked.T


def _nsa_attn(qt, gt, kc, vct, mapt, ks, vst, kw, vwt):
    b, g, nt = qt.shape[:3]
    seq = ks.shape[2]
    n_cmp_rows = kc.shape[2]
    per_bg = lambda bi, gi, i: (bi, gi, 0, 0)
    return pl.pallas_call(
        _nsa_attn_kernel,
        grid=(b, g, nt),
        in_specs=[
            pl.BlockSpec((None, None, None, HEAD_DIM, Q_COLS), lambda bi, gi, i: (bi, gi, i, 0, 0)),
            pl.BlockSpec((None, None, GATE_ROWS, Q_TILE), lambda bi, gi, i: (bi, gi, 0, i)),
            pl.BlockSpec((None, None, n_cmp_rows, HEAD_DIM), per_bg),
            pl.BlockSpec((None, None, V_ROWS, n_cmp_rows), per_bg),
            pl.BlockSpec(mapt.shape, lambda bi, gi, i: (0, 0)),
            pl.BlockSpec((None, None, seq, HEAD_DIM + SLC_LANES), per_bg),
            pl.BlockSpec((None, None, V_ROWS, seq), per_bg),
            pl.BlockSpec((None, None, seq, HEAD_DIM), per_bg),
            pl.BlockSpec((None, None, V_ROWS, seq), per_bg),
        ],
        out_specs=pl.BlockSpec((Q_TILE, HEADS_PER_GROUP * HEAD_DIM),
                               lambda bi, gi, i: (bi * nt + i, gi)),
        out_shape=jax.ShapeDtypeStruct((b * seq, NSA_WIDTH), F32),
        scratch_shapes=[
            pltpu.VMEM((SLC_LANES + HEAD_DIM, Q_COLS), BF16),
            pltpu.VMEM((SLC_KEYS, Q_COLS), F32),
            pltpu.VMEM((SLC_KEYS, Q_COLS), F32),
            pltpu.VMEM((1, Q_COLS), F32),
            pltpu.VMEM((1, Q_COLS), F32),
            pltpu.VMEM((1, Q_COLS), F32),
            pltpu.VMEM((V_ROWS, Q_COLS), F32),
            pltpu.VMEM((HEAD_DIM, Q_COLS), F32),
            pltpu.VMEM((HEAD_DIM, Q_COLS), F32),
        ],
        compiler_params=_cparams(3),
        name="nsa_attn",
    )(qt, gt, kc, vct, mapt, ks, vst, kw, vwt)


SCAN_MXU_SIZES = (2, 4, 8)


def _scan_matrices(rows):
    t = np.arange(rows)[:, None]
    s = np.arange(rows)[None, :]
    mats = [(s // h == t // h) & (s <= t) for h in SCAN_MXU_SIZES]
    mats += [(s // h == t // h) & (s >= t) for h in SCAN_MXU_SIZES]
    return jnp.asarray(np.concatenate([np.tile(m, (1, 3)) for m in mats], axis=0), BF16)


def _segmented_scans(g, mats_ref):
    rows = g.shape[0]
    hi = g.astype(BF16)
    rest = g - hi.astype(F32)
    mid = rest.astype(BF16)
    lo = (rest - mid.astype(F32)).astype(BF16)
    small = jnp.dot(mats_ref[...], jnp.concatenate([hi, mid, lo], axis=0),
                    preferred_element_type=F32)
    n = len(SCAN_MXU_SIZES)
    out = {1: (g, g)}
    for j, h in enumerate(SCAN_MXU_SIZES):
        out[h] = (small[j * rows:(j + 1) * rows], small[(n + j) * rows:(n + j + 1) * rows])
    h = SCAN_MXU_SIZES[-1]
    pre, suf = out[h]
    while h < rows:
        pre_parts, suf_parts = [], []
        for b in range(rows // (2 * h)):
            lo_r, mid_r, hi_r = b * 2 * h, b * 2 * h + h, (b + 1) * 2 * h
            pre_parts += [pre[lo_r:mid_r], pre[mid_r:hi_r] + pre[mid_r - 1:mid_r, :]]
            suf_parts += [suf[lo_r:mid_r] + suf[mid_r:mid_r + 1, :], suf[mid_r:hi_r]]
        pre = jnp.concatenate(pre_parts, axis=0)
        suf = jnp.concatenate(suf_parts, axis=0)
        h *= 2
        out[h] = (pre, suf)
    return out


def _hgrn_kernel(hg_ref, lb_ref, gain_ref, mats_ref, o_ref, state_ref):
    c = pl.program_id(1)

    @pl.when(c == 0)
    def _():
        state_ref[...] = jnp.zeros_like(state_ref)

    rows = HG_CHUNK
    t_idx = lax.broadcasted_iota(jnp.int32, (rows, rows), 0)
    s_idx = lax.broadcasted_iota(jnp.int32, (rows, rows), 1)
    diff = jnp.where(t_idx > s_idx, t_idx ^ s_idx, 0)
    nt_dims = (((1,), (1,)), ((), ()))
    lb = lb_ref[...]
    gain = gain_ref[...]

    def heads(x):
        return [x[:, h * HG_DIM:(h + 1) * HG_DIM] for h in range(HG_HEADS)]

    def gates_and_scans(tok):
        w = HG_WIDTH
        q = hg_ref[tok, 0:w]
        f = lb + (1.0 - lb) * jax.nn.sigmoid(hg_ref[tok, w:2 * w])
        logf = jnp.log(f)
        return dict(tok=tok, qf=q * jax.nn.sigmoid(q), k=1.0 - f, logf=logf,
                    scans=_segmented_scans(logf, mats_ref))

    def intra_chunk(c):
        qf, k, logf, scans = c["qf"], c["k"], c["logf"], c["scans"]
        qb, kb = heads(qf.astype(BF16)), heads(k.astype(BF16))
        a = [jnp.where(t_idx == s_idx,
                       lax.dot_general(qb[h], kb[h], nt_dims, preferred_element_type=F32), 0.0)
             for h in range(HG_HEADS)]
        level, h_size = 0, 1
        while h_size < rows:
            pre, suf = scans[h_size]
            ql = heads((qf * jnp.exp(pre)).astype(BF16))
            kl = heads((k * jnp.exp(suf - logf)).astype(BF16))
            mask = (diff >> level) == 1
            for h in range(HG_HEADS):
                al = lax.dot_general(ql[h], kl[h], nt_dims, preferred_element_type=F32)
                a[h] = jnp.where(mask, al, a[h])
            level += 1
            h_size *= 2
        cum, rev = scans[rows]
        c.update(a=a, cum=cum,
                 q_in=heads((qf * jnp.exp(cum)).astype(BF16)),
                 k_out=heads((k * jnp.exp(rev - logf)).astype(BF16)))
        return c

    def outputs_and_state(c, states):
        tok, w = c["tok"], HG_WIDTH
        v = hg_ref[tok, 2 * w:3 * w]
        og = hg_ref[tok, 3 * w:4 * w]
        vh = heads(v.astype(BF16))
        new_states = []
        for h in range(HG_HEADS):
            lanes = slice(h * HG_DIM, (h + 1) * HG_DIM)
            state_t = states[h]
            o = jnp.dot(c["a"][h].astype(BF16), vh[h], preferred_element_type=F32)
            o = o + lax.dot_general(c["q_in"][h], state_t.astype(BF16), nt_dims,
                                    preferred_element_type=F32)
            decay = jnp.exp(c["cum"][rows - 1:rows, lanes])
            v_t = v[:, lanes].T.astype(BF16)
            new_states.append(
                state_t * decay + jnp.dot(v_t, c["k_out"][h], preferred_element_type=F32))
            ms = jnp.mean(o * o, axis=-1, keepdims=True)
            gate = og[:, lanes]
            o_ref[tok, lanes] = (o * lax.rsqrt(ms + EPS) * gain[:, lanes]
                                 * (gate * jax.nn.sigmoid(gate)))
        return new_states

    toks = [slice(j * rows, (j + 1) * rows) for j in range(hg_ref.shape[0] // rows)]
    chunks = [gates_and_scans(tok) for tok in toks]
    chunks = [intra_chunk(c) for c in chunks]
    states = [state_ref[h] for h in range(HG_HEADS)]
    for c in chunks:
        states = outputs_and_state(c, states)
    for h in range(HG_HEADS):
        state_ref[h] = states[h]


def _hgrn(hg_proj, lb, gain, batch, seq):
    n = hg_proj.shape[0]
    n_chunk = seq // HG_STEP
    mats = _scan_matrices(HG_CHUNK)
    row = lambda b, c: (b * n_chunk + c, 0)
    const = lambda b, c: (0, 0)
    return pl.pallas_call(
        _hgrn_kernel,
        grid=(batch, n_chunk),
        in_specs=[
            pl.BlockSpec((HG_STEP, 4 * HG_WIDTH), row),
            pl.BlockSpec((1, HG_WIDTH), const),
            pl.BlockSpec((1, HG_WIDTH), const),
            pl.BlockSpec(mats.shape, const),
        ],
        out_specs=pl.BlockSpec((HG_STEP, HG_WIDTH), row),
        out_shape=jax.ShapeDtypeStruct((n, HG_WIDTH), F32),
        scratch_shapes=[pltpu.VMEM((HG_HEADS, HG_DIM, HG_DIM), F32)],
        compiler_params=_cparams(2),
        name="hgrn",
    )(hg_proj, lb, gain, mats)


def _out_mlp_kernel(x_ref, yn_ref, yh_ref, gn_ref, gm_ref, wa_ref, wb_ref,
                    w1_ref, w2_ref, o_ref, *, ff_chunk):
    yn = yn_ref[...]
    ms = jnp.mean(yn * yn, axis=-1, keepdims=True)
    yn = (yn * lax.rsqrt(ms + EPS) * gn_ref[...]).astype(BF16)
    x1 = (x_ref[...]
          + jnp.dot(yn, wa_ref[...], preferred_element_type=F32)
          + jnp.dot(yh_ref[...].astype(BF16), wb_ref[...], preferred_element_type=F32))
    ms = jnp.mean(x1 * x1, axis=-1, keepdims=True)
    h = (x1 * lax.rsqrt(ms + EPS) * gm_ref[...]).astype(BF16)
    mlp = None
    for c in range(w1_ref.shape[1] // ff_chunk):
        cols = slice(c * ff_chunk, (c + 1) * ff_chunk)
        u = jnp.maximum(jnp.dot(h, w1_ref[:, cols], preferred_element_type=F32), 0.0)
        d = jnp.dot((u * u).astype(BF16), w2_ref[cols, :], preferred_element_type=F32)
        mlp = d if mlp is None else mlp + d
    o_ref[...] = x1 + mlp


def _out_mlp(x2, y_nsa, y_hg, gn, gm, wa, wb, w1, w2, tm=512, ff_chunk=1024):
    n, d = x2.shape
    row = lambda i: (i, 0)
    const = lambda i: (0, 0)

    def resident(shape):
        return pl.BlockSpec(shape, const, pipeline_mode=pl.Buffered(1))

    return pl.pallas_call(
        functools.partial(_out_mlp_kernel, ff_chunk=ff_chunk),
        grid=(n // tm,),
        in_specs=[
            pl.BlockSpec((tm, d), row),
            pl.BlockSpec((tm, y_nsa.shape[1]), row),
            pl.BlockSpec((tm, y_hg.shape[1]), row),
            pl.BlockSpec((1, y_nsa.shape[1]), const),
            pl.BlockSpec((1, d), const),
            resident(wa.shape),
            resident(wb.shape),
            resident(w1.shape),
            resident(w2.shape),
        ],
        out_specs=pl.BlockSpec((tm, d), row),
        out_shape=jax.ShapeDtypeStruct((n, d), F32),
        compiler_params=_cparams(1),
        name="out_mlp",
    )(x2, y_nsa, y_hg, gn, gm, wa, wb, w1, w2)


def _rope_tables(seq):
    pos = jnp.arange(seq, dtype=F32)
    inv_freq = ROPE_THETA ** (-jnp.arange(0, ROT_DIM, 2, dtype=F32) / ROT_DIM)
    ang = pos[:, None] * inv_freq[None, :]
    cos, sin = jnp.cos(ang), jnp.sin(ang)
    rest = HEAD_DIM - ROT_DIM
    cos_h = jnp.concatenate([cos, cos, jnp.ones((seq, rest), F32)], axis=-1)
    sin_h = jnp.concatenate([-sin, sin, jnp.zeros((seq, rest), F32)], axis=-1)
    return jnp.tile(cos_h, (1, 2)), jnp.tile(sin_h, (1, 2))


def _cmp_to_slc_t(n_cmp_rows, n_cmp, n_slc):
    tok = np.arange(n_cmp)[:, None] * CMP_STRIDE + np.arange(CMP_LEN)[None, :]
    frac = (tok[:, :, None] // SLC_BLOCK == np.arange(n_slc)[None, None, :]).mean(axis=1)
    out = np.zeros((SLC_LANES, n_cmp_rows), np.float32)
    out[:n_slc, :n_cmp] = frac.T
    return jnp.asarray(out, BF16)


def kernel(x, attn_norm, w_in, q_norm, k_norm, cmp_pe_k, cmp_w1_k, cmp_w2_k, cmp_pe_v,
           cmp_w1_v, cmp_w2_v, nsa_out_norm, hgrn_lb, hgrn_out_norm, w_out, mlp_norm,
           w_mlp_in, w_mlp_out):
    batch, seq, d_model = x.shape
    n = batch * seq
    n_slc = seq // SLC_BLOCK
    n_cmp_rows = seq // CMP_STRIDE
    n_cmp = (seq - CMP_LEN) // CMP_STRIDE + 1
    assert n_slc <= SLC_LANES and seq % SLC_KEYS == 0 and seq >= max(WINDOW + Q_TILE, 2 * SLC_KEYS)
    assert w_in.shape[0] == 1, "single-layer block"

    kvw = KV_WIDTH
    o_q, o_kc, o_vc, o_ks, o_vs, o_kw, o_vw, o_gate = np.cumsum(
        [0, NSA_WIDTH, kvw, kvw, kvw, kvw, kvw, kvw])
    o_hg = o_gate + NSA_HEADS * 3
    w = w_in[0]
    pad = jnp.zeros((d_model, NSA_COLS - (NSA_WIDTH + 6 * kvw + NSA_HEADS * 3)), w.dtype)
    w_nsa = jnp.concatenate(
        [w[:, o_q:o_kc], w[:, o_kc:o_vc], w[:, o_ks:o_vs], w[:, o_kw:o_vw],
         w[:, o_vc:o_ks], w[:, o_vs:o_kw], w[:, o_vw:o_gate], w[:, o_gate:o_hg], pad],
        axis=1).astype(BF16)
    w_hg = w[:, o_hg:].astype(BF16)
    x2 = x.reshape(n, d_model)
    cos_t, sin_t = _rope_tables(seq)
    qg = jnp.tile(q_norm[0], 2)[None, :]
    kg = jnp.tile(k_norm[0], 2)[None, :]
    hg_proj, qt, kc, ks_aug, kw, vc, vs_t, vw_t, gt = _inproj(
        x2, attn_norm[0][None, :], w_nsa, w_hg, cos_t, sin_t, qg, kg, batch, seq)


    blocks = lambda t: t.reshape(batch * NSA_GROUPS, n_cmp_rows, CMP_STRIDE * HEAD_DIM)
    pe8 = lambda pe: jnp.broadcast_to(pe.reshape(1, CMP_LEN * HEAD_DIM), (8, CMP_LEN * HEAD_DIM)).astype(BF16)
    k_cmp, v_cmp_t = _compress(
        blocks(kc), blocks(vc), pe8(cmp_pe_k[0]), pe8(cmp_pe_v[0]),
        cmp_w1_k[0].astype(BF16), cmp_w1_v[0].astype(BF16),
        cmp_w2_k[0].astype(BF16), cmp_w2_v[0].astype(BF16))
    k_cmp = k_cmp.reshape(batch, NSA_GROUPS, n_cmp_rows, HEAD_DIM)
    v_cmp_t = v_cmp_t.reshape(batch, NSA_GROUPS, V_ROWS, n_cmp_rows)
    y_nsa = _nsa_attn(qt, gt, k_cmp, v_cmp_t, _cmp_to_slc_t(n_cmp_rows, n_cmp, n_slc),
                      ks_aug, vs_t, kw, vw_t)

    lb_all = jnp.cumsum(jax.nn.softmax(hgrn_lb.astype(F32), axis=0), axis=0)
    y_hg = _hgrn(hg_proj, lb_all[0][None, :], jnp.tile(hgrn_out_norm[0], HG_HEADS)[None, :],
                 batch, seq)

    wo = w_out[0].astype(BF16)
    out = _out_mlp(x2, y_nsa, y_hg, nsa_out_norm[0][None, :], mlp_norm[0][None, :],
                   wo[:NSA_WIDTH], wo[NSA_WIDTH:], w_mlp_in[0].astype(BF16),
                   w_mlp_out[0].astype(BF16))
    return out.reshape(batch, seq, d_model)
```

```python
import functools

import jax
import jax.numpy as jnp
import numpy as np
from jax import lax
from jax.experimental import pallas as pl
from jax.experimental.pallas import tpu as pltpu

F32 = jnp.float32
BF16 = jnp.bfloat16

NSA_HEADS = 8
NSA_GROUPS = 2
HEADS_PER_GROUP = NSA_HEADS // NSA_GROUPS
HEAD_DIM = 64
NSA_WIDTH = NSA_HEADS * HEAD_DIM
KV_WIDTH = NSA_GROUPS * HEAD_DIM
CMP_LEN = 32
CMP_STRIDE = 16
CMP_HIDDEN = 2 * HEAD_DIM
SLC_BLOCK = 64
SLC_SHIFT = 6
SLC_TOPN = 16
WINDOW = 512
ROT_DIM = HEAD_DIM // 4
ROPE_THETA = 500000.0
HG_HEADS = 4
HG_DIM = 128
HG_WIDTH = HG_HEADS * HG_DIM
EPS = 1e-6

LANES = 128
Q_TILE = 512
Q_COLS = HEADS_PER_GROUP * Q_TILE
SLC_KEYS = 512
SLC_LANES = 128
MXU_COLS = 256
V_ROWS = 80
GATE_ROWS = 16
NSA_COLS = 1408
NEG = -1e30
Q_SCALE = HEAD_DIM ** -0.5 * 1.4426950408889634
BIG = 2.0 ** 100
HG_CHUNK = 128
HG_STEP = 512
VMEM_LIMIT = 56 * 1024 * 1024


def _cparams(n_grid):
    return pltpu.CompilerParams(
        dimension_semantics=("arbitrary",) * n_grid, vmem_limit_bytes=VMEM_LIMIT)


def _value_rows_tail(cols):
    r = lax.broadcasted_iota(jnp.int32, (V_ROWS - HEAD_DIM, cols), 0)
    return jnp.where(r == 0, 1.0, 0.0).astype(BF16)


def _inproj_kernel(x_ref, g_ref, w_ref, cos_ref, sin_ref, qg_ref, kg_ref,
                   hg_ref, qt_ref, kc_ref, ks_ref, kw_ref, vc_ref, vst_ref, vwt_ref, gt_ref,
                   rows_ref):
    tq = x_ref.shape[0]
    x = x_ref[...]
    ms = jnp.mean(x * x, axis=-1, keepdims=True)
    h = (x * lax.rsqrt(ms + EPS) * g_ref[...]).astype(BF16)
    p = jnp.dot(h, w_ref[:, 0:NSA_COLS], preferred_element_type=F32)
    hg_ref[...] = jnp.dot(h, w_ref[:, NSA_COLS:], preferred_element_type=F32)
    cos = cos_ref[...]
    sin = sin_ref[...]
    lane = lax.broadcasted_iota(jnp.int32, (tq, LANES), 1)
    low_head = lane < HEAD_DIM
    dim = lane & (HEAD_DIM - 1)
    first_half = dim < ROT_DIM // 2

    def norm_rope(xs, gain, scale):
        x2 = xs * xs
        s_lo = jnp.sum(jnp.where(low_head, x2, 0.0), axis=-1, keepdims=True)
        s_hi = jnp.sum(jnp.where(low_head, 0.0, x2), axis=-1, keepdims=True)
        ms = jnp.where(low_head, s_lo, s_hi) * (1.0 / HEAD_DIM)
        y = xs * lax.rsqrt(ms + EPS) * gain
        partner = jnp.where(first_half,
                            pltpu.roll(y, LANES - ROT_DIM // 2, 1),
                            pltpu.roll(y, ROT_DIM // 2, 1))
        return (y * cos + partner * sin) * scale

    def split_groups(ref, x, lanes=slice(None)):
        for g in range(NSA_GROUPS):
            ref[g, :, lanes] = x[:, g * HEAD_DIM:(g + 1) * HEAD_DIM].astype(BF16)

    def split_groups_blocked(ref, x):
        rows_ref[...] = x
        n_blocks = tq // CMP_STRIDE
        for l in range(CMP_STRIDE):
            part = rows_ref[pl.ds(l, n_blocks, stride=CMP_STRIDE), :]
            for g in range(NSA_GROUPS):
                ref[g, :, l * HEAD_DIM:(l + 1) * HEAD_DIM] = (
                    part[:, g * HEAD_DIM:(g + 1) * HEAD_DIM].astype(BF16))

    def split_groups_t(ref, x):
        xt = x.T
        for g in range(NSA_GROUPS):
            ref[g, 0:HEAD_DIM, :] = xt[g * HEAD_DIM:(g + 1) * HEAD_DIM, :].astype(BF16)
            ref[g, HEAD_DIM:V_ROWS, :] = _value_rows_tail(tq)

    qg = qg_ref[...]
    kg = kg_ref[...]
    for s in range(NSA_WIDTH // LANES):
        yt = norm_rope(p[:, s * LANES:(s + 1) * LANES], qg, Q_SCALE).T
        g, h0 = divmod(2 * s, HEADS_PER_GROUP)
        for j in range(2):
            qt_ref[g, :, (h0 + j) * tq:(h0 + j + 1) * tq] = (
                yt[j * HEAD_DIM:(j + 1) * HEAD_DIM, :].astype(BF16))
    base = NSA_WIDTH
    split_groups_blocked(kc_ref, norm_rope(p[:, base:base + LANES], kg, 1.0))
    tok = pl.program_id(1) * tq + lax.broadcasted_iota(jnp.int32, (tq, SLC_LANES), 0)
    blk = lax.broadcasted_iota(jnp.int32, (tq, SLC_LANES), 1)
    one_hot = jnp.where((tok >> SLC_SHIFT) == blk, 1.0, 0.0).astype(BF16)
    for g in range(NSA_GROUPS):
        ks_ref[g, :, 0:SLC_LANES] = one_hot
    split_groups(ks_ref, norm_rope(p[:, base + LANES:base + 2 * LANES], kg, 1.0),
                 slice(SLC_LANES, SLC_LANES + HEAD_DIM))
    split_groups(kw_ref, norm_rope(p[:, base + 2 * LANES:base + 3 * LANES], kg, 1.0))
    base += 3 * LANES
    split_groups_blocked(vc_ref, p[:, base:base + LANES])
    split_groups_t(vst_ref, p[:, base + LANES:base + 2 * LANES])
    split_groups_t(vwt_ref, p[:, base + 2 * LANES:base + 3 * LANES])
    base += 3 * LANES
    gates_t = jax.nn.sigmoid(p[:, base:base + LANES]).T
    per_group = HEADS_PER_GROUP * 3
    for g in range(NSA_GROUPS):
        gt_ref[g, 0:per_group, :] = gates_t[g * per_group:(g + 1) * per_group, :]
        gt_ref[g, per_group:GATE_ROWS, :] = jnp.zeros((GATE_ROWS - per_group, tq), F32)


def _inproj(x2, gain, w_all, cos_t, sin_t, qg, kg, batch, seq):
    tq = Q_TILE
    nt = seq // tq
    d = x2.shape[1]
    hg_cols = w_all.shape[1] - NSA_COLS
    tok = lambda b, i: (b * nt + i, 0)
    const = lambda b, i: (0, 0)
    rows = lambda b, i: (b, 0, i, 0)
    cols = lambda b, i: (b, 0, 0, i)
    grp = NSA_GROUPS
    return pl.pallas_call(
        _inproj_kernel,
        grid=(batch, nt),
        in_specs=[
            pl.BlockSpec((tq, d), tok),
            pl.BlockSpec((1, d), const),
            pl.BlockSpec(w_all.shape, const),
            pl.BlockSpec((tq, LANES), lambda b, i: (i, 0)),
            pl.BlockSpec((tq, LANES), lambda b, i: (i, 0)),
            pl.BlockSpec((1, LANES), const),
            pl.BlockSpec((1, LANES), const),
        ],
        out_specs=[
            pl.BlockSpec((tq, hg_cols), tok),
            pl.BlockSpec((None, grp, None, HEAD_DIM, Q_COLS), lambda b, i: (b, 0, i, 0, 0)),
            pl.BlockSpec((None, grp, tq // CMP_STRIDE, CMP_STRIDE * HEAD_DIM), rows),
            pl.BlockSpec((None, grp, tq, SLC_LANES + HEAD_DIM), rows),
            pl.BlockSpec((None, grp, tq, HEAD_DIM), rows),
            pl.BlockSpec((None, grp, tq // CMP_STRIDE, CMP_STRIDE * HEAD_DIM), rows),
            pl.BlockSpec((None, grp, V_ROWS, tq), cols),
            pl.BlockSpec((None, grp, V_ROWS, tq), cols),
            pl.BlockSpec((None, grp, GATE_ROWS, tq), cols),
        ],
        out_shape=[
            jax.ShapeDtypeStruct((batch * seq, hg_cols), F32),
            jax.ShapeDtypeStruct((batch, grp, nt, HEAD_DIM, Q_COLS), BF16),
            jax.ShapeDtypeStruct((batch, grp, seq // CMP_STRIDE, CMP_STRIDE * HEAD_DIM), BF16),
            jax.ShapeDtypeStruct((batch, grp, seq, SLC_LANES + HEAD_DIM), BF16),
            jax.ShapeDtypeStruct((batch, grp, seq, HEAD_DIM), BF16),
            jax.ShapeDtypeStruct((batch, grp, seq // CMP_STRIDE, CMP_STRIDE * HEAD_DIM), BF16),
            jax.ShapeDtypeStruct((batch, grp, V_ROWS, seq), BF16),
            jax.ShapeDtypeStruct((batch, grp, V_ROWS, seq), BF16),
            jax.ShapeDtypeStruct((batch, grp, GATE_ROWS, seq), F32),
        ],
        scratch_shapes=[pltpu.VMEM((tq, LANES), F32)],
        compiler_params=_cparams(2),
        name="inproj",
    )(x2, gain, w_all, cos_t, sin_t, qg, kg)


def _compress_kernel(tk_ref, tv_ref, pek_ref, pev_ref, w1k_ref, w1v_ref,
                     w2k_ref, w2v_ref, ok_ref, ov_ref):
    half = CMP_STRIDE * HEAD_DIM

    def run(t_ref, pe_ref, w1_ref, w2_ref, o_ref, transposed):
        t = t_ref[...]
        n_rows = t.shape[0]
        top = jnp.dot(t, w1_ref[0:half, :], preferred_element_type=F32)
        bot = jnp.dot(t, w1_ref[half:2 * half, :], preferred_element_type=F32)
        bot_next = pltpu.roll(bot, n_rows - 1, 0)
        pe = jnp.dot(pe_ref[...], w1_ref[...], preferred_element_type=F32)[0:1, :]
        h = top + bot_next + pe
        h = h * jax.nn.sigmoid(h)
        o = jnp.dot(h.astype(BF16), w2_ref[...], preferred_element_type=F32)
        r = lax.broadcasted_iota(jnp.int32, o.shape, 0)
        o = jnp.where(r < n_rows - 1, o, 0.0)
        if transposed:
            ot = jnp.concatenate([o, jnp.zeros_like(o)], axis=1).T
            o_ref[0:HEAD_DIM, :] = ot[0:HEAD_DIM, :].astype(BF16)
            o_ref[HEAD_DIM:V_ROWS, :] = _value_rows_tail(n_rows)
        else:
            o_ref[...] = o.astype(BF16)

    run(tk_ref, pek_ref, w1k_ref, w2k_ref, ok_ref, False)
    run(tv_ref, pev_ref, w1v_ref, w2v_ref, ov_ref, True)


def _compress(tk, tv, pek, pev, w1k, w1v, w2k, w2v):
    bg, n_rows, width = tk.shape
    blk = lambda i: (i, 0, 0)
    const = lambda i: (0, 0)
    return pl.pallas_call(
        _compress_kernel,
        grid=(bg,),
        in_specs=[
            pl.BlockSpec((None, n_rows, width), blk),
            pl.BlockSpec((None, n_rows, width), blk),
            pl.BlockSpec(pek.shape, const),
            pl.BlockSpec(pev.shape, const),
            pl.BlockSpec(w1k.shape, const),
            pl.BlockSpec(w1v.shape, const),
            pl.BlockSpec(w2k.shape, const),
            pl.BlockSpec(w2v.shape, const),
        ],
        out_specs=[pl.BlockSpec((None, n_rows, HEAD_DIM), blk),
                   pl.BlockSpec((None, V_ROWS, n_rows), blk)],
        out_shape=[jax.ShapeDtypeStruct((bg, n_rows, HEAD_DIM), BF16),
                   jax.ShapeDtypeStruct((bg, V_ROWS, n_rows), BF16)],
        compiler_params=_cparams(1),
        name="compress",
    )(tk, tv, pek, pev, w1k, w1v, w2k, w2v)


def _nsa_attn_kernel(qt_ref, gt_ref, kc_ref, vct_ref, mapt_ref, ks_ref, vst_ref,
                     kw_ref, vwt_ref, y_ref, qaug_ref, s0_ref, s1_ref, c0_ref, c1_ref,
                     m_ref, acc_ref, ocmp_ref, owin_ref):
    i = pl.program_id(2)
    q0 = i * Q_TILE
    qt = qt_ref[...]
    col = lax.broadcasted_iota(jnp.int32, (1, Q_COLS), 1)
    tpos = q0 + (col & (Q_TILE - 1))

    def compressed_branch(n_rows, n_blk):
        kc = kc_ref[0:n_rows, :]
        cmp_end = lax.broadcasted_iota(jnp.int32, (n_rows, 1), 0) * CMP_STRIDE + (CMP_LEN - 1)
        mapt = mapt_ref[0:n_blk, 0:n_rows]
        head_cols = [slice(h * Q_TILE, (h + 1) * Q_TILE) for h in range(HEADS_PER_GROUP)]
        scores = [jnp.dot(kc, qt[:, cols], preferred_element_type=F32) for cols in head_cols]
        imp = None
        for cols, s in zip(head_cols, scores):
            s = jnp.where(cmp_end <= tpos[:, cols], s, NEG)
            m = jnp.max(s, axis=0, keepdims=True)
            p = jnp.exp2(s - jnp.maximum(m, 0.1 * NEG)).astype(BF16)
            acc = jnp.dot(vct_ref[:, 0:n_rows], p, preferred_element_type=F32)
            l = acc[HEAD_DIM:HEAD_DIM + 1, :]
            inv = jnp.where(l > 0.0, 1.0 / l, 0.0)
            ocmp_ref[:, cols] = acc[0:HEAD_DIM, :] * inv
            part = jnp.dot(mapt, p, preferred_element_type=F32) * inv
            imp = part if imp is None else imp + part
        return imp

    def select_blocks(imp, n_blk):
        blk_i = lax.broadcasted_iota(jnp.int32, (n_blk, Q_TILE), 0)
        cur = (q0 + lax.broadcasted_iota(jnp.int32, (n_blk, Q_TILE), 1)) >> SLC_SHIFT
        blk = blk_i.astype(F32)
        sel = jnp.zeros((n_blk, Q_TILE), F32)
        for forced in (blk_i == 0, blk_i == cur, blk_i == cur - 1):
            sel = jnp.where(forced, 1.0, sel)
        val = jnp.where(blk_i <= cur, jnp.where(sel > 0.5, -1.0, imp), -1.0)
        for _ in range(min(SLC_TOPN, n_blk) - 3):
            mx = jnp.max(val, axis=0, keepdims=True)
            first = jnp.min(jnp.where(val == mx, blk, float(n_blk)), axis=0, keepdims=True)
            first = jnp.where(mx > -0.5, first, -1.0)
            pick = blk == first
            sel = jnp.where(pick, 1.0, sel)
            val = jnp.where(pick, -1.0, val)
        sel_bias = jnp.where(sel > 0.5, 0.0, -BIG).astype(BF16)
        for h in range(HEADS_PER_GROUP):
            qaug_ref[0:n_blk, h * Q_TILE:(h + 1) * Q_TILE] = sel_bias
        if n_blk < SLC_LANES:
            qaug_ref[n_blk:SLC_LANES, :] = jnp.full((SLC_LANES - n_blk, Q_COLS), -BIG, BF16)

    col_tiles = [slice(c0, c0 + MXU_COLS) for c0 in range(0, Q_COLS, MXU_COLS)]
    buffers = ((s0_ref, c0_ref), (s1_ref, c1_ref))

    def slc_scores(k0, s_ref, cmax_ref, cols):
        sc = jnp.dot(ks_ref[pl.ds(k0, SLC_KEYS), :], qaug_ref[:, cols],
                     preferred_element_type=F32)
        s_ref[:, cols] = sc
        cmax_ref[:, cols] = jnp.max(sc, axis=0, keepdims=True)

    span = WINDOW + Q_TILE
    t_loc = col & (Q_TILE - 1)

    def window_scores(interior):
        if not interior:
            sw = jnp.dot(kw_ref[0:span, :], qt, preferred_element_type=F32)
            kpos = lax.broadcasted_iota(jnp.int32, (span, 1), 0)
            sw = jnp.where(kpos <= tpos, jnp.where(kpos > tpos - WINDOW, sw, NEG), NEG)
            return [sw], 0
        start = pl.multiple_of(q0 - WINDOW, Q_TILE)
        i_loc = lax.broadcasted_iota(jnp.int32, (Q_TILE, 1), 0)
        n_parts = span // Q_TILE
        parts = []
        for r in range(n_parts):
            sw = jnp.dot(kw_ref[pl.ds(start + r * Q_TILE, Q_TILE), :], qt,
                         preferred_element_type=F32)
            if r == 0:
                sw = jnp.where(i_loc > t_loc, sw, NEG)
            elif r == n_parts - 1:
                sw = jnp.where(i_loc <= t_loc, sw, NEG)
            parts.append(sw)
        return parts, start

    def window_finish(parts, v_start):
        mw = None
        for sw in parts:
            cm = jnp.max(sw, axis=0, keepdims=True)
            mw = cm if mw is None else jnp.maximum(mw, cm)
        accw, off = None, 0
        for sw in parts:
            pw = jnp.exp2(sw - mw).astype(BF16)
            d = jnp.dot(vwt_ref[:, pl.ds(v_start + off, sw.shape[0])], pw,
                        preferred_element_type=F32)
            accw = d if accw is None else accw + d
            off += sw.shape[0]
        owin_ref[...] = accw[0:HEAD_DIM, :] / accw[HEAD_DIM:HEAD_DIM + 1, :]

    def front(n_rows, n_blk, interior):
        imp = compressed_branch(n_rows, n_blk)
        window_finish(*window_scores(interior))
        select_blocks(imp, n_blk)
        qaug_ref[SLC_LANES:SLC_LANES + HEAD_DIM, :] = qt
        for cols in col_tiles:
            slc_scores(0, *buffers[0], cols)
        for cols in col_tiles[:-1]:
            slc_scores(SLC_KEYS, *buffers[1], cols)

    n_cmp_rows = kc_ref.shape[0]
    row_steps = list(range(LANES, n_cmp_rows, LANES)) + [n_cmp_rows]
    q_end = q0 + Q_TILE
    lo = 0
    for n_rows in row_steps:
        hi = n_rows * CMP_STRIDE
        n_blk = min(SLC_LANES, -(-(hi // SLC_BLOCK) // 16) * 16)
        cuts = [lo, WINDOW, hi] if lo < WINDOW < hi else [lo, hi]
        for c_lo, c_hi in zip(cuts[:-1], cuts[1:]):
            pl.when(jnp.logical_and(q_end > c_lo, q_end <= c_hi))(
                functools.partial(front, n_rows, n_blk, c_lo >= WINDOW))
        lo = hi

    def slc_update(k0, s_ref, cmax_ref, cols, masked):
        sc = s_ref[:, cols]
        if masked:
            kpos = k0 + lax.broadcasted_iota(jnp.int32, (SLC_KEYS, 1), 0)
            sc = jnp.where(kpos <= tpos[:, cols], sc, -BIG)
            cmax = jnp.max(sc, axis=0, keepdims=True)
        else:
            cmax = cmax_ref[:, cols]
        m_prev = m_ref[:, cols]
        m_new = jnp.maximum(m_prev, cmax)
        pr = jnp.exp2(sc - m_new).astype(BF16)
        pv = jnp.dot(vst_ref[:, pl.ds(k0, SLC_KEYS)], pr, preferred_element_type=F32)
        acc_ref[:, cols] = acc_ref[:, cols] * jnp.exp2(m_prev - m_new) + pv
        m_ref[:, cols] = m_new

    n_full = q0 // SLC_KEYS
    last_tile = ks_ref.shape[0] // SLC_KEYS - 1

    def tile_start(t):
        return pl.multiple_of(jnp.minimum(t, last_tile) * SLC_KEYS, SLC_KEYS)

    even, odd = buffers

    def slc_body(j, carry):
        k_even, k_odd = tile_start(2 * j), tile_start(2 * j + 1)
        k_even_next, k_odd_next = tile_start(2 * j + 2), tile_start(2 * j + 3)
        slc_scores(k_odd, *odd, col_tiles[-1])
        for cols in col_tiles:
            slc_update(k_even, *even, cols, masked=False)
            slc_scores(k_even_next, *even, cols)
        for cols in col_tiles[:-1]:
            slc_update(k_odd, *odd, cols, masked=False)
            slc_scores(k_odd_next, *odd, cols)
        slc_update(k_odd, *odd, col_tiles[-1], masked=False)
        return carry

    m_ref[...] = jnp.full(m_ref.shape, NEG, F32)
    acc_ref[...] = jnp.zeros(acc_ref.shape, F32)
    lax.fori_loop(0, n_full // 2, slc_body, 0)
    k_last = tile_start(n_full)

    @pl.when((n_full & 1) == 0)
    def _():
        for cols in col_tiles:
            slc_update(k_last, *even, cols, masked=True)

    @pl.when((n_full & 1) == 1)
    def _():
        slc_scores(k_last, *odd, col_tiles[-1])
        for cols in col_tiles:
            slc_update(tile_start(n_full - 1), *even, cols, masked=False)
        for cols in col_tiles:
            slc_update(k_last, *odd, cols, masked=True)

    o_slc = acc_ref[0:HEAD_DIM, :] / acc_ref[HEAD_DIM:HEAD_DIM + 1, :]

    g = gt_ref[...]

    def gate_row(branch):
        return jnp.concatenate(
            [g[h * 3 + branch:h * 3 + branch + 1, :] for h in range(HEADS_PER_GROUP)], axis=1)

    y = (gate_row(0) * ocmp_ref[...] + gate_row(1) * o_slc
         + gate_row(2) * owin_ref[...])
    for pair in range(HEADS_PER_GROUP // 2):
        lo, mid, hi = (2 * pair) * Q_TILE, (2 * pair + 1) * Q_TILE, (2 * pair + 2) * Q_TILE
        stacked = jnp.concatenate([y[:, lo:mid], y[:, mid:hi]], axis=0)
        y_ref[:, pair * LANES:(pair + 1) * LANES] = stacked.T


def _nsa_attn(qt, gt, kc, vct, mapt, ks, vst, kw, vwt):
    b, g, nt = qt.shape[:3]
    seq = ks.shape[2]
    n_cmp_rows = kc.shape[2]
    per_bg = lambda bi, gi, i: (bi, gi, 0, 0)
    return pl.pallas_call(
        _nsa_attn_kernel,
        grid=(b, g, nt),
        in_specs=[
            pl.BlockSpec((None, None, None, HEAD_DIM, Q_COLS), lambda bi, gi, i: (bi, gi, i, 0, 0)),
            pl.BlockSpec((None, None, GATE_ROWS, Q_TILE), lambda bi, gi, i: (bi, gi, 0, i)),
            pl.BlockSpec((None, None, n_cmp_rows, HEAD_DIM), per_bg),
            pl.BlockSpec((None, None, V_ROWS, n_cmp_rows), per_bg),
            pl.BlockSpec(mapt.shape, lambda bi, gi, i: (0, 0)),
            pl.BlockSpec((None, None, seq, HEAD_DIM + SLC_LANES), per_bg),
            pl.BlockSpec((None, None, V_ROWS, seq), per_bg),
            pl.BlockSpec((None, None, seq, HEAD_DIM), per_bg),
            pl.BlockSpec((None, None, V_ROWS, seq), per_bg),
        ],
        out_specs=pl.BlockSpec((Q_TILE, HEADS_PER_GROUP * HEAD_DIM),
                               lambda bi, gi, i: (bi * nt + i, gi)),
        out_shape=jax.ShapeDtypeStruct((b * seq, NSA_WIDTH), F32),
        scratch_shapes=[
            pltpu.VMEM((SLC_LANES + HEAD_DIM, Q_COLS), BF16),
            pltpu.VMEM((SLC_KEYS, Q_COLS), F32),
            pltpu.VMEM((SLC_KEYS, Q_COLS), F32),
            pltpu.VMEM((1, Q_COLS), F32),
            pltpu.VMEM((1, Q_COLS), F32),
            pltpu.VMEM((1, Q_COLS), F32),
            pltpu.VMEM((V_ROWS, Q_COLS), F32),
            pltpu.VMEM((HEAD_DIM, Q_COLS), F32),
            pltpu.VMEM((HEAD_DIM, Q_COLS), F32),
        ],
        compiler_params=_cparams(3),
        name="nsa_attn",
    )(qt, gt, kc, vct, mapt, ks, vst, kw, vwt)


SCAN_MXU_SIZES = (2, 4, 8)


def _scan_matrices(rows):
    t = np.arange(rows)[:, None]
    s = np.arange(rows)[None, :]
    mats = [(s // h == t // h) & (s <= t) for h in SCAN_MXU_SIZES]
    mats += [(s // h == t // h) & (s >= t) for h in SCAN_MXU_SIZES]
    return jnp.asarray(np.concatenate([np.tile(m, (1, 3)) for m in mats], axis=0), BF16)


def _segmented_scans(g, mats_ref):
    rows = g.shape[0]
    hi = g.astype(BF16)
    rest = g - hi.astype(F32)
    mid = rest.astype(BF16)
    lo = (rest - mid.astype(F32)).astype(BF16)
    small = jnp.dot(mats_ref[...], jnp.concatenate([hi, mid, lo], axis=0),
                    preferred_element_type=F32)
    n = len(SCAN_MXU_SIZES)
    out = {1: (g, g)}
    for j, h in enumerate(SCAN_MXU_SIZES):
        out[h] = (small[j * rows:(j + 1) * rows], small[(n + j) * rows:(n + j + 1) * rows])
    h = SCAN_MXU_SIZES[-1]
    pre, suf = out[h]
    while h < rows:
        pre_parts, suf_parts = [], []
        for b in range(rows // (2 * h)):
            lo_r, mid_r, hi_r = b * 2 * h, b * 2 * h + h, (b + 1) * 2 * h
            pre_parts += [pre[lo_r:mid_r], pre[mid_r:hi_r] + pre[mid_r - 1:mid_r, :]]
            suf_parts += [suf[lo_r:mid_r] + suf[mid_r:mid_r + 1, :], suf[mid_r:hi_r]]
        pre = jnp.concatenate(pre_parts, axis=0)
        suf = jnp.concatenate(suf_parts, axis=0)
        h *= 2
        out[h] = (pre, suf)
    return out


def _hgrn_kernel(hg_ref, lb_ref, gain_ref, mats_ref, o_ref, state_ref):
    c = pl.program_id(1)

    @pl.when(c == 0)
    def _():
        state_ref[...] = jnp.zeros_like(state_ref)

    rows = HG_CHUNK
    t_idx = lax.broadcasted_iota(jnp.int32, (rows, rows), 0)
    s_idx = lax.broadcasted_iota(jnp.int32, (rows, rows), 1)
    diff = jnp.where(t_idx > s_idx, t_idx ^ s_idx, 0)
    nt_dims = (((1,), (1,)), ((), ()))
    lb = lb_ref[...]
    gain = gain_ref[...]

    def heads(x):
        return [x[:, h * HG_DIM:(h + 1) * HG_DIM] for h in range(HG_HEADS)]

    def gates_and_scans(tok):
        w = HG_WIDTH
        q = hg_ref[tok, 0:w]
        f = lb + (1.0 - lb) * jax.nn.sigmoid(hg_ref[tok, w:2 * w])
        logf = jnp.log(f)
        return dict(tok=tok, qf=q * jax.nn.sigmoid(q), k=1.0 - f, logf=logf,
                    scans=_segmented_scans(logf, mats_ref))

    def intra_chunk(c):
        qf, k, logf, scans = c["qf"], c["k"], c["logf"], c["scans"]
        qb, kb = heads(qf.astype(BF16)), heads(k.astype(BF16))
        a = [jnp.where(t_idx == s_idx,
                       lax.dot_general(qb[h], kb[h], nt_dims, preferred_element_type=F32), 0.0)
             for h in range(HG_HEADS)]
        level, h_size = 0, 1
        while h_size < rows:
            pre, suf = scans[h_size]
            ql = heads((qf * jnp.exp(pre)).astype(BF16))
            kl = heads((k * jnp.exp(suf - logf)).astype(BF16))
            mask = (diff >> level) == 1
            for h in range(HG_HEADS):
                al = lax.dot_general(ql[h], kl[h], nt_dims, preferred_element_type=F32)
                a[h] = jnp.where(mask, al, a[h])
            level += 1
            h_size *= 2
        cum, rev = scans[rows]
        c.update(a=a, cum=cum,
                 q_in=heads((qf * jnp.exp(cum)).astype(BF16)),
                 k_out=heads((k * jnp.exp(rev - logf)).astype(BF16)))
        return c

    def outputs_and_state(c, states):
        tok, w = c["tok"], HG_WIDTH
        v = hg_ref[tok, 2 * w:3 * w]
        og = hg_ref[tok, 3 * w:4 * w]
        vh = heads(v.astype(BF16))
        new_states = []
        for h in range(HG_HEADS):
            lanes = slice(h * HG_DIM, (h + 1) * HG_DIM)
            state_t = states[h]
            o = jnp.dot(c["a"][h].astype(BF16), vh[h], preferred_element_type=F32)
            o = o + lax.dot_general(c["q_in"][h], state_t.astype(BF16), nt_dims,
                                    preferred_element_type=F32)
            decay = jnp.exp(c["cum"][rows - 1:rows, lanes])
            v_t = v[:, lanes].T.astype(BF16)
            new_states.append(
                state_t * decay + jnp.dot(v_t, c["k_out"][h], preferred_element_type=F32))
            ms = jnp.mean(o * o, axis=-1, keepdims=True)
            gate = og[:, lanes]
            o_ref[tok, lanes] = (o * lax.rsqrt(ms + EPS) * gain[:, lanes]
                                 * (gate * jax.nn.sigmoid(gate)))
        return new_states

    toks = [slice(j * rows, (j + 1) * rows) for j in range(hg_ref.shape[0] // rows)]
    chunks = [gates_and_scans(tok) for tok in toks]
    chunks = [intra_chunk(c) for c in chunks]
    states = [state_ref[h] for h in range(HG_HEADS)]
    for c in chunks:
        states = outputs_and_state(c, states)
    for h in range(HG_HEADS):
        state_ref[h] = states[h]


def _hgrn(hg_proj, lb, gain, batch, seq):
    n = hg_proj.shape[0]
    n_chunk = seq // HG_STEP
    mats = _scan_matrices(HG_CHUNK)
    row = lambda b, c: (b * n_chunk + c, 0)
    const = lambda b, c: (0, 0)
    return pl.pallas_call(
        _hgrn_kernel,
        grid=(batch, n_chunk),
        in_specs=[
            pl.BlockSpec((HG_STEP, 4 * HG_WIDTH), row),
            pl.BlockSpec((1, HG_WIDTH), const),
            pl.BlockSpec((1, HG_WIDTH), const),
            pl.BlockSpec(mats.shape, const),
        ],
        out_specs=pl.BlockSpec((HG_STEP, HG_WIDTH), row),
        out_shape=jax.ShapeDtypeStruct((n, HG_WIDTH), F32),
        scratch_shapes=[pltpu.VMEM((HG_HEADS, HG_DIM, HG_DIM), F32)],
        compiler_params=_cparams(2),
        name="hgrn",
    )(hg_proj, lb, gain, mats)


def _out_mlp_kernel(x_ref, yn_ref, yh_ref, gn_ref, gm_ref, wo_ref,
                    w1_ref, w2_ref, o_ref, *, ff_chunk):
    yn = yn_ref[...]
    n_nsa = yn.shape[1]
    ms = jnp.mean(yn * yn, axis=-1, keepdims=True)
    yn = (yn * lax.rsqrt(ms + EPS) * gn_ref[...]).astype(BF16)
    x1 = (x_ref[...]
          + jnp.dot(yn, wo_ref[0:n_nsa, :], preferred_element_type=F32)
          + jnp.dot(yh_ref[...].astype(BF16), wo_ref[n_nsa:, :], preferred_element_type=F32))
    ms = jnp.mean(x1 * x1, axis=-1, keepdims=True)
    h = (x1 * lax.rsqrt(ms + EPS) * gm_ref[...]).astype(BF16)
    mlp = None
    for c in range(w1_ref.shape[1] // ff_chunk):
        cols = slice(c * ff_chunk, (c + 1) * ff_chunk)
        u = jnp.maximum(jnp.dot(h, w1_ref[:, cols], preferred_element_type=F32), 0.0)
        d = jnp.dot((u * u).astype(BF16), w2_ref[cols, :], preferred_element_type=F32)
        mlp = d if mlp is None else mlp + d
    o_ref[...] = x1 + mlp


def _out_mlp(x2, y_nsa, y_hg, gn, gm, wo, w1, w2, tm=512, ff_chunk=1024):
    n, d = x2.shape
    row = lambda i: (i, 0)
    const = lambda i: (0, 0)

    def resident(shape):
        return pl.BlockSpec(shape, const, pipeline_mode=pl.Buffered(1))

    return pl.pallas_call(
        functools.partial(_out_mlp_kernel, ff_chunk=ff_chunk),
        grid=(n // tm,),
        in_specs=[
            pl.BlockSpec((tm, d), row),
            pl.BlockSpec((tm, y_nsa.shape[1]), row),
            pl.BlockSpec((tm, y_hg.shape[1]), row),
            pl.BlockSpec((1, y_nsa.shape[1]), const),
            pl.BlockSpec((1, d), const),
            resident(wo.shape),
            resident(w1.shape),
            resident(w2.shape),
        ],
        out_specs=pl.BlockSpec((tm, d), row),
        out_shape=jax.ShapeDtypeStruct((n, d), F32),
        compiler_params=_cparams(1),
        name="out_mlp",
    )(x2, y_nsa, y_hg, gn, gm, wo, w1, w2)


def _rope_tables(seq):
    pos = jnp.arange(seq, dtype=F32)
    inv_freq = ROPE_THETA ** (-jnp.arange(0, ROT_DIM, 2, dtype=F32) / ROT_DIM)
    ang = pos[:, None] * inv_freq[None, :]
    cos, sin = jnp.cos(ang), jnp.sin(ang)
    rest = HEAD_DIM - ROT_DIM
    cos_h = jnp.concatenate([cos, cos, jnp.ones((seq, rest), F32)], axis=-1)
    sin_h = jnp.concatenate([-sin, sin, jnp.zeros((seq, rest), F32)], axis=-1)
    return jnp.tile(cos_h, (1, 2)), jnp.tile(sin_h, (1, 2))


def _cmp_to_slc_t(n_cmp_rows, n_cmp, n_slc):
    tok = np.arange(n_cmp)[:, None] * CMP_STRIDE + np.arange(CMP_LEN)[None, :]
    frac = (tok[:, :, None] // SLC_BLOCK == np.arange(n_slc)[None, None, :]).mean(axis=1)
    out = np.zeros((SLC_LANES, n_cmp_rows), np.float32)
    out[:n_slc, :n_cmp] = frac.T
    return jnp.asarray(out, BF16)


def kernel(x, attn_norm, w_in, q_norm, k_norm, cmp_pe_k, cmp_w1_k, cmp_w2_k, cmp_pe_v,
           cmp_w1_v, cmp_w2_v, nsa_out_norm, hgrn_lb, hgrn_out_norm, w_out, mlp_norm,
           w_mlp_in, w_mlp_out):
    batch, seq, d_model = x.shape
    n = batch * seq
    n_slc = seq // SLC_BLOCK
    n_cmp_rows = seq // CMP_STRIDE
    n_cmp = (seq - CMP_LEN) // CMP_STRIDE + 1
    assert n_slc <= SLC_LANES and seq % SLC_KEYS == 0 and seq >= max(WINDOW + Q_TILE, 2 * SLC_KEYS)
    assert w_in.shape[0] == 1, "single-layer block"

    kvw = KV_WIDTH
    o_q, o_kc, o_vc, o_ks, o_vs, o_kw, o_vw, o_gate = np.cumsum(
        [0, NSA_WIDTH, kvw, kvw, kvw, kvw, kvw, kvw])
    o_hg = o_gate + NSA_HEADS * 3
    w = w_in[0]
    pad = jnp.zeros((d_model, NSA_COLS - (NSA_WIDTH + 6 * kvw + NSA_HEADS * 3)), w.dtype)
    w_all = jnp.concatenate(
        [w[:, o_q:o_kc], w[:, o_kc:o_vc], w[:, o_ks:o_vs], w[:, o_kw:o_vw],
         w[:, o_vc:o_ks], w[:, o_vs:o_kw], w[:, o_vw:o_gate], w[:, o_gate:o_hg], pad,
         w[:, o_hg:]], axis=1).astype(BF16)
    x2 = x.reshape(n, d_model)
    cos_t, sin_t = _rope_tables(seq)
    qg = jnp.tile(q_norm[0], 2)[None, :]
    kg = jnp.tile(k_norm[0], 2)[None, :]
    hg_proj, qt, kc, ks_aug, kw, vc, vs_t, vw_t, gt = _inproj(
        x2, attn_norm[0][None, :], w_all, cos_t, sin_t, qg, kg, batch, seq)


    blocks = lambda t: t.reshape(batch * NSA_GROUPS, n_cmp_rows, CMP_STRIDE * HEAD_DIM)
    pe8 = lambda pe: jnp.broadcast_to(pe.reshape(1, CMP_LEN * HEAD_DIM), (8, CMP_LEN * HEAD_DIM)).astype(BF16)
    k_cmp, v_cmp_t = _compress(
        blocks(kc), blocks(vc), pe8(cmp_pe_k[0]), pe8(cmp_pe_v[0]),
        cmp_w1_k[0].astype(BF16), cmp_w1_v[0].astype(BF16),
        cmp_w2_k[0].astype(BF16), cmp_w2_v[0].astype(BF16))
    k_cmp = k_cmp.reshape(batch, NSA_GROUPS, n_cmp_rows, HEAD_DIM)
    v_cmp_t = v_cmp_t.reshape(batch, NSA_GROUPS, V_ROWS, n_cmp_rows)
    y_nsa = _nsa_attn(qt, gt, k_cmp, v_cmp_t, _cmp_to_slc_t(n_cmp_rows, n_cmp, n_slc),
                      ks_aug, vs_t, kw, vw_t)

    lb_all = jnp.cumsum(jax.nn.softmax(hgrn_lb.astype(F32), axis=0), axis=0)
    y_hg = _hgrn(hg_proj, lb_all[0][None, :], jnp.tile(hgrn_out_norm[0], HG_HEADS)[None, :],
                 batch, seq)

    out = _out_mlp(x2, y_nsa, y_hg, nsa_out_norm[0][None, :], mlp_norm[0][None, :],
                   w_out[0].astype(BF16), w_mlp_in[0].astype(BF16),
                   w_mlp_out[0].astype(BF16))
    return out.reshape(batch, seq, d_model)
```

```python
import functools

import jax
import jax.numpy as jnp
import numpy as np
from jax import lax
from jax.experimental import pallas as pl
from jax.experimental.pallas import tpu as pltpu

F32 = jnp.float32
BF16 = jnp.bfloat16

NSA_HEADS = 8
NSA_GROUPS = 2
HEADS_PER_GROUP = NSA_HEADS // NSA_GROUPS
HEAD_DIM = 64
NSA_WIDTH = NSA_HEADS * HEAD_DIM
KV_WIDTH = NSA_GROUPS * HEAD_DIM
CMP_LEN = 32
CMP_STRIDE = 16
CMP_HIDDEN = 2 * HEAD_DIM
SLC_BLOCK = 64
SLC_SHIFT = 6
SLC_TOPN = 16
WINDOW = 512
ROT_DIM = HEAD_DIM // 4
ROPE_THETA = 500000.0
HG_HEADS = 4
HG_DIM = 128
HG_WIDTH = HG_HEADS * HG_DIM
EPS = 1e-6

LANES = 128
Q_TILE = 512
Q_COLS = HEADS_PER_GROUP * Q_TILE
SLC_KEYS = 512
SLC_LANES = 128
MXU_COLS = 256
V_ROWS = 80
GATE_ROWS = 16
NSA_COLS = 1408
NEG = -1e30
Q_SCALE = HEAD_DIM ** -0.5 * 1.4426950408889634
BIG = 2.0 ** 100
HG_CHUNK = 128
HG_STEP = 512
VMEM_LIMIT = 56 * 1024 * 1024


def _cparams(n_grid):
    return pltpu.CompilerParams(
        dimension_semantics=("arbitrary",) * n_grid, vmem_limit_bytes=VMEM_LIMIT)


def _value_rows_tail(cols):
    r = lax.broadcasted_iota(jnp.int32, (V_ROWS - HEAD_DIM, cols), 0)
    return jnp.where(r == 0, 1.0, 0.0).astype(BF16)


def _inproj_kernel(x_ref, g_ref, wn_ref, wh_ref, cos_ref, sin_ref, qg_ref, kg_ref,
                   hg_ref, qt_ref, kc_ref, ks_ref, kw_ref, vc_ref, vst_ref, vwt_ref, gt_ref,
                   rows_ref):
    tq = x_ref.shape[0]
    x = x_ref[...]
    ms = jnp.mean(x * x, axis=-1, keepdims=True)
    h = (x * lax.rsqrt(ms + EPS) * g_ref[...]).astype(BF16)
    p = jnp.dot(h, wn_ref[...], preferred_element_type=F32)
    hg_ref[...] = jnp.dot(h, wh_ref[...], preferred_element_type=F32)
    cos = cos_ref[...]
    sin = sin_ref[...]
    lane = lax.broadcasted_iota(jnp.int32, (tq, LANES), 1)
    low_head = lane < HEAD_DIM
    dim = lane & (HEAD_DIM - 1)
    first_half = dim < ROT_DIM // 2

    def norm_rope(xs, gain, scale):
        x2 = xs * xs
        s_lo = jnp.sum(jnp.where(low_head, x2, 0.0), axis=-1, keepdims=True)
        s_hi = jnp.sum(jnp.where(low_head, 0.0, x2), axis=-1, keepdims=True)
        ms = jnp.where(low_head, s_lo, s_hi) * (1.0 / HEAD_DIM)
        y = xs * lax.rsqrt(ms + EPS) * gain
        partner = jnp.where(first_half,
                            pltpu.roll(y, LANES - ROT_DIM // 2, 1),
                            pltpu.roll(y, ROT_DIM // 2, 1))
        return (y * cos + partner * sin) * scale

    def split_groups(ref, x, lanes=slice(None)):
        for g in range(NSA_GROUPS):
            ref[g, :, lanes] = x[:, g * HEAD_DIM:(g + 1) * HEAD_DIM].astype(BF16)

    def split_groups_blocked(ref, x):
        rows_ref[...] = x
        n_blocks = tq // CMP_STRIDE
        for l in range(CMP_STRIDE):
            part = rows_ref[pl.ds(l, n_blocks, stride=CMP_STRIDE), :]
            for g in range(NSA_GROUPS):
                ref[g, :, l * HEAD_DIM:(l + 1) * HEAD_DIM] = (
                    part[:, g * HEAD_DIM:(g + 1) * HEAD_DIM].astype(BF16))

    def split_groups_t(ref, x):
        xt = x.T
        for g in range(NSA_GROUPS):
            ref[g, 0:HEAD_DIM, :] = xt[g * HEAD_DIM:(g + 1) * HEAD_DIM, :].astype(BF16)
            ref[g, HEAD_DIM:V_ROWS, :] = _value_rows_tail(tq)

    qg = qg_ref[...]
    kg = kg_ref[...]
    for s in range(NSA_WIDTH // LANES):
        yt = norm_rope(p[:, s * LANES:(s + 1) * LANES], qg, Q_SCALE).T
        g, h0 = divmod(2 * s, HEADS_PER_GROUP)
        for j in range(2):
            qt_ref[g, :, (h0 + j) * tq:(h0 + j + 1) * tq] = (
                yt[j * HEAD_DIM:(j + 1) * HEAD_DIM, :].astype(BF16))
    base = NSA_WIDTH
    split_groups_blocked(kc_ref, norm_rope(p[:, base:base + LANES], kg, 1.0))
    tok = pl.program_id(1) * tq + lax.broadcasted_iota(jnp.int32, (tq, SLC_LANES), 0)
    blk = lax.broadcasted_iota(jnp.int32, (tq, SLC_LANES), 1)
    one_hot = jnp.where((tok >> SLC_SHIFT) == blk, 1.0, 0.0).astype(BF16)
    for g in range(NSA_GROUPS):
        ks_ref[g, :, 0:SLC_LANES] = one_hot
    split_groups(ks_ref, norm_rope(p[:, base + LANES:base + 2 * LANES], kg, 1.0),
                 slice(SLC_LANES, SLC_LANES + HEAD_DIM))
    split_groups(kw_ref, norm_rope(p[:, base + 2 * LANES:base + 3 * LANES], kg, 1.0))
    base += 3 * LANES
    split_groups_blocked(vc_ref, p[:, base:base + LANES])
    split_groups_t(vst_ref, p[:, base + LANES:base + 2 * LANES])
    split_groups_t(vwt_ref, p[:, base + 2 * LANES:base + 3 * LANES])
    base += 3 * LANES
    gates_t = jax.nn.sigmoid(p[:, base:base + LANES]).T
    per_group = HEADS_PER_GROUP * 3
    for g in range(NSA_GROUPS):
        gt_ref[g, 0:per_group, :] = gates_t[g * per_group:(g + 1) * per_group, :]
        gt_ref[g, per_group:GATE_ROWS, :] = jnp.zeros((GATE_ROWS - per_group, tq), F32)


def _inproj(x2, gain, w_nsa, w_hg, cos_t, sin_t, qg, kg, batch, seq):
    tq = Q_TILE
    nt = seq // tq
    d = x2.shape[1]
    tok = lambda b, i: (b * nt + i, 0)
    const = lambda b, i: (0, 0)
    rows = lambda b, i: (b, 0, i, 0)
    cols = lambda b, i: (b, 0, 0, i)
    grp = NSA_GROUPS
    return pl.pallas_call(
        _inproj_kernel,
        grid=(batch, nt),
        in_specs=[
            pl.BlockSpec((tq, d), tok),
            pl.BlockSpec((1, d), const),
            pl.BlockSpec(w_nsa.shape, const),
            pl.BlockSpec(w_hg.shape, const),
            pl.BlockSpec((tq, LANES), lambda b, i: (i, 0)),
            pl.BlockSpec((tq, LANES), lambda b, i: (i, 0)),
            pl.BlockSpec((1, LANES), const),
            pl.BlockSpec((1, LANES), const),
        ],
        out_specs=[
            pl.BlockSpec((tq, w_hg.shape[1]), tok),
            pl.BlockSpec((None, grp, None, HEAD_DIM, Q_COLS), lambda b, i: (b, 0, i, 0, 0)),
            pl.BlockSpec((None, grp, tq // CMP_STRIDE, CMP_STRIDE * HEAD_DIM), rows),
            pl.BlockSpec((None, grp, tq, SLC_LANES + HEAD_DIM), rows),
            pl.BlockSpec((None, grp, tq, HEAD_DIM), rows),
            pl.BlockSpec((None, grp, tq // CMP_STRIDE, CMP_STRIDE * HEAD_DIM), rows),
            pl.BlockSpec((None, grp, V_ROWS, tq), cols),
            pl.BlockSpec((None, grp, V_ROWS, tq), cols),
            pl.BlockSpec((None, grp, GATE_ROWS, tq), cols),
        ],
        out_shape=[
            jax.ShapeDtypeStruct((batch * seq, w_hg.shape[1]), F32),
            jax.ShapeDtypeStruct((batch, grp, nt, HEAD_DIM, Q_COLS), BF16),
            jax.ShapeDtypeStruct((batch, grp, seq // CMP_STRIDE, CMP_STRIDE * HEAD_DIM), BF16),
            jax.ShapeDtypeStruct((batch, grp, seq, SLC_LANES + HEAD_DIM), BF16),
            jax.ShapeDtypeStruct((batch, grp, seq, HEAD_DIM), BF16),
            jax.ShapeDtypeStruct((batch, grp, seq // CMP_STRIDE, CMP_STRIDE * HEAD_DIM), BF16),
            jax.ShapeDtypeStruct((batch, grp, V_ROWS, seq), BF16),
            jax.ShapeDtypeStruct((batch, grp, V_ROWS, seq), BF16),
            jax.ShapeDtypeStruct((batch, grp, GATE_ROWS, seq), F32),
        ],
        scratch_shapes=[pltpu.VMEM((tq, LANES), F32)],
        compiler_params=_cparams(2),
        name="inproj",
    )(x2, gain, w_nsa, w_hg, cos_t, sin_t, qg, kg)


def _compress_kernel(tk_ref, tv_ref, pek_ref, pev_ref, w1k_ref, w1v_ref,
                     w2k_ref, w2v_ref, ok_ref, ov_ref):
    half = CMP_STRIDE * HEAD_DIM

    def run(t_ref, pe_ref, w1_ref, w2_ref, o_ref, transposed):
        t = t_ref[...]
        n_rows = t.shape[0]
        top = jnp.dot(t, w1_ref[0:half, :], preferred_element_type=F32)
        bot = jnp.dot(t, w1_ref[half:2 * half, :], preferred_element_type=F32)
        bot_next = pltpu.roll(bot, n_rows - 1, 0)
        pe = jnp.dot(pe_ref[...], w1_ref[...], preferred_element_type=F32)[0:1, :]
        h = top + bot_next + pe
        h = h * jax.nn.sigmoid(h)
        o = jnp.dot(h.astype(BF16), w2_ref[...], preferred_element_type=F32)
        r = lax.broadcasted_iota(jnp.int32, o.shape, 0)
        o = jnp.where(r < n_rows - 1, o, 0.0)
        if transposed:
            ot = jnp.concatenate([o, jnp.zeros_like(o)], axis=1).T
            o_ref[0:HEAD_DIM, :] = ot[0:HEAD_DIM, :].astype(BF16)
            o_ref[HEAD_DIM:V_ROWS, :] = _value_rows_tail(n_rows)
        else:
            o_ref[...] = o.astype(BF16)

    run(tk_ref, pek_ref, w1k_ref, w2k_ref, ok_ref, False)
    run(tv_ref, pev_ref, w1v_ref, w2v_ref, ov_ref, True)


def _compress(tk, tv, pek, pev, w1k, w1v, w2k, w2v):
    bg, n_rows, width = tk.shape
    blk = lambda i: (i, 0, 0)
    const = lambda i: (0, 0)
    return pl.pallas_call(
        _compress_kernel,
        grid=(bg,),
        in_specs=[
            pl.BlockSpec((None, n_rows, width), blk),
            pl.BlockSpec((None, n_rows, width), blk),
            pl.BlockSpec(pek.shape, const),
            pl.BlockSpec(pev.shape, const),
            pl.BlockSpec(w1k.shape, const),
            pl.BlockSpec(w1v.shape, const),
            pl.BlockSpec(w2k.shape, const),
            pl.BlockSpec(w2v.shape, const),
        ],
        out_specs=[pl.BlockSpec((None, n_rows, HEAD_DIM), blk),
                   pl.BlockSpec((None, V_ROWS, n_rows), blk)],
        out_shape=[jax.ShapeDtypeStruct((bg, n_rows, HEAD_DIM), BF16),
                   jax.ShapeDtypeStruct((bg, V_ROWS, n_rows), BF16)],
        compiler_params=_cparams(1),
        name="compress",
    )(tk, tv, pek, pev, w1k, w1v, w2k, w2v)


def _nsa_attn_kernel(qt_ref, gt_ref, kc_ref, vct_ref, mapt_ref, ks_ref, vst_ref,
                     kw_ref, vwt_ref, y_ref, qaug_ref, s0_ref, s1_ref, c0_ref, c1_ref,
                     m_ref, acc_ref, ocmp_ref, owin_ref):
    i = pl.program_id(2)
    q0 = i * Q_TILE
    qt = qt_ref[...]
    col = lax.broadcasted_iota(jnp.int32, (1, Q_COLS), 1)
    tpos = q0 + (col & (Q_TILE - 1))

    def compressed_branch(n_rows, n_blk):
        kc = kc_ref[0:n_rows, :]
        cmp_end = lax.broadcasted_iota(jnp.int32, (n_rows, 1), 0) * CMP_STRIDE + (CMP_LEN - 1)
        mapt = mapt_ref[0:n_blk, 0:n_rows]
        head_cols = [slice(h * Q_TILE, (h + 1) * Q_TILE) for h in range(HEADS_PER_GROUP)]
        scores = [jnp.dot(kc, qt[:, cols], preferred_element_type=F32) for cols in head_cols]
        imp = None
        for cols, s in zip(head_cols, scores):
            s = jnp.where(cmp_end <= tpos[:, cols], s, NEG)
            m = jnp.max(s, axis=0, keepdims=True)
            p = jnp.exp2(s - jnp.maximum(m, 0.1 * NEG)).astype(BF16)
            acc = jnp.dot(vct_ref[:, 0:n_rows], p, preferred_element_type=F32)
            l = acc[HEAD_DIM:HEAD_DIM + 1, :]
            inv = jnp.where(l > 0.0, 1.0 / l, 0.0)
            ocmp_ref[:, cols] = acc[0:HEAD_DIM, :] * inv
            part = jnp.dot(mapt, p, preferred_element_type=F32) * inv
            imp = part if imp is None else imp + part
        return imp

    def select_blocks(imp, n_blk):
        blk_i = lax.broadcasted_iota(jnp.int32, (n_blk, Q_TILE), 0)
        cur = (q0 + lax.broadcasted_iota(jnp.int32, (n_blk, Q_TILE), 1)) >> SLC_SHIFT
        blk = blk_i.astype(F32)
        sel = jnp.zeros((n_blk, Q_TILE), F32)
        for forced in (blk_i == 0, blk_i == cur, blk_i == cur - 1):
            sel = jnp.where(forced, 1.0, sel)
        val = jnp.where(blk_i <= cur, jnp.where(sel > 0.5, -1.0, imp), -1.0)
        for _ in range(min(SLC_TOPN, n_blk) - 3):
            mx = jnp.max(val, axis=0, keepdims=True)
            first = jnp.min(jnp.where(val == mx, blk, float(n_blk)), axis=0, keepdims=True)
            first = jnp.where(mx > -0.5, first, -1.0)
            val = jnp.where(blk == first, -2.0, val)
        sel_bias = jnp.where(sel > 0.5, 0.0, jnp.where(val < -1.5, 0.0, -BIG)).astype(BF16)
        for h in range(HEADS_PER_GROUP):
            qaug_ref[0:n_blk, h * Q_TILE:(h + 1) * Q_TILE] = sel_bias
        if n_blk < SLC_LANES:
            qaug_ref[n_blk:SLC_LANES, :] = jnp.full((SLC_LANES - n_blk, Q_COLS), -BIG, BF16)

    col_tiles = [slice(c0, c0 + MXU_COLS) for c0 in range(0, Q_COLS, MXU_COLS)]
    buffers = ((s0_ref, c0_ref), (s1_ref, c1_ref))

    def slc_scores(k0, s_ref, cmax_ref, cols):
        sc = jnp.dot(ks_ref[pl.ds(k0, SLC_KEYS), :], qaug_ref[:, cols],
                     preferred_element_type=F32)
        s_ref[:, cols] = sc
        cmax_ref[:, cols] = jnp.max(sc, axis=0, keepdims=True)

    span = WINDOW + Q_TILE
    t_loc = col & (Q_TILE - 1)

    def window_scores(interior):
        if not interior:
            sw = jnp.dot(kw_ref[0:span, :], qt, preferred_element_type=F32)
            kpos = lax.broadcasted_iota(jnp.int32, (span, 1), 0)
            sw = jnp.where(kpos <= tpos, jnp.where(kpos > tpos - WINDOW, sw, NEG), NEG)
            return [sw], 0
        start = pl.multiple_of(q0 - WINDOW, Q_TILE)
        i_loc = lax.broadcasted_iota(jnp.int32, (Q_TILE, 1), 0)
        n_parts = span // Q_TILE
        parts = []
        for r in range(n_parts):
            sw = jnp.dot(kw_ref[pl.ds(start + r * Q_TILE, Q_TILE), :], qt,
                         preferred_element_type=F32)
            if r == 0:
                sw = jnp.where(i_loc > t_loc, sw, NEG)
            elif r == n_parts - 1:
                sw = jnp.where(i_loc <= t_loc, sw, NEG)
            parts.append(sw)
        return parts, start

    def window_finish(parts, v_start):
        mw = None
        for sw in parts:
            cm = jnp.max(sw, axis=0, keepdims=True)
            mw = cm if mw is None else jnp.maximum(mw, cm)
        accw, off = None, 0
        for sw in parts:
            pw = jnp.exp2(sw - mw).astype(BF16)
            d = jnp.dot(vwt_ref[:, pl.ds(v_start + off, sw.shape[0])], pw,
                        preferred_element_type=F32)
            accw = d if accw is None else accw + d
            off += sw.shape[0]
        owin_ref[...] = accw[0:HEAD_DIM, :] / accw[HEAD_DIM:HEAD_DIM + 1, :]

    def front(n_rows, n_blk, interior):
        imp = compressed_branch(n_rows, n_blk)
        parts, v_start = window_scores(interior)
        select_blocks(imp, n_blk)
        qaug_ref[SLC_LANES:SLC_LANES + HEAD_DIM, :] = qt
        for cols in col_tiles:
            slc_scores(0, *buffers[0], cols)
        window_finish(parts, v_start)
        for cols in col_tiles[:-1]:
            slc_scores(SLC_KEYS, *buffers[1], cols)

    n_cmp_rows = kc_ref.shape[0]
    row_steps = list(range(LANES, n_cmp_rows, LANES)) + [n_cmp_rows]
    q_end = q0 + Q_TILE
    lo = 0
    for n_rows in row_steps:
        hi = n_rows * CMP_STRIDE
        n_blk = min(SLC_LANES, -(-(hi // SLC_BLOCK) // 16) * 16)
        cuts = [lo, WINDOW, hi] if lo < WINDOW < hi else [lo, hi]
        for c_lo, c_hi in zip(cuts[:-1], cuts[1:]):
            pl.when(jnp.logical_and(q_end > c_lo, q_end <= c_hi))(
                functools.partial(front, n_rows, n_blk, c_lo >= WINDOW))
        lo = hi

    def slc_update(k0, s_ref, cmax_ref, cols, masked):
        sc = s_ref[:, cols]
        if masked:
            kpos = k0 + lax.broadcasted_iota(jnp.int32, (SLC_KEYS, 1), 0)
            sc = jnp.where(kpos <= tpos[:, cols], sc, -BIG)
            cmax = jnp.max(sc, axis=0, keepdims=True)
        else:
            cmax = cmax_ref[:, cols]
        m_prev = m_ref[:, cols]
        m_new = jnp.maximum(m_prev, cmax)
        pr = jnp.exp2(sc - m_new).astype(BF16)
        pv = jnp.dot(vst_ref[:, pl.ds(k0, SLC_KEYS)], pr, preferred_element_type=F32)
        acc_ref[:, cols] = acc_ref[:, cols] * jnp.exp2(m_prev - m_new) + pv
        m_ref[:, cols] = m_new

    n_full = q0 // SLC_KEYS
    last_tile = ks_ref.shape[0] // SLC_KEYS - 1

    def tile_start(t):
        return pl.multiple_of(jnp.minimum(t, last_tile) * SLC_KEYS, SLC_KEYS)

    even, odd = buffers

    def slc_body(j, carry):
        k_even, k_odd = tile_start(2 * j), tile_start(2 * j + 1)
        k_even_next, k_odd_next = tile_start(2 * j + 2), tile_start(2 * j + 3)
        slc_scores(k_odd, *odd, col_tiles[-1])
        for cols in col_tiles:
            slc_update(k_even, *even, cols, masked=False)
            slc_scores(k_even_next, *even, cols)
        for cols in col_tiles[:-1]:
            slc_update(k_odd, *odd, cols, masked=False)
            slc_scores(k_odd_next, *odd, cols)
        slc_update(k_odd, *odd, col_tiles[-1], masked=False)
        return carry

    m_ref[...] = jnp.full(m_ref.shape, NEG, F32)
    acc_ref[...] = jnp.zeros(acc_ref.shape, F32)
    lax.fori_loop(0, n_full // 2, slc_body, 0)
    k_last = tile_start(n_full)

    @pl.when((n_full & 1) == 0)
    def _():
        for cols in col_tiles:
            slc_update(k_last, *even, cols, masked=True)

    @pl.when((n_full & 1) == 1)
    def _():
        slc_scores(k_last, *odd, col_tiles[-1])
        for cols in col_tiles:
            slc_update(tile_start(n_full - 1), *even, cols, masked=False)
        for cols in col_tiles:
            slc_update(k_last, *odd, cols, masked=True)

    o_slc = acc_ref[0:HEAD_DIM, :] / acc_ref[HEAD_DIM:HEAD_DIM + 1, :]

    g = gt_ref[...]

    def gate_row(branch):
        return jnp.concatenate(
            [g[h * 3 + branch:h * 3 + branch + 1, :] for h in range(HEADS_PER_GROUP)], axis=1)

    y = (gate_row(0) * ocmp_ref[...] + gate_row(1) * o_slc
         + gate_row(2) * owin_ref[...])
    for pair in range(HEADS_PER_GROUP // 2):
        lo, mid, hi = (2 * pair) * Q_TILE, (2 * pair + 1) * Q_TILE, (2 * pair + 2) * Q_TILE
        stacked = jnp.concatenate([y[:, lo:mid], y[:, mid:hi]], axis=0)
        y_ref[:, pair * LANES:(pair + 1) * LANES] = stacked.T


def _nsa_attn(qt, gt, kc, vct, mapt, ks, vst, kw, vwt):
    b, g, nt = qt.shape[:3]
    seq = ks.shape[2]
    n_cmp_rows = kc.shape[2]
    per_bg = lambda bi, gi, i: (bi, gi, 0, 0)
    return pl.pallas_call(
        _nsa_attn_kernel,
        grid=(b, g, nt),
        in_specs=[
            pl.BlockSpec((None, None, None, HEAD_DIM, Q_COLS), lambda bi, gi, i: (bi, gi, i, 0, 0)),
            pl.BlockSpec((None, None, GATE_ROWS, Q_TILE), lambda bi, gi, i: (bi, gi, 0, i)),
            pl.BlockSpec((None, None, n_cmp_rows, HEAD_DIM), per_bg),
            pl.BlockSpec((None, None, V_ROWS, n_cmp_rows), per_bg),
            pl.BlockSpec(mapt.shape, lambda bi, gi, i: (0, 0)),
            pl.BlockSpec((None, None, seq, HEAD_DIM + SLC_LANES), per_bg),
            pl.BlockSpec((None, None, V_ROWS, seq), per_bg),
            pl.BlockSpec((None, None, seq, HEAD_DIM), per_bg),
            pl.BlockSpec((None, None, V_ROWS, seq), per_bg),
        ],
        out_specs=pl.BlockSpec((Q_TILE, HEADS_PER_GROUP * HEAD_DIM),
                               lambda bi, gi, i: (bi * nt + i, gi)),
        out_shape=jax.ShapeDtypeStruct((b * seq, NSA_WIDTH), F32),
        scratch_shapes=[
            pltpu.VMEM((SLC_LANES + HEAD_DIM, Q_COLS), BF16),
            pltpu.VMEM((SLC_KEYS, Q_COLS), F32),
            pltpu.VMEM((SLC_KEYS, Q_COLS), F32),
            pltpu.VMEM((1, Q_COLS), F32),
            pltpu.VMEM((1, Q_COLS), F32),
            pltpu.VMEM((1, Q_COLS), F32),
            pltpu.VMEM((V_ROWS, Q_COLS), F32),
            pltpu.VMEM((HEAD_DIM, Q_COLS), F32),
            pltpu.VMEM((HEAD_DIM, Q_COLS), F32),
        ],
        compiler_params=_cparams(3),
        name="nsa_attn",
    )(qt, gt, kc, vct, mapt, ks, vst, kw, vwt)


SCAN_MXU_SIZES = (2, 4, 8)


def _scan_matrices(rows):
    t = np.arange(rows)[:, None]
    s = np.arange(rows)[None, :]
    mats = [(s // h == t // h) & (s <= t) for h in SCAN_MXU_SIZES]
    mats += [(s // h == t // h) & (s >= t) for h in SCAN_MXU_SIZES]
    return jnp.asarray(np.concatenate([np.tile(m, (1, 3)) for m in mats], axis=0), BF16)


def _segmented_scans(g, mats_ref):
    rows = g.shape[0]
    hi = g.astype(BF16)
    rest = g - hi.astype(F32)
    mid = rest.astype(BF16)
    lo = (rest - mid.astype(F32)).astype(BF16)
    small = jnp.dot(mats_ref[...], jnp.concatenate([hi, mid, lo], axis=0),
                    preferred_element_type=F32)
    n = len(SCAN_MXU_SIZES)
    out = {1: (g, g)}
    for j, h in enumerate(SCAN_MXU_SIZES):
        out[h] = (small[j * rows:(j + 1) * rows], small[(n + j) * rows:(n + j + 1) * rows])
    h = SCAN_MXU_SIZES[-1]
    pre, suf = out[h]
    while h < rows:
        pre_parts, suf_parts = [], []
        for b in range(rows // (2 * h)):
            lo_r, mid_r, hi_r = b * 2 * h, b * 2 * h + h, (b + 1) * 2 * h
            pre_parts += [pre[lo_r:mid_r], pre[mid_r:hi_r] + pre[mid_r - 1:mid_r, :]]
            suf_parts += [suf[lo_r:mid_r] + suf[mid_r:mid_r + 1, :], suf[mid_r:hi_r]]
        pre = jnp.concatenate(pre_parts, axis=0)
        suf = jnp.concatenate(suf_parts, axis=0)
        h *= 2
        out[h] = (pre, suf)
    return out


def _hgrn_kernel(hg_ref, lb_ref, gain_ref, mats_ref, o_ref, state_ref):
    c = pl.program_id(1)

    @pl.when(c == 0)
    def _():
        state_ref[...] = jnp.zeros_like(state_ref)

    rows = HG_CHUNK
    t_idx = lax.broadcasted_iota(jnp.int32, (rows, rows), 0)
    s_idx = lax.broadcasted_iota(jnp.int32, (rows, rows), 1)
    diff = jnp.where(t_idx > s_idx, t_idx ^ s_idx, 0)
    nt_dims = (((1,), (1,)), ((), ()))
    lb = lb_ref[...]
    gain = gain_ref[...]

    def heads(x):
        return [x[:, h * HG_DIM:(h + 1) * HG_DIM] for h in range(HG_HEADS)]

    def gates_and_scans(tok):
        w = HG_WIDTH
        q = hg_ref[tok, 0:w]
        f = lb + (1.0 - lb) * jax.nn.sigmoid(hg_ref[tok, w:2 * w])
        logf = jnp.log(f)
        return dict(tok=tok, qf=q * jax.nn.sigmoid(q), k=1.0 - f, logf=logf,
                    scans=_segmented_scans(logf, mats_ref))

    def intra_chunk(c):
        qf, k, logf, scans = c["qf"], c["k"], c["logf"], c["scans"]
        qb, kb = heads(qf.astype(BF16)), heads(k.astype(BF16))
        a = [jnp.where(t_idx == s_idx,
                       lax.dot_general(qb[h], kb[h], nt_dims, preferred_element_type=F32), 0.0)
             for h in range(HG_HEADS)]
        level, h_size = 0, 1
        while h_size < rows:
            pre, suf = scans[h_size]
            ql = heads((qf * jnp.exp(pre)).astype(BF16))
            kl = heads((k * jnp.exp(suf - logf)).astype(BF16))
            mask = (diff >> level) == 1
            for h in range(HG_HEADS):
                al = lax.dot_general(ql[h], kl[h], nt_dims, preferred_element_type=F32)
                a[h] = jnp.where(mask, al, a[h])
            level += 1
            h_size *= 2
        cum, rev = scans[rows]
        c.update(a=a, cum=cum,
                 q_in=heads((qf * jnp.exp(cum)).astype(BF16)),
                 k_out=heads((k * jnp.exp(rev - logf)).astype(BF16)))
        return c

    def outputs_and_state(c, states):
        tok, w = c["tok"], HG_WIDTH
        v = hg_ref[tok, 2 * w:3 * w]
        og = hg_ref[tok, 3 * w:4 * w]
        vh = heads(v.astype(BF16))
        new_states = []
        for h in range(HG_HEADS):
            lanes = slice(h * HG_DIM, (h + 1) * HG_DIM)
            state_t = states[h]
            o = jnp.dot(c["a"][h].astype(BF16), vh[h], preferred_element_type=F32)
            o = o + lax.dot_general(c["q_in"][h], state_t.astype(BF16), nt_dims,
                                    preferred_element_type=F32)
            decay = jnp.exp(c["cum"][rows - 1:rows, lanes])
            v_t = v[:, lanes].T.astype(BF16)
            new_states.append(
                state_t * decay + jnp.dot(v_t, c["k_out"][h], preferred_element_type=F32))
            ms = jnp.mean(o * o, axis=-1, keepdims=True)
            gate = og[:, lanes]
            o_ref[tok, lanes] = (o * lax.rsqrt(ms + EPS) * gain[:, lanes]
                                 * (gate * jax.nn.sigmoid(gate)))
        return new_states

    toks = [slice(j * rows, (j + 1) * rows) for j in range(hg_ref.shape[0] // rows)]
    chunks = [gates_and_scans(tok) for tok in toks]
    chunks = [intra_chunk(c) for c in chunks]
    states = [state_ref[h] for h in range(HG_HEADS)]
    for c in chunks:
        states = outputs_and_state(c, states)
    for h in range(HG_HEADS):
        state_ref[h] = states[h]


def _hgrn(hg_proj, lb, gain, batch, seq):
    n = hg_proj.shape[0]
    n_chunk = seq // HG_STEP
    mats = _scan_matrices(HG_CHUNK)
    row = lambda b, c: (b * n_chunk + c, 0)
    const = lambda b, c: (0, 0)
    return pl.pallas_call(
        _hgrn_kernel,
        grid=(batch, n_chunk),
        in_specs=[
            pl.BlockSpec((HG_STEP, 4 * HG_WIDTH), row),
            pl.BlockSpec((1, HG_WIDTH), const),
            pl.BlockSpec((1, HG_WIDTH), const),
            pl.BlockSpec(mats.shape, const),
        ],
        out_specs=pl.BlockSpec((HG_STEP, HG_WIDTH), row),
        out_shape=jax.ShapeDtypeStruct((n, HG_WIDTH), F32),
        scratch_shapes=[pltpu.VMEM((HG_HEADS, HG_DIM, HG_DIM), F32)],
        compiler_params=_cparams(2),
        name="hgrn",
    )(hg_proj, lb, gain, mats)


def _out_mlp_kernel(x_ref, yn_ref, yh_ref, gn_ref, gm_ref, wa_ref, wb_ref,
                    w1_ref, w2_ref, o_ref, *, ff_chunk):
    yn = yn_ref[...]
    ms = jnp.mean(yn * yn, axis=-1, keepdims=True)
    yn = (yn * lax.rsqrt(ms + EPS) * gn_ref[...]).astype(BF16)
    x1 = (x_ref[...]
          + jnp.dot(yn, wa_ref[...], preferred_element_type=F32)
          + jnp.dot(yh_ref[...].astype(BF16), wb_ref[...], preferred_element_type=F32))
    ms = jnp.mean(x1 * x1, axis=-1, keepdims=True)
    h = (x1 * lax.rsqrt(ms + EPS) * gm_ref[...]).astype(BF16)
    mlp = None
    for c in range(w1_ref.shape[1] // ff_chunk):
        cols = slice(c * ff_chunk, (c + 1) * ff_chunk)
        u = jnp.maximum(jnp.dot(h, w1_ref[:, cols], preferred_element_type=F32), 0.0)
        d = jnp.dot((u * u).astype(BF16), w2_ref[cols, :], preferred_element_type=F32)
        mlp = d if mlp is None else mlp + d
    o_ref[...] = x1 + mlp


def _out_mlp(x2, y_nsa, y_hg, gn, gm, wa, wb, w1, w2, tm=512, ff_chunk=1024):
    n, d = x2.shape
    row = lambda i: (i, 0)
    const = lambda i: (0, 0)

    def resident(shape):
        return pl.BlockSpec(shape, const, pipeline_mode=pl.Buffered(1))

    return pl.pallas_call(
        functools.partial(_out_mlp_kernel, ff_chunk=ff_chunk),
        grid=(n // tm,),
        in_specs=[
            pl.BlockSpec((tm, d), row),
            pl.BlockSpec((tm, y_nsa.shape[1]), row),
            pl.BlockSpec((tm, y_hg.shape[1]), row),
            pl.BlockSpec((1, y_nsa.shape[1]), const),
            pl.BlockSpec((1, d), const),
            resident(wa.shape),
            resident(wb.shape),
            resident(w1.shape),
            resident(w2.shape),
        ],
        out_specs=pl.BlockSpec((tm, d), row),
        out_shape=jax.ShapeDtypeStruct((n, d), F32),
        compiler_params=_cparams(1),
        name="out_mlp",
    )(x2, y_nsa, y_hg, gn, gm, wa, wb, w1, w2)


def _rope_tables(seq):
    pos = jnp.arange(seq, dtype=F32)
    inv_freq = ROPE_THETA ** (-jnp.arange(0, ROT_DIM, 2, dtype=F32) / ROT_DIM)
    ang = pos[:, None] * inv_freq[None, :]
    cos, sin = jnp.cos(ang), jnp.sin(ang)
    rest = HEAD_DIM - ROT_DIM
    cos_h = jnp.concatenate([cos, cos, jnp.ones((seq, rest), F32)], axis=-1)
    sin_h = jnp.concatenate([-sin, sin, jnp.zeros((seq, rest), F32)], axis=-1)
    return jnp.tile(cos_h, (1, 2)), jnp.tile(sin_h, (1, 2))


def _cmp_to_slc_t(n_cmp_rows, n_cmp, n_slc):
    tok = np.arange(n_cmp)[:, None] * CMP_STRIDE + np.arange(CMP_LEN)[None, :]
    frac = (tok[:, :, None] // SLC_BLOCK == np.arange(n_slc)[None, None, :]).mean(axis=1)
    out = np.zeros((SLC_LANES, n_cmp_rows), np.float32)
    out[:n_slc, :n_cmp] = frac.T
    return jnp.asarray(out, BF16)


def kernel(x, attn_norm, w_in, q_norm, k_norm, cmp_pe_k, cmp_w1_k, cmp_w2_k, cmp_pe_v,
           cmp_w1_v, cmp_w2_v, nsa_out_norm, hgrn_lb, hgrn_out_norm, w_out, mlp_norm,
           w_mlp_in, w_mlp_out):
    batch, seq, d_model = x.shape
    n = batch * seq
    n_slc = seq // SLC_BLOCK
    n_cmp_rows = seq // CMP_STRIDE
    n_cmp = (seq - CMP_LEN) // CMP_STRIDE + 1
    assert n_slc <= SLC_LANES and seq % SLC_KEYS == 0 and seq >= max(WINDOW + Q_TILE, 2 * SLC_KEYS)
    assert w_in.shape[0] == 1, "single-layer block"

    kvw = KV_WIDTH
    o_q, o_kc, o_vc, o_ks, o_vs, o_kw, o_vw, o_gate = np.cumsum(
        [0, NSA_WIDTH, kvw, kvw, kvw, kvw, kvw, kvw])
    o_hg = o_gate + NSA_HEADS * 3
    w = w_in[0]
    pad = jnp.zeros((d_model, NSA_COLS - (NSA_WIDTH + 6 * kvw + NSA_HEADS * 3)), w.dtype)
    w_nsa = jnp.concatenate(
        [w[:, o_q:o_kc], w[:, o_kc:o_vc], w[:, o_ks:o_vs], w[:, o_kw:o_vw],
         w[:, o_vc:o_ks], w[:, o_vs:o_kw], w[:, o_vw:o_gate], w[:, o_gate:o_hg], pad],
        axis=1).astype(BF16)
    w_hg = w[:, o_hg:].astype(BF16)
    x2 = x.reshape(n, d_model)
    cos_t, sin_t = _rope_tables(seq)
    qg = jnp.tile(q_norm[0], 2)[None, :]
    kg = jnp.tile(k_norm[0], 2)[None, :]
    hg_proj, qt, kc, ks_aug, kw, vc, vs_t, vw_t, gt = _inproj(
        x2, attn_norm[0][None, :], w_nsa, w_hg, cos_t, sin_t, qg, kg, batch, seq)


    blocks = lambda t: t.reshape(batch * NSA_GROUPS, n_cmp_rows, CMP_STRIDE * HEAD_DIM)
    pe8 = lambda pe: jnp.broadcast_to(pe.reshape(1, CMP_LEN * HEAD_DIM), (8, CMP_LEN * HEAD_DIM)).astype(BF16)
    k_cmp, v_cmp_t = _compress(
        blocks(kc), blocks(vc), pe8(cmp_pe_k[0]), pe8(cmp_pe_v[0]),
        cmp_w1_k[0].astype(BF16), cmp_w1_v[0].astype(BF16),
        cmp_w2_k[0].astype(BF16), cmp_w2_v[0].astype(BF16))
    k_cmp = k_cmp.reshape(batch, NSA_GROUPS, n_cmp_rows, HEAD_DIM)
    v_cmp_t = v_cmp_t.reshape(batch, NSA_GROUPS, V_ROWS, n_cmp_rows)
    y_nsa = _nsa_attn(qt, gt, k_cmp, v_cmp_t, _cmp_to_slc_t(n_cmp_rows, n_cmp, n_slc),
                      ks_aug, vs_t, kw, vw_t)

    lb_all = jnp.cumsum(jax.nn.softmax(hgrn_lb.astype(F32), axis=0), axis=0)
    y_hg = _hgrn(hg_proj, lb_all[0][None, :], jnp.tile(hgrn_out_norm[0], HG_HEADS)[None, :],
                 batch, seq)

    wo = w_out[0].astype(BF16)
    out = _out_mlp(x2, y_nsa, y_hg, nsa_out_norm[0][None, :], mlp_norm[0][None, :],
                   wo[:NSA_WIDTH], wo[NSA_WIDTH:], w_mlp_in[0].astype(BF16),
                   w_mlp_out[0].astype(BF16))
    return out.reshape(batch, seq, d_model)
```
